```python
import math
import jax
import jax.numpy as jnp
from jax import lax
import numpy as np

D_MODEL = 2048
BATCH = 2
SEQ = 4096
DEPTH = 2
DEC_BATCH = 8
DEC_SEQ = 8
PAST_LEN = 16384
PAGE_SIZE = 128

MIX_WIDTH = D_MODEL
ATTN_WIDTH = MIX_WIDTH // 2
CONV_DIM = MIX_WIDTH - ATTN_WIDTH
HEAD_DIM = 64
N_HEADS = ATTN_WIDTH // HEAD_DIM
KV_HEADS = 4
Q_PER_KV = N_HEADS // KV_HEADS
KV_WIDTH = KV_HEADS * HEAD_DIM
N_BRANCH = 3
BLOCK = 64
N_SEL = 16
WINDOW = 512
CONV_W = 3
D_FF = 4 * D_MODEL
ALPHA = (2.0 * DEPTH) ** 0.25
BETA = (8.0 * DEPTH) ** -0.25
LN_EPS = 1e-5
Q_BLOCK = 32
NEG = -1e30
FORCED_SCORE = 1e9
PROJ_SPLITS = (ATTN_WIDTH, 2 * KV_WIDTH, 2 * KV_WIDTH, 2 * KV_WIDTH, N_HEADS * N_BRANCH, CONV_DIM, CONV_DIM, CONV_DIM)
PROJ_WIDTH = sum(PROJ_SPLITS)

kernel_name = 'nsa_shortconv_hymba_deepnorm_step'


def layer_norm(x, g, b):
    xf = x.astype(jnp.float32)
    mu = jnp.mean(xf, axis=-1, keepdims=True)
    var = jnp.mean(jnp.square(xf - mu), axis=-1, keepdims=True)
    return ((xf - mu) * lax.rsqrt(var + LN_EPS) * g + b).astype(x.dtype)


def masked_softmax(s, mask):
    s = jnp.where(mask, s.astype(jnp.float32), NEG)
    m = jnp.max(s, axis=-1, keepdims=True)
    e = jnp.where(mask, jnp.exp(s - m), 0.0)
    return e / jnp.maximum(jnp.sum(e, axis=-1, keepdims=True), 1e-30)


def in_projection(h, w_in):
    u = jnp.einsum('btd,de->bte', h, w_in)
    B, T, _ = u.shape
    q, kv_c, kv_s, kv_w, g, xin, bg, cg = jnp.split(u, np.cumsum(PROJ_SPLITS)[:-1].tolist(), axis=-1)
    q = q.reshape(B, T, KV_HEADS, Q_PER_KV, HEAD_DIM)
    kv_c = kv_c.reshape(B, T, 2, KV_HEADS, HEAD_DIM)
    kv_s = kv_s.reshape(B, T, 2, KV_HEADS, HEAD_DIM)
    kv_w = kv_w.reshape(B, T, 2, KV_HEADS, HEAD_DIM)
    gates = jax.nn.sigmoid(g.reshape(B, T, KV_HEADS, Q_PER_KV, N_BRANCH))
    return q, kv_c, kv_s, kv_w, gates, xin, bg, cg


def nsa_attention(q, q_pos0, kv_cmp, kv_slc, kv_win, win_pos0, gates, w_cmp_k, w_cmp_v):
    B, Tq = q.shape[0], q.shape[1]
    L = kv_cmp.shape[1]
    scale = HEAD_DIM ** -0.5
    q_pos = q_pos0 + jnp.arange(Tq, dtype=jnp.int32)
    n_blk = -(-L // BLOCK)
    pad = n_blk * BLOCK - L
    blk = jnp.arange(n_blk, dtype=jnp.int32)
    pad_cfg = ((0, 0), (0, pad), (0, 0), (0, 0), (0, 0))
    kvc = jnp.pad(kv_cmp, pad_cfg).reshape(B, n_blk, BLOCK, 2, KV_HEADS, HEAD_DIM)
    k_cmp = jnp.einsum('bnlgd,ld->bngd', kvc[:, :, :, 0], w_cmp_k)
    v_cmp = jnp.einsum('bnlgd,ld->bngd', kvc[:, :, :, 1], w_cmp_v)
    s_c = jnp.einsum('bqgrd,bngd->bgrqn', q, k_cmp) * scale
    cmask = (blk[None, :] * BLOCK + BLOCK - 1) <= q_pos[:, None]
    p_c = masked_softmax(s_c, cmask)
    o_cmp = jnp.einsum('bgrqn,bngd->bqgrd', p_c.astype(v_cmp.dtype), v_cmp)
    cur = q_pos // BLOCK
    imp = jnp.sum(p_c, axis=2)
    forced = (blk[None, :] == 0) | (blk[None, :] == cur[:, None]) | (blk[None, :] == cur[:, None] - 1)
    cand = blk[None, :] <= cur[:, None]
    score = jnp.where(forced, FORCED_SCORE, jnp.where(cand, imp, -FORCED_SCORE))
    n_sel = min(N_SEL, n_blk)
    _, sel = lax.top_k(score, n_sel)
    qc = math.gcd(Tq, Q_BLOCK)
    n_chunk = Tq // qc
    kvs_blk = jnp.pad(kv_slc, pad_cfg).reshape(B, n_blk, BLOCK, 2, KV_HEADS, HEAD_DIM).transpose(0, 4, 1, 2, 3, 5)
    kvw_pad = jnp.pad(kv_win, ((0, 0), (WINDOW - 1, 0), (0, 0), (0, 0), (0, 0)))
    q_chunks = q.reshape(B, n_chunk, qc, KV_HEADS, Q_PER_KV, HEAD_DIM).swapaxes(0, 1)
    sel_chunks = sel.reshape(B, KV_HEADS, n_chunk, qc, n_sel).transpose(2, 0, 1, 3, 4)
    pos_chunks = q_pos.reshape(n_chunk, qc)
    b_idx = jnp.arange(B)[:, None, None, None]
    g_idx = jnp.arange(KV_HEADS)[None, :, None, None]
    offs = jnp.arange(BLOCK, dtype=jnp.int32)
    n_wk = WINDOW - 1 + qc
    w_offs = jnp.arange(n_wk, dtype=jnp.int32)

    def chunk_fn(args):
        qb, sb, pb = args
        kv_sel = kvs_blk[b_idx, g_idx, sb]
        k_pos = sb[..., None] * BLOCK + offs
        smask = (k_pos <= pb[None, None, :, None, None]).reshape(B, KV_HEADS, 1, qc, n_sel * BLOCK)
        s_s = jnp.einsum('bqgrd,bgqnkd->bgrqnk', qb, kv_sel[..., 0, :]) * scale
        p_s = masked_softmax(s_s.reshape(B, KV_HEADS, Q_PER_KV, qc, n_sel * BLOCK), smask)
        p_s = p_s.reshape(B, KV_HEADS, Q_PER_KV, qc, n_sel, BLOCK).astype(kv_sel.dtype)
        o_s = jnp.einsum('bgrqnk,bgqnkd->bqgrd', p_s, kv_sel[..., 1, :])
        start = pb[0] - win_pos0
        kvw = lax.dynamic_slice_in_dim(kvw_pad, start, n_wk, axis=1)
        w_pos = pb[0] - (WINDOW - 1) + w_offs
        delta = pb[:, None] - w_pos[None, :]
        wmask = (delta >= 0) & (delta < WINDOW) & (w_pos[None, :] >= win_pos0)
        s_w = jnp.einsum('bqgrd,bkgd->bgrqk', qb, kvw[:, :, 0]) * scale
        p_w = masked_softmax(s_w, wmask).astype(kvw.dtype)
        o_w = jnp.einsum('bgrqk,bkgd->bqgrd', p_w, kvw[:, :, 1])
        return o_s, o_w

    o_sel, o_win = lax.map(chunk_fn, (q_chunks, sel_chunks, pos_chunks))
    o_sel = o_sel.swapaxes(0, 1).reshape(B, Tq, KV_HEADS, Q_PER_KV, HEAD_DIM)
    o_win = o_win.swapaxes(0, 1).reshape(B, Tq, KV_HEADS, Q_PER_KV, HEAD_DIM)
    return gates[..., 0:1] * o_cmp + gates[..., 1:2] * o_sel + gates[..., 2:3] * o_win


def short_conv(xin, bg, cg, prev, conv_w):
    z = cg * xin
    T = z.shape[1]
    zp = jnp.concatenate([prev.astype(z.dtype), z], axis=1)
    y = conv_w[CONV_W - 1] * zp[:, CONV_W - 1:CONV_W - 1 + T]
    for j in range(CONV_W - 1):
        y = y + conv_w[j] * zp[:, j:j + T]
    return bg * y, zp[:, -(CONV_W - 1):]


def gather_pages(pool, page_table):
    pages = pool[page_table]
    return pages.reshape(page_table.shape[0], -1, *pool.shape[2:])


def hybrid_layer(x, past, w_in, w_cmp_k, w_cmp_v, conv_w, w_out, ln1_g, ln1_b, w_mlp1, w_mlp2, ln2_g, ln2_b):
    B, T, _ = x.shape
    q, kv_c, kv_s, kv_w, gates, xin, bg, cg = in_projection(x, w_in)
    if past is None:
        q_pos0 = 0
        kc_all, ks_all, kw_all = kv_c, kv_s, kv_w
        win_pos0 = 0
        conv_prev = jnp.zeros((B, CONV_W - 1, CONV_DIM), x.dtype)
        win_keep = min(WINDOW, T)
    else:
        pc, ps, pw, pconv = past
        q_pos0 = pc.shape[1]
        kc_all = jnp.concatenate([pc.astype(kv_c.dtype), kv_c], axis=1)
        ks_all = jnp.concatenate([ps.astype(kv_s.dtype), kv_s], axis=1)
        kw_all = jnp.concatenate([pw.astype(kv_w.dtype), kv_w], axis=1)
        win_pos0 = q_pos0 - pw.shape[1]
        conv_prev = pconv
        win_keep = pw.shape[1]
    attn = nsa_attention(q, q_pos0, kc_all, ks_all, kw_all, win_pos0, gates, w_cmp_k, w_cmp_v)
    conv_out, conv_state = short_conv(xin, bg, cg, conv_prev, conv_w)
    mix = jnp.einsum('bte,ed->btd', jnp.concatenate([attn.reshape(B, T, ATTN_WIDTH).astype(conv_out.dtype), conv_out], axis=-1), w_out)
    h = layer_norm(ALPHA * x + mix, ln1_g, ln1_b)
    f = jnp.einsum('btf,fd->btd', jnp.square(jax.nn.relu(jnp.einsum('btd,df->btf', h, w_mlp1))), w_mlp2)
    y = layer_norm(ALPHA * h + f, ln2_g, ln2_b)
    return y, kv_c, kv_s, kw_all[:, -win_keep:], conv_state


def setup_inputs(seed: int = 0) -> dict:
    key = jax.random.key(seed)
    ks = jax.random.split(key, 20)
    nrm = jax.random.normal
    f32 = jnp.float32
    n_pages = PAST_LEN // PAGE_SIZE
    n_pool = (DEC_BATCH * n_pages * 5) // 4
    win_buf = min(WINDOW, PAST_LEN)
    page_table = jax.random.permutation(ks[6], n_pool)[:DEC_BATCH * n_pages].reshape(DEC_BATCH, n_pages).astype(jnp.int32)
    return {
        'x_prompt': nrm(ks[0], (BATCH, SEQ, D_MODEL), f32),
        'x_sample': nrm(ks[1], (DEC_BATCH, DEC_SEQ, D_MODEL), f32),
        'cache_cmp': nrm(ks[2], (DEPTH, n_pool, PAGE_SIZE, 2, KV_HEADS, HEAD_DIM), f32),
        'cache_slc': nrm(ks[3], (DEPTH, n_pool, PAGE_SIZE, 2, KV_HEADS, HEAD_DIM), f32),
        'cache_win': nrm(ks[4], (DEPTH, DEC_BATCH, win_buf, 2, KV_HEADS, HEAD_DIM), f32),
        'state_conv': nrm(ks[5], (DEPTH, DEC_BATCH, CONV_W - 1, CONV_DIM), f32),
        'page_table': page_table,
        'w_in': nrm(ks[7], (DEPTH, D_MODEL, PROJ_WIDTH), f32) * D_MODEL ** -0.5,
        'w_cmp_k': (1.0 + 0.1 * nrm(ks[8], (DEPTH, BLOCK, HEAD_DIM), f32)) / BLOCK,
        'w_cmp_v': (1.0 + 0.1 * nrm(ks[9], (DEPTH, BLOCK, HEAD_DIM), f32)) / BLOCK,
        'conv_w': nrm(ks[10], (DEPTH, CONV_W, CONV_DIM), f32) * CONV_W ** -0.5,
        'w_out': nrm(ks[11], (DEPTH, MIX_WIDTH, D_MODEL), f32) * (MIX_WIDTH ** -0.5 * BETA),
        'ln1_g': 1.0 + 0.02 * nrm(ks[12], (DEPTH, D_MODEL), f32),
        'ln1_b': 0.02 * nrm(ks[13], (DEPTH, D_MODEL), f32),
        'w_mlp1': nrm(ks[14], (DEPTH, D_MODEL, D_FF), f32) * D_MODEL ** -0.5,
        'w_mlp2': nrm(ks[15], (DEPTH, D_FF, D_MODEL), f32) * (D_FF ** -0.5 * BETA),
        'ln2_g': 1.0 + 0.02 * nrm(ks[16], (DEPTH, D_MODEL), f32),
        'ln2_b': 0.02 * nrm(ks[17], (DEPTH, D_MODEL), f32),
    }


def reference(x_prompt, x_sample, cache_cmp, cache_slc, cache_win, state_conv, page_table, w_in, w_cmp_k, w_cmp_v, conv_w, w_out, ln1_g, ln1_b, w_mlp1, w_mlp2, ln2_g, ln2_b):
    yp, ys = x_prompt, x_sample
    cmp_p, slc_p, win_p, conv_p = [], [], [], []
    cmp_s, slc_s, win_s, conv_s = [], [], [], []
    for l in range(DEPTH):
        params = (w_in[l], w_cmp_k[l], w_cmp_v[l], conv_w[l], w_out[l], ln1_g[l], ln1_b[l], w_mlp1[l], w_mlp2[l], ln2_g[l], ln2_b[l])
        yp, c1, s1, w1, v1 = hybrid_layer(yp, None, *params)
        past = (gather_pages(cache_cmp[l], page_table), gather_pages(cache_slc[l], page_table), cache_win[l], state_conv[l])
        ys, c2, s2, w2, v2 = hybrid_layer(ys, past, *params)
        cmp_p.append(c1); slc_p.append(s1); win_p.append(w1); conv_p.append(v1)
        cmp_s.append(c2); slc_s.append(s2); win_s.append(w2); conv_s.append(v2)
    return (yp, ys, jnp.stack(cmp_p), jnp.stack(slc_p), jnp.stack(win_p), jnp.stack(conv_p), jnp.stack(cmp_s), jnp.stack(slc_s), jnp.stack(win_s), jnp.stack(conv_s))
```

```python
import functools

import jax
import jax.numpy as jnp
from jax import lax
from jax.experimental import pallas as pl
from jax.experimental.pallas import tpu as pltpu

HEAD_DIM = 64
KV_HEADS = 4
Q_PER_KV = 4
N_HEADS = KV_HEADS * Q_PER_KV
ATTN_WIDTH = N_HEADS * HEAD_DIM
KV_WIDTH = KV_HEADS * HEAD_DIM
N_BRANCH = 3
BLOCK = 64
BLOCK_SHIFT = BLOCK.bit_length() - 1
N_SEL = 16
WINDOW = 512
CONV_W = 3
LN_EPS = 1e-5
NEG = -1e30
FORCED_SCORE = 1e9
SCALE = HEAD_DIM ** -0.5
GATE_PAD = 128

SUBLANES = 8
VMEM_LIMIT = 56 * 1024 * 1024

BF16 = jnp.bfloat16
F32 = jnp.float32


def _cparams(sem):
    return pltpu.CompilerParams(dimension_semantics=sem, vmem_limit_bytes=VMEM_LIMIT)


def _layer_norm(v, g, b):
    mu = jnp.mean(v, axis=-1, keepdims=True)
    d = v - mu
    var = jnp.mean(d * d, axis=-1, keepdims=True)
    return d * lax.rsqrt(var + LN_EPS) * g + b


def _proj_kernel(x_ref, w_ref, q_ref, kc_ref, ks_ref, kw_ref, g_ref):
    x = x_ref[...].astype(BF16)

    def mm(c0, c1):
        return jnp.dot(x, w_ref[:, c0:c1], preferred_element_type=F32)

    c = ATTN_WIDTH
    q_ref[...] = (mm(0, c) * SCALE).astype(BF16)
    kc_ref[...] = mm(c, c + 2 * KV_WIDTH)
    ks_ref[...] = mm(c + 2 * KV_WIDTH, c + 4 * KV_WIDTH)
    kw_ref[...] = mm(c + 4 * KV_WIDTH, c + 6 * KV_WIDTH)
    logits = mm(c + 6 * KV_WIDTH, c + 6 * KV_WIDTH + GATE_PAD)
    g_ref[...] = 1.0 / (1.0 + jnp.exp(-logits))


def _proj(x, w_a, tm):
    n, d = x.shape
    pw = w_a.shape[1]
    row = lambda i: (i, 0)
    return pl.pallas_call(
        _proj_kernel,
        grid=(n // tm,),
        in_specs=[pl.BlockSpec((tm, d), row), pl.BlockSpec((d, pw), lambda i: (0, 0))],
        out_specs=[
            pl.BlockSpec((tm, ATTN_WIDTH), row),
            pl.BlockSpec((tm, 2 * KV_WIDTH), row),
            pl.BlockSpec((tm, 2 * KV_WIDTH), row),
            pl.BlockSpec((tm, 2 * KV_WIDTH), row),
            pl.BlockSpec((tm, GATE_PAD), row),
        ],
        out_shape=[
            jax.ShapeDtypeStruct((n, ATTN_WIDTH), BF16),
            jax.ShapeDtypeStruct((n, 2 * KV_WIDTH), F32),
            jax.ShapeDtypeStruct((n, 2 * KV_WIDTH), F32),
            jax.ShapeDtypeStruct((n, 2 * KV_WIDTH), F32),
            jax.ShapeDtypeStruct((n, GATE_PAD), F32),
        ],
        compiler_params=_cparams(("parallel",)),
        name="proj",
    )(x, w_a)


def _conv_from_z(z, zp, bgate, cw_ref, c0, c1):
    rows = z.shape[0]
    sub = lax.broadcasted_iota(jnp.int32, z.shape, 0) & (SUBLANES - 1)
    z1 = jnp.where(sub >= 1, pltpu.roll(z, 1, 0), pltpu.roll(zp, rows - (SUBLANES - 1), 0))
    z2 = jnp.where(sub >= 2, pltpu.roll(z, 2, 0), pltpu.roll(zp, rows - (SUBLANES - 2), 0))
    w0 = cw_ref[0:1, c0:c1]
    w1 = cw_ref[1:2, c0:c1]
    w2 = cw_ref[2:3, c0:c1]
    return bgate * (w2 * z + w0 * z2 + w1 * z1)


def _conv_prompt_kernel(x_ref, xh_ref, w_ref, cw_ref, y_ref, zl_ref, *, tiles_per_seq, chunk):
    i = pl.program_id(0)
    cdim = y_ref.shape[1]
    x = x_ref[...].astype(BF16)
    xh = xh_ref[...].astype(BF16)
    keep = jnp.where(i % tiles_per_seq == 0, 0.0, 1.0)
    for c0 in range(0, cdim, chunk):
        c1 = c0 + chunk

        def mm(a, off):
            return jnp.dot(a, w_ref[:, off + c0:off + c1], preferred_element_type=F32)

        z = mm(x, 2 * cdim) * mm(x, 0)
        zh = mm(xh, 2 * cdim) * mm(xh, 0) * keep
        bgate = mm(x, cdim)
        zp = jnp.concatenate([zh, z[:-SUBLANES]], axis=0)
        y_ref[:, c0:c1] = _conv_from_z(z, zp, bgate, cw_ref, c0, c1).astype(BF16)
        zl_ref[:, c0:c1] = z[-SUBLANES:]


def _conv_prompt(x, w_b, cw, seq, tm, chunk=512):
    n, d = x.shape
    cdim = cw.shape[1]
    slabs = tm // SUBLANES
    kern = functools.partial(_conv_prompt_kernel, tiles_per_seq=seq // tm, chunk=chunk)
    return pl.pallas_call(
        kern,
        grid=(n // tm,),
        in_specs=[
            pl.BlockSpec((tm, d), lambda i: (i, 0)),
            pl.BlockSpec((SUBLANES, d), lambda i: (jnp.maximum(i * slabs - 1, 0), 0)),
            pl.BlockSpec((d, 3 * cdim), lambda i: (0, 0)),
            pl.BlockSpec((CONV_W, cdim), lambda i: (0, 0)),
        ],
        out_specs=[
            pl.BlockSpec((tm, cdim), lambda i: (i, 0)),
            pl.BlockSpec((SUBLANES, cdim), lambda i: (i, 0)),
        ],
        out_shape=[
            jax.ShapeDtypeStruct((n, cdim), BF16),
            jax.ShapeDtypeStruct((n // tm * SUBLANES, cdim), F32),
        ],
        compiler_params=_cparams(("parallel",)),
        name="conv_prompt",
    )(x, x, w_b, cw)


def _conv_sample_kernel(x_ref, zp_ref, w_ref, cw_ref, y_ref, z_ref, *, chunk):
    cdim = y_ref.shape[1]
    x = x_ref[...].astype(BF16)
    for c0 in range(0, cdim, chunk):
        c1 = c0 + chunk

        def mm(off):
            return jnp.dot(x, w_ref[:, off + c0:off + c1], preferred_element_type=F32)

        z = mm(2 * cdim) * mm(0)
        bgate = mm(cdim)
        y_ref[:, c0:c1] = _conv_from_z(z, zp_ref[:, c0:c1], bgate, cw_ref, c0, c1).astype(BF16)
        z_ref[:, c0:c1] = z


def _conv_sample(x, zp, w_b, cw, chunk=512):
    n, d = x.shape
    cdim = cw.shape[1]
    full = lambda shape: pl.BlockSpec(shape, lambda i: (0, 0))
    return pl.pallas_call(
        functools.partial(_conv_sample_kernel, chunk=chunk),
        grid=(1,),
        in_specs=[full((n, d)), full((n, cdim)), full((d, 3 * cdim)), full((CONV_W, cdim))],
        out_specs=[full((n, cdim)), full((n, cdim))],
        out_shape=[jax.ShapeDtypeStruct((n, cdim), BF16), jax.ShapeDtypeStruct((n, cdim), F32)],
        compiler_params=_cparams(("arbitrary",)),
        name="conv_sample",
    )(x, zp, w_b, cw)


def _cmp_rows_kernel(kv_ref, w_ref, o_ref):
    nb = o_ref.shape[1]
    x = kv_ref[0].reshape(nb, BLOCK, kv_ref.shape[2])
    o_ref[0] = jnp.sum(x * w_ref[...][None], axis=1)


def _cmp_rows(kv, w_cmp, nb):
    b, l, c = kv.shape
    return pl.pallas_call(
        _cmp_rows_kernel,
        grid=(b, l // (nb * BLOCK)),
        in_specs=[
            pl.BlockSpec((1, nb * BLOCK, c), lambda i, j: (i, j, 0)),
            pl.BlockSpec((BLOCK, c), lambda i, j: (0, 0)),
        ],
        out_specs=pl.BlockSpec((1, nb, c), lambda i, j: (i, j, 0)),
        out_shape=jax.ShapeDtypeStruct((b, l // BLOCK, c), F32),
        compiler_params=_cparams(("parallel", "parallel")),
        name="cmp_rows",
    )(kv, w_cmp)


def _cmp_pages_kernel(pt_ref, *refs, pages_per_step):
    page_refs = refs[:pages_per_step]
    w_ref = refs[pages_per_step]
    o_ref = refs[pages_per_step + 1]
    w = w_ref[...][None]
    for j in range(pages_per_step):
        page = page_refs[j][0]
        x = page.reshape(page.shape[0] // BLOCK, BLOCK, page.shape[1])
        o_ref[0, j] = jnp.sum(x * w, axis=1)


def _page_specs(page_rows, c, pages_per_step):
    def spec(j):
        return pl.BlockSpec((1, page_rows, c), lambda b, s, pt: (pt[b, s * pages_per_step + j], 0, 0))
    return [spec(j) for j in range(pages_per_step)]


def _cmp_pages(pool, page_table, w_cmp, pages_per_step=8):
    _, page_rows, c = pool.shape
    b, n_pages = page_table.shape
    bpp = page_rows // BLOCK
    grid_spec = pltpu.PrefetchScalarGridSpec(
        num_scalar_prefetch=1,
        grid=(b, n_pages // pages_per_step),
        in_specs=_page_specs(page_rows, c, pages_per_step)
        + [pl.BlockSpec((BLOCK, c), lambda i, s, pt: (0, 0))],
        out_specs=pl.BlockSpec((1, pages_per_step, bpp, c), lambda i, s, pt: (i, s, 0, 0)),
    )
    out = pl.pallas_call(
        functools.partial(_cmp_pages_kernel, pages_per_step=pages_per_step),
        grid_spec=grid_spec,
        out_shape=jax.ShapeDtypeStruct((b, n_pages, bpp, c), F32),
        compiler_params=_cparams(("parallel", "parallel")),
        name="cmp_pages",
    )(page_table, *([pool] * pages_per_step), w_cmp)
    return out.reshape(b, n_pages * bpp, c)


def _masked_softmax0(s, mask):
    s = jnp.where(mask, s, NEG)
    m = jnp.max(s, axis=0, keepdims=True)
    e = jnp.where(mask, jnp.exp(s - m), 0.0)
    return e / jnp.maximum(jnp.sum(e, axis=0, keepdims=True), 1e-30)


def _topk_mask0(score, blk, n_sel):
    big = jnp.int32(2 ** 30)
    sel = jnp.zeros(score.shape, F32)
    for _ in range(n_sel):
        m = jnp.max(score, axis=0, keepdims=True)
        idx = jnp.min(jnp.where(score == m, blk, big), axis=0, keepdims=True)
        pick = blk == idx
        sel = jnp.where(pick, 1.0, sel)
        score = jnp.where(pick, -jnp.inf, score)
    return sel


def _select_blocks(imp, blk, cur, n_valid_blk):
    forced = (blk == 0) | (blk == cur) | (blk == cur - 1)
    cand = blk <= cur
    score = jnp.where(forced, FORCED_SCORE, jnp.where(cand, imp, -FORCED_SCORE))
    score = jnp.where(blk < n_valid_blk, score, -jnp.inf)
    return _topk_mask0(score, blk, min(N_SEL, n_valid_blk))


def _flash_update(s, mask, v_dot, m_ref, l_ref, acc_ref, idx):
    s = jnp.where(mask, s, NEG)
    m_old = m_ref[idx]
    m_new = jnp.maximum(m_old, jnp.max(s, axis=0, keepdims=True))
    alpha = jnp.exp(m_old - m_new)
    p = jnp.where(mask, jnp.exp(s - m_new), 0.0)
    l_ref[idx] = alpha * l_ref[idx] + jnp.sum(p, axis=0, keepdims=True)
    acc_ref[idx] = alpha * acc_ref[idx] + v_dot(p.astype(BF16))
    m_ref[idx] = m_new


def _flash_init(m_ref, l_ref, acc_ref):
    m_ref[...] = jnp.full(m_ref.shape, NEG, F32)
    l_ref[...] = jnp.zeros(l_ref.shape, F32)
    acc_ref[...] = jnp.zeros(acc_ref.shape, F32)


def _attn_prompt_kernel(qt_ref, kc_ref, vct_ref, ks_ref, vst_ref, kw_ref, vwt_ref, gt_ref,
                        o_ref, sel_ref, m_ref, l_ref, acc_ref, out_ref, *, tq, tk):
    qi = pl.program_id(2)
    q0 = qi * tq
    n_blk = kc_ref.shape[2]
    pos = q0 + lax.broadcasted_iota(jnp.int32, (1, tq), 1)

    blk = lax.broadcasted_iota(jnp.int32, (n_blk, tq), 0)
    cmask = blk * BLOCK + (BLOCK - 1) <= pos
    kc = kc_ref[0, 0]
    vct = vct_ref[0, 0]
    imp = jnp.zeros((n_blk, tq), F32)
    for r in range(Q_PER_KV):
        s = jnp.dot(kc, qt_ref[0, 0, r], preferred_element_type=F32)
        p = _masked_softmax0(s, cmask)
        imp = imp + p
        o_c = jnp.dot(vct, p.astype(BF16), preferred_element_type=F32)
        out_ref[r] = gt_ref[0, 0, r, 0:1, :] * o_c
    sel_ref[...] = _select_blocks(imp, blk, pos >> BLOCK_SHIFT, n_blk)

    kpos_tile = lax.broadcasted_iota(jnp.int32, (tk, tq), 0)
    bpt = tk // BLOCK

    def run_branch(k_ref, vt_ref, kt_lo, kt_hi, mask_fn, gate_idx):
        _flash_init(m_ref, l_ref, acc_ref)

        def body(kt, carry):
            mask = mask_fn(kt)
            k_tile = k_ref[0, 0, pl.ds(pl.multiple_of(kt * tk, tk), tk), :]
            vt = vt_ref[0, 0, kt]
            v_dot = lambda p: jnp.dot(vt, p, preferred_element_type=F32)
            for r in range(Q_PER_KV):
                s = jnp.dot(k_tile, qt_ref[0, 0, r], preferred_element_type=F32)
                _flash_update(s, mask, v_dot, m_ref, l_ref, acc_ref, r)
            return carry

        lax.fori_loop(kt_lo, kt_hi, body, 0)
        for r in range(Q_PER_KV):
            o = acc_ref[r] / jnp.maximum(l_ref[r], 1e-30)
            out_ref[r] = out_ref[r] + gt_ref[0, 0, r, gate_idx:gate_idx + 1, :] * o

    def sel_mask(kt):
        rows = [jnp.broadcast_to(sel_ref[pl.ds(kt * bpt + j, 1), :], (BLOCK, tq)) for j in range(bpt)]
        chosen = jnp.concatenate(rows, axis=0) > 0.5
        return chosen & (kt * tk + kpos_tile <= pos)

    def win_mask(kt):
        delta = pos - (kt * tk + kpos_tile)
        return (delta >= 0) & (delta < WINDOW)

    kt_end = (q0 + tq + tk - 1) // tk
    run_branch(ks_ref, vst_ref, 0, kt_end, sel_mask, 1)
    run_branch(kw_ref, vwt_ref, jnp.maximum(q0 - (WINDOW - 1), 0) // tk, kt_end, win_mask, 2)
    for r in range(Q_PER_KV):
        o_ref[0, 0, r] = out_ref[r].astype(o_ref.dtype)


def _attn_prompt(qt, kc, vct, ks, vst, kw, vwt, gt, tq, tk):
    b, g, r, d, t = qt.shape
    n_blk = kc.shape[2]
    nkt = t // tk
    bg = lambda *tail: (lambda i, j, q: (i, j) + tail)
    kern = functools.partial(_attn_prompt_kernel, tq=tq, tk=tk)
    return pl.pallas_call(
        kern,
        grid=(b, g, t // tq),
        in_specs=[
            pl.BlockSpec((1, 1, r, d, tq), lambda i, j, q: (i, j, 0, 0, q)),
            pl.BlockSpec((1, 1, n_blk, d), bg(0, 0)),
            pl.BlockSpec((1, 1, d, n_blk), bg(0, 0)),
            pl.BlockSpec((1, 1, t, d), bg(0, 0)),
            pl.BlockSpec((1, 1, nkt, d, tk), bg(0, 0, 0)),
            pl.BlockSpec((1, 1, t, d), bg(0, 0)),
            pl.BlockSpec((1, 1, nkt, d, tk), bg(0, 0, 0)),
            pl.BlockSpec((1, 1, r, N_BRANCH, tq), lambda i, j, q: (i, j, 0, 0, q)),
        ],
        out_specs=pl.BlockSpec((1, 1, r, d, tq), lambda i, j, q: (i, j, 0, 0, q)),
        out_shape=jax.ShapeDtypeStruct((b, g, r, d, t), BF16),
        scratch_shapes=[
            pltpu.VMEM((n_blk, tq), F32),
            pltpu.VMEM((r, 1, tq), F32),
            pltpu.VMEM((r, 1, tq), F32),
            pltpu.VMEM((r, d, tq), F32),
            pltpu.VMEM((r, d, tq), F32),
        ],
        compiler_params=_cparams(("parallel", "parallel", "parallel")),
        name="attn_prompt",
    )(qt, kc, vct, ks, vst, kw, vwt, gt)


def _attn_sample_kernel(pt_ref, *refs, pages_per_step, past_len, n_blk_valid):
    page_refs = refs[:pages_per_step]
    (qbd_ref, kc_ref, vc_ref, new_s_ref, win_ref, new_w_ref, gt_ref, pos_ref,
     o_ref, sel_ref, m_ref, l_ref, acc_ref, out_ref) = refs[pages_per_step:]
    step = pl.program_id(1)
    n_steps = pl.num_programs(1)
    ncol = qbd_ref.shape[2]
    kvw = qbd_ref.shape[1]
    qbd = qbd_ref[0]
    pos = pos_ref[...]
    group_cols = ncol // Q_PER_KV

    def kv_split(rows):
        return rows[:, :kvw].astype(BF16), rows[:, kvw:].astype(BF16)

    def v_dot_for(v):
        return lambda p: lax.dot_general(v, p, (((0,), (0,)), ((), ())), preferred_element_type=F32)

    def dense_tile(rows, mask):
        k, v = kv_split(rows)
        s = jnp.dot(k, qbd, preferred_element_type=F32)
        _flash_update(s, mask, v_dot_for(v), m_ref, l_ref, acc_ref, 0)

    @pl.when(step == 0)
    def _():
        n_blk = kc_ref.shape[1]
        blk = lax.broadcasted_iota(jnp.int32, (n_blk, ncol), 0)
        cmask = (blk * BLOCK + (BLOCK - 1) <= pos) & (blk < n_blk_valid)
        s = jnp.dot(kc_ref[0], qbd, preferred_element_type=F32)
        p = _masked_softmax0(s, cmask)
        o_c = v_dot_for(vc_ref[0])(p.astype(BF16))
        out_ref[...] = gt_ref[0, 0:1, :] * o_c
        imp = p
        for r in range(1, Q_PER_KV):
            imp = imp + pltpu.roll(p, r * group_cols, 1)
        sel_ref[...] = _select_blocks(imp, blk, pos >> BLOCK_SHIFT, n_blk_valid)

        _flash_init(m_ref, l_ref, acc_ref)
        n_win = win_ref.shape[1]
        n_new = new_w_ref.shape[1]
        for rows, p0 in ((new_w_ref[0], past_len), (win_ref[0], past_len - n_win)):
            kpos = p0 + lax.broadcasted_iota(jnp.int32, (rows.shape[0], ncol), 0)
            delta = pos - kpos
            dense_tile(rows, (delta >= 0) & (delta < WINDOW))
        o_w = acc_ref[0] / jnp.maximum(l_ref[0], 1e-30)
        out_ref[...] = out_ref[...] + gt_ref[0, 2:3, :] * o_w

        _flash_init(m_ref, l_ref, acc_ref)
        kpos = past_len + lax.broadcasted_iota(jnp.int32, (n_new, ncol), 0)
        chosen = sel_ref[pl.ds(past_len // BLOCK, 1), :] > 0.5
        dense_tile(new_s_ref[0], chosen & (kpos <= pos))

    for j in range(pages_per_step):
        rows = page_refs[j][0]
        page_rows = rows.shape[0]
        pg = step * pages_per_step + j
        bpp = page_rows // BLOCK
        chosen = jnp.concatenate(
            [jnp.broadcast_to(sel_ref[pl.ds(pg * bpp + i, 1), :], (BLOCK, ncol)) for i in range(bpp)],
            axis=0) > 0.5
        kpos = pg * page_rows + lax.broadcasted_iota(jnp.int32, (page_rows, ncol), 0)
        dense_tile(rows, chosen & (kpos <= pos))

    @pl.when(step == n_steps - 1)
    def _():
        o_s = acc_ref[0] / jnp.maximum(l_ref[0], 1e-30)
        o_ref[0] = out_ref[...] + gt_ref[0, 1:2, :] * o_s


def _attn_sample(pool, page_table, qbd, kc, vc, new_s, win, new_w, gt, pos, pages_per_step=8):
    _, page_rows, c = pool.shape
    b, n_pages = page_table.shape
    _, kvw, ncol = qbd.shape
    n_blk = kc.shape[1]
    past_len = n_pages * page_rows
    per_b = lambda shape: pl.BlockSpec((1,) + shape, lambda i, s, pt: (i,) + (0,) * len(shape))
    grid_spec = pltpu.PrefetchScalarGridSpec(
        num_scalar_prefetch=1,
        grid=(b, n_pages // pages_per_step),
        in_specs=_page_specs(page_rows, c, pages_per_step) + [
            per_b((kvw, ncol)),
            per_b((n_blk, kvw)),
            per_b((n_blk, kvw)),
            per_b(new_s.shape[1:]),
            per_b(win.shape[1:]),
            per_b(new_w.shape[1:]),
            per_b((N_BRANCH, ncol)),
            pl.BlockSpec((1, ncol), lambda i, s, pt: (0, 0)),
        ],
        out_specs=per_b((kvw, ncol)),
        scratch_shapes=[
            pltpu.VMEM((n_blk, ncol), F32),
            pltpu.VMEM((1, 1, ncol), F32),
            pltpu.VMEM((1, 1, ncol), F32),
            pltpu.VMEM((1, kvw, ncol), F32),
            pltpu.VMEM((kvw, ncol), F32),
        ],
    )
    kern = functools.partial(_attn_sample_kernel, pages_per_step=pages_per_step, past_len=past_len,
                             n_blk_valid=past_len // BLOCK + 1)
    return pl.pallas_call(
        kern,
        grid_spec=grid_spec,
        out_shape=jax.ShapeDtypeStruct((b, kvw, ncol), F32),
        compiler_params=_cparams(("parallel", "arbitrary")),
        name="attn_sample",
    )(page_table, *([pool] * pages_per_step), qbd, kc, vc, new_s, win, new_w, gt, pos)


def _outproj_kernel(a_ref, c_ref, x_ref, w_ref, g_ref, b_ref, h_ref, *, alpha):
    aw = a_ref.shape[1]
    mix = jnp.dot(a_ref[...], w_ref[:aw, :], preferred_element_type=F32)
    mix = mix + jnp.dot(c_ref[...], w_ref[aw:, :], preferred_element_type=F32)
    h_ref[...] = _layer_norm(alpha * x_ref[...] + mix, g_ref[...], b_ref[...])


def _outproj_ln(attn, conv, x, w_out, g, b, alpha, tm):
    n, d = x.shape
    row = lambda w: pl.BlockSpec((tm, w), lambda i: (i, 0))
    const = lambda shape: pl.BlockSpec(shape, lambda i: (0, 0))
    return pl.pallas_call(
        functools.partial(_outproj_kernel, alpha=alpha),
        grid=(n // tm,),
        in_specs=[row(attn.shape[1]), row(conv.shape[1]), row(d), const(w_out.shape), const((1, d)), const((1, d))],
        out_specs=row(d),
        out_shape=jax.ShapeDtypeStruct((n, d), F32),
        compiler_params=_cparams(("parallel",)),
        name="outproj_ln",
    )(attn, conv, x, w_out, g, b)


def _mlp_kernel(h_ref, w1_ref, w2_ref, g_ref, b_ref, y_ref, acc_ref, hb_ref, *, alpha):
    f = pl.program_id(1)

    @pl.when(f == 0)
    def _():
        hb_ref[...] = h_ref[...].astype(BF16)
        acc_ref[...] = jnp.zeros(acc_ref.shape, F32)

    a = jnp.dot(hb_ref[...], w1_ref[...], preferred_element_type=F32)
    a = jnp.square(jnp.maximum(a, 0.0)).astype(BF16)
    acc_ref[...] += jnp.dot(a, w2_ref[...], preferred_element_type=F32)

    @pl.when(f == pl.num_programs(1) - 1)
    def _():
        y_ref[...] = _layer_norm(alpha * h_ref[...] + acc_ref[...], g_ref[...], b_ref[...])


def _mlp_ln(h, w1, w2, g, b, alpha, tm, tf):
    n, d = h.shape
    dff = w1.shape[1]
    return pl.pallas_call(
        functools.partial(_mlp_kernel, alpha=alpha),
        grid=(n // tm, dff // tf),
        in_specs=[
            pl.BlockSpec((tm, d), lambda i, f: (i, 0)),
            pl.BlockSpec((d, tf), lambda i, f: (0, f)),
            pl.BlockSpec((tf, d), lambda i, f: (f, 0)),
            pl.BlockSpec((1, d), lambda i, f: (0, 0)),
            pl.BlockSpec((1, d), lambda i, f: (0, 0)),
        ],
        out_specs=pl.BlockSpec((tm, d), lambda i, f: (i, 0)),
        out_shape=jax.ShapeDtypeStruct((n, d), F32),
        scratch_shapes=[pltpu.VMEM((tm, d), F32), pltpu.VMEM((tm, d), BF16)],
        compiler_params=_cparams(("parallel", "arbitrary")),
        name="mlp_ln",
    )(h, w1, w2, g, b)


def _split_kv(kv, b, t):
    return kv.reshape(b, t, 2, KV_HEADS, HEAD_DIM)


def _prompt_layer(x, p, b, t, alpha, tm, tq, tk):
    n = b * t
    q, kv_c, kv_s, kv_w, gates = _proj(x, p["w_a"], tm)
    conv, zlast = _conv_prompt(x, p["w_b"], p["conv_w"], t, tm)
    cmp = _cmp_rows(kv_c.reshape(b, t, 2 * KV_WIDTH), p["w_cmp"], 8)
    n_blk = t // BLOCK
    cmp = cmp.reshape(b, n_blk, 2, KV_HEADS, HEAD_DIM).astype(BF16)
    kc = cmp[:, :, 0].transpose(0, 2, 1, 3)
    vct = cmp[:, :, 1].transpose(0, 2, 3, 1)

    def k_layout(kv):
        return _split_kv(kv, b, t)[:, :, 0].astype(BF16).transpose(0, 2, 1, 3)

    def vt_layout(kv):
        v = _split_kv(kv, b, t)[:, :, 1].astype(BF16)
        return v.reshape(b, t // tk, tk, KV_HEADS, HEAD_DIM).transpose(0, 3, 1, 4, 2)

    qt = q.reshape(b, t, KV_HEADS, Q_PER_KV, HEAD_DIM).transpose(0, 2, 3, 4, 1)
    gt = gates[:, :N_HEADS * N_BRANCH].reshape(b, t, KV_HEADS, Q_PER_KV, N_BRANCH).transpose(0, 2, 3, 4, 1)
    ot = _attn_prompt(qt, kc, vct, k_layout(kv_s), vt_layout(kv_s), k_layout(kv_w), vt_layout(kv_w), gt, tq, tk)
    attn = ot.transpose(0, 4, 1, 2, 3).reshape(n, ATTN_WIDTH)
    h = _outproj_ln(attn, conv, x, p["w_out"], p["ln1_g"], p["ln1_b"], alpha, tm)
    y = _mlp_ln(h, p["w_mlp1"], p["w_mlp2"], p["ln2_g"], p["ln2_b"], alpha, tm, 1024)
    conv_state = zlast.reshape(b, t // tm, SUBLANES, -1)[:, -1, SUBLANES - (CONV_W - 1):]
    keep = min(WINDOW, t)
    return (y, _split_kv(kv_c, b, t), _split_kv(kv_s, b, t), _split_kv(kv_w, b, t)[:, t - keep:], conv_state)


def _sample_layer(x, p, cache_cmp, cache_slc, cache_win, state_conv, page_table, db, dt, alpha):
    n = db * dt
    n_pool, page_rows = cache_cmp.shape[:2]
    n_pages = page_table.shape[1]
    past_len = n_pages * page_rows
    c = 2 * KV_WIDTH
    q, kv_c, kv_s, kv_w, gates = _proj(x, p["w_a"], n)

    zp = jnp.pad(state_conv, ((0, 0), (SUBLANES - (CONV_W - 1), 0), (0, 0))).reshape(n, -1)
    conv, z = _conv_sample(x, zp, p["w_b"], p["conv_w"])
    conv_state = jnp.concatenate([state_conv, z.reshape(db, dt, -1)], axis=1)[:, -(CONV_W - 1):]

    pool_c = cache_cmp.reshape(n_pool, page_rows, c)
    pool_s = cache_slc.reshape(n_pool, page_rows, c)
    cmp_past = _cmp_pages(pool_c, page_table, p["w_cmp"])
    tail = jnp.pad(kv_c.reshape(db, dt, c), ((0, 0), (0, SUBLANES * BLOCK - dt), (0, 0)))
    cmp_tail = _cmp_rows(tail, p["w_cmp"], SUBLANES)
    cmp_all = jnp.concatenate([cmp_past, cmp_tail], axis=1).astype(BF16)
    kc, vc = cmp_all[..., :KV_WIDTH], cmp_all[..., KV_WIDTH:]

    q5 = q.reshape(db, dt, KV_HEADS, Q_PER_KV, HEAD_DIM).transpose(0, 2, 4, 3, 1)
    eye = jnp.eye(KV_HEADS, dtype=q.dtype)
    qbd = (q5[:, :, :, :, None, :] * eye[None, :, None, None, :, None]).reshape(db, KV_WIDTH, N_HEADS * dt)
    gt = gates[:, :N_HEADS * N_BRANCH].reshape(db, dt, KV_HEADS, Q_PER_KV, N_BRANCH)
    gt = gt.transpose(0, 4, 3, 2, 1).reshape(db, N_BRANCH, N_HEADS * dt)
    pos = jnp.tile(past_len + jnp.arange(dt, dtype=jnp.int32), N_HEADS).reshape(1, N_HEADS * dt)
    win = cache_win.reshape(db, -1, c)
    new_rows = lambda kv: jnp.pad(kv.reshape(db, dt, c), ((0, 0), (0, 2 * SUBLANES - dt), (0, 0)))
    o = _attn_sample(pool_s, page_table, qbd, kc, vc, new_rows(kv_s), win, new_rows(kv_w), gt, pos)
    o = o.reshape(db, KV_HEADS, HEAD_DIM, Q_PER_KV, KV_HEADS, dt)
    o = jnp.stack([o[:, g, :, :, g, :] for g in range(KV_HEADS)], axis=1)
    attn = o.transpose(0, 4, 1, 3, 2).reshape(n, ATTN_WIDTH).astype(BF16)

    h = _outproj_ln(attn, conv, x, p["w_out"], p["ln1_g"], p["ln1_b"], alpha, n)
    y = _mlp_ln(h, p["w_mlp1"], p["w_mlp2"], p["ln2_g"], p["ln2_b"], alpha, n, 1024)
    win_keep = cache_win.shape[1]
    new_win = jnp.concatenate([cache_win, _split_kv(kv_w, db, dt)], axis=1)[:, -win_keep:]
    return y, _split_kv(kv_c, db, dt), _split_kv(kv_s, db, dt), new_win, conv_state


def kernel(x_prompt, x_sample, cache_cmp, cache_slc, cache_win, state_conv, page_table, w_in, w_cmp_k, w_cmp_v,
           conv_w, w_out, ln1_g, ln1_b, w_mlp1, w_mlp2, ln2_g, ln2_b):
    depth = w_in.shape[0]
    b, t, d = x_prompt.shape
    db, dt, _ = x_sample.shape
    alpha = (2.0 * depth) ** 0.25
    qkvg = ATTN_WIDTH + 6 * KV_WIDTH + N_HEADS * N_BRANCH
    tm = min(512, t)
    tq = tk = min(256, t)

    yp = x_prompt.reshape(b * t, d)
    ys = x_sample.reshape(db * dt, d)
    outs = [[] for _ in range(8)]
    for l in range(depth):
        w_a = jnp.pad(w_in[l][:, :qkvg], ((0, 0), (0, GATE_PAD - N_HEADS * N_BRANCH))).astype(BF16)
        p = {
            "w_a": w_a,
            "w_b": w_in[l][:, qkvg:].astype(BF16),
            "w_cmp": jnp.concatenate([jnp.tile(w_cmp_k[l], (1, KV_HEADS)), jnp.tile(w_cmp_v[l], (1, KV_HEADS))], axis=1),
            "conv_w": conv_w[l],
            "w_out": w_out[l].astype(BF16),
            "ln1_g": ln1_g[l].reshape(1, d), "ln1_b": ln1_b[l].reshape(1, d),
            "w_mlp1": w_mlp1[l].astype(BF16), "w_mlp2": w_mlp2[l].astype(BF16),
            "ln2_g": ln2_g[l].reshape(1, d), "ln2_b": ln2_b[l].reshape(1, d),
        }
        yp, c1, s1, w1, v1 = _prompt_layer(yp, p, b, t, alpha, tm, tq, tk)
        ys, c2, s2, w2, v2 = _sample_layer(ys, p, cache_cmp[l], cache_slc[l], cache_win[l], state_conv[l],
                                           page_table, db, dt, alpha)
        for lst, v in zip(outs, (c1, s1, w1, v1, c2, s2, w2, v2)):
            lst.append(v)
    stacked = [jnp.stack(o) for o in outs]
    return (yp.reshape(b, t, d), ys.reshape(db, dt, d), *stacked)
```

```python
import functools
import math

import jax
import jax.numpy as jnp
from jax import lax
from jax.experimental import pallas as pl
from jax.experimental.pallas import tpu as pltpu

HEAD_DIM = 64
KV_HEADS = 4
Q_PER_KV = 4
N_HEADS = KV_HEADS * Q_PER_KV
ATTN_WIDTH = N_HEADS * HEAD_DIM
KV_WIDTH = KV_HEADS * HEAD_DIM
N_BRANCH = 3
BLOCK = 64
BLOCK_SHIFT = BLOCK.bit_length() - 1
N_SEL = 16
WINDOW = 512
CONV_W = 3
LN_EPS = 1e-5
NEG = -1e30
FORCED_SCORE = 1e9
Q_SCALE = HEAD_DIM ** -0.5 * math.log2(math.e)
GATE_PAD = 128

SUBLANES = 8
VMEM_LIMIT = 56 * 1024 * 1024

BF16 = jnp.bfloat16
F32 = jnp.float32


def _cparams(sem):
    return pltpu.CompilerParams(dimension_semantics=sem, vmem_limit_bytes=VMEM_LIMIT)


def _layer_norm(v, g, b):
    mu = jnp.mean(v, axis=-1, keepdims=True)
    d = v - mu
    var = jnp.mean(d * d, axis=-1, keepdims=True)
    return d * lax.rsqrt(var + LN_EPS) * g + b


def _proj_kernel(x_ref, w_ref, q_ref, kc_ref, ks_ref, kw_ref, g_ref):
    x = x_ref[...].astype(BF16)

    def mm(c0, c1):
        return jnp.dot(x, w_ref[:, c0:c1], preferred_element_type=F32)

    c = ATTN_WIDTH
    q_ref[...] = (mm(0, c) * Q_SCALE).astype(BF16)
    kc_ref[...] = mm(c, c + 2 * KV_WIDTH)
    ks_ref[...] = mm(c + 2 * KV_WIDTH, c + 4 * KV_WIDTH)
    kw_ref[...] = mm(c + 4 * KV_WIDTH, c + 6 * KV_WIDTH)
    logits = mm(c + 6 * KV_WIDTH, c + 6 * KV_WIDTH + GATE_PAD)
    g_ref[...] = 1.0 / (1.0 + jnp.exp(-logits))


def _proj(x, w_a, tm):
    n, d = x.shape
    pw = w_a.shape[1]
    row = lambda i: (i, 0)
    return pl.pallas_call(
        _proj_kernel,
        grid=(n // tm,),
        in_specs=[pl.BlockSpec((tm, d), row), pl.BlockSpec((d, pw), lambda i: (0, 0))],
        out_specs=[
            pl.BlockSpec((tm, ATTN_WIDTH), row),
            pl.BlockSpec((tm, 2 * KV_WIDTH), row),
            pl.BlockSpec((tm, 2 * KV_WIDTH), row),
            pl.BlockSpec((tm, 2 * KV_WIDTH), row),
            pl.BlockSpec((tm, GATE_PAD), row),
        ],
        out_shape=[
            jax.ShapeDtypeStruct((n, ATTN_WIDTH), BF16),
            jax.ShapeDtypeStruct((n, 2 * KV_WIDTH), F32),
            jax.ShapeDtypeStruct((n, 2 * KV_WIDTH), F32),
            jax.ShapeDtypeStruct((n, 2 * KV_WIDTH), F32),
            jax.ShapeDtypeStruct((n, GATE_PAD), F32),
        ],
        compiler_params=_cparams(("parallel",)),
        name="proj",
    )(x, w_a)


def _conv_from_z(z, zp, bgate, cw_ref, c0, c1):
    rows = z.shape[0]
    sub = lax.broadcasted_iota(jnp.int32, z.shape, 0) & (SUBLANES - 1)
    z1 = jnp.where(sub >= 1, pltpu.roll(z, 1, 0), pltpu.roll(zp, rows - (SUBLANES - 1), 0))
    z2 = jnp.where(sub >= 2, pltpu.roll(z, 2, 0), pltpu.roll(zp, rows - (SUBLANES - 2), 0))
    w0 = cw_ref[0:1, c0:c1]
    w1 = cw_ref[1:2, c0:c1]
    w2 = cw_ref[2:3, c0:c1]
    return bgate * (w2 * z + w0 * z2 + w1 * z1)


def _conv_prompt_kernel(x_ref, xh_ref, w_ref, cw_ref, y_ref, zl_ref, *, tiles_per_seq, chunk):
    i = pl.program_id(0)
    cdim = y_ref.shape[1]
    x = x_ref[...].astype(BF16)
    xh = xh_ref[...].astype(BF16)
    keep = jnp.where(i % tiles_per_seq == 0, 0.0, 1.0)
    for c0 in range(0, cdim, chunk):
        c1 = c0 + chunk

        def mm(a, off):
            return jnp.dot(a, w_ref[:, off + c0:off + c1], preferred_element_type=F32)

        z = mm(x, 2 * cdim) * mm(x, 0)
        zh = mm(xh, 2 * cdim) * mm(xh, 0) * keep
        bgate = mm(x, cdim)
        zp = jnp.concatenate([zh, z[:-SUBLANES]], axis=0)
        y_ref[:, c0:c1] = _conv_from_z(z, zp, bgate, cw_ref, c0, c1).astype(BF16)
        zl_ref[:, c0:c1] = z[-SUBLANES:]


def _conv_prompt(x, w_b, cw, seq, tm, chunk=512):
    n, d = x.shape
    cdim = cw.shape[1]
    slabs = tm // SUBLANES
    kern = functools.partial(_conv_prompt_kernel, tiles_per_seq=seq // tm, chunk=chunk)
    return pl.pallas_call(
        kern,
        grid=(n // tm,),
        in_specs=[
            pl.BlockSpec((tm, d), lambda i: (i, 0)),
            pl.BlockSpec((SUBLANES, d), lambda i: (jnp.maximum(i * slabs - 1, 0), 0)),
            pl.BlockSpec((d, 3 * cdim), lambda i: (0, 0)),
            pl.BlockSpec((CONV_W, cdim), lambda i: (0, 0)),
        ],
        out_specs=[
            pl.BlockSpec((tm, cdim), lambda i: (i, 0)),
            pl.BlockSpec((SUBLANES, cdim), lambda i: (i, 0)),
        ],
        out_shape=[
            jax.ShapeDtypeStruct((n, cdim), BF16),
            jax.ShapeDtypeStruct((n // tm * SUBLANES, cdim), F32),
        ],
        compiler_params=_cparams(("parallel",)),
        name="conv_prompt",
    )(x, x, w_b, cw)


def _conv_sample_kernel(x_ref, zp_ref, w_ref, cw_ref, y_ref, z_ref, *, chunk):
    cdim = y_ref.shape[1]
    x = x_ref[...].astype(BF16)
    for c0 in range(0, cdim, chunk):
        c1 = c0 + chunk

        def mm(off):
            return jnp.dot(x, w_ref[:, off + c0:off + c1], preferred_element_type=F32)

        z = mm(2 * cdim) * mm(0)
        bgate = mm(cdim)
        y_ref[:, c0:c1] = _conv_from_z(z, zp_ref[:, c0:c1], bgate, cw_ref, c0, c1).astype(BF16)
        z_ref[:, c0:c1] = z


def _conv_sample(x, zp, w_b, cw, chunk=512):
    n, d = x.shape
    cdim = cw.shape[1]
    full = lambda shape: pl.BlockSpec(shape, lambda i: (0, 0))
    return pl.pallas_call(
        functools.partial(_conv_sample_kernel, chunk=chunk),
        grid=(1,),
        in_specs=[full((n, d)), full((n, cdim)), full((d, 3 * cdim)), full((CONV_W, cdim))],
        out_specs=[full((n, cdim)), full((n, cdim))],
        out_shape=[jax.ShapeDtypeStruct((n, cdim), BF16), jax.ShapeDtypeStruct((n, cdim), F32)],
        compiler_params=_cparams(("arbitrary",)),
        name="conv_sample",
    )(x, zp, w_b, cw)


def _cmp_rows_kernel(kv_ref, w_ref, o_ref):
    nb = o_ref.shape[1]
    x = kv_ref[0].reshape(nb, BLOCK, kv_ref.shape[2])
    o_ref[0] = jnp.sum(x * w_ref[...][None], axis=1)


def _cmp_rows(kv, w_cmp, nb):
    b, l, c = kv.shape
    return pl.pallas_call(
        _cmp_rows_kernel,
        grid=(b, l // (nb * BLOCK)),
        in_specs=[
            pl.BlockSpec((1, nb * BLOCK, c), lambda i, j: (i, j, 0)),
            pl.BlockSpec((BLOCK, c), lambda i, j: (0, 0)),
        ],
        out_specs=pl.BlockSpec((1, nb, c), lambda i, j: (i, j, 0)),
        out_shape=jax.ShapeDtypeStruct((b, l // BLOCK, c), F32),
        compiler_params=_cparams(("parallel", "parallel")),
        name="cmp_rows",
    )(kv, w_cmp)


NT_DIMS = (((1,), (1,)), ((), ()))
TN_DIMS = (((0,), (0,)), ((), ()))


def _cmp_pages_kernel(pt_ref, *refs, pages_per_step):
    page_refs = refs[:pages_per_step]
    w_ref = refs[pages_per_step]
    o_ref = refs[pages_per_step + 1]
    w = w_ref[...]
    nb = o_ref.shape[1]
    page_rows = w.shape[1]
    bpp = page_rows // BLOCK
    out_blk = lax.broadcasted_iota(jnp.int32, (nb, page_rows), 0)
    row_blk = lax.broadcasted_iota(jnp.int32, (nb, page_rows), 1) >> BLOCK_SHIFT
    acc = jnp.zeros(o_ref.shape[1:], F32)
    for j in range(pages_per_step):
        pw = page_refs[j][0] * w
        hi = pw.astype(BF16)
        lo = (pw - hi.astype(F32)).astype(BF16)
        member = jnp.where(out_blk == j * bpp + row_blk, 1.0, 0.0).astype(BF16)
        acc = acc + lax.dot_general(member, hi, NT_DIMS, preferred_element_type=F32)
        acc = acc + lax.dot_general(member, lo, NT_DIMS, preferred_element_type=F32)
    o_ref[0] = acc


def _page_specs(c, page_rows, pages_per_step, page0):
    def spec(j):
        return pl.BlockSpec((1, c, page_rows), lambda b, s, pt: (page0 + pt[b, s * pages_per_step + j], 0, 0))
    return [spec(j) for j in range(pages_per_step)]


def _cmp_pages(pool_t, page0, page_table, w_cmp_t, pages_per_step=8):
    _, c, page_rows = pool_t.shape
    b, n_pages = page_table.shape
    nb = pages_per_step * page_rows // BLOCK
    grid_spec = pltpu.PrefetchScalarGridSpec(
        num_scalar_prefetch=1,
        grid=(b, n_pages // pages_per_step),
        in_specs=_page_specs(c, page_rows, pages_per_step, page0)
        + [pl.BlockSpec((c, page_rows), lambda i, s, pt: (0, 0))],
        out_specs=pl.BlockSpec((1, nb, c), lambda i, s, pt: (i, s, 0)),
    )
    return pl.pallas_call(
        functools.partial(_cmp_pages_kernel, pages_per_step=pages_per_step),
        grid_spec=grid_spec,
        out_shape=jax.ShapeDtypeStruct((b, n_pages * page_rows // BLOCK, c), F32),
        compiler_params=_cparams(("parallel", "parallel")),
        name="cmp_pages",
    )(page_table, *([pool_t] * pages_per_step), w_cmp_t)


def _masked_softmax0(s, mask):
    s = jnp.where(mask, s, NEG)
    m = jnp.max(s, axis=0, keepdims=True)
    e = jnp.where(mask, jnp.exp2(s - m), 0.0)
    return e / jnp.maximum(jnp.sum(e, axis=0, keepdims=True), 1e-30)


def _topk_mask0(score, blk, n_sel):
    big = jnp.int32(2 ** 30)
    sel = jnp.zeros(score.shape, F32)
    for _ in range(n_sel):
        m = jnp.max(score, axis=0, keepdims=True)
        idx = jnp.min(jnp.where(score == m, blk, big), axis=0, keepdims=True)
        pick = blk == idx
        sel = jnp.where(pick, 1.0, sel)
        score = jnp.where(pick, -jnp.inf, score)
    return sel


def _select_blocks(imp, blk, cur, n_valid_blk):
    forced = (blk == 0) | (blk == cur) | (blk == cur - 1)
    cand = blk <= cur
    score = jnp.where(forced, FORCED_SCORE, jnp.where(cand, imp, -FORCED_SCORE))
    score = jnp.where(blk < n_valid_blk, score, -jnp.inf)
    return _topk_mask0(score, blk, min(N_SEL, n_valid_blk))


def _sel_to_bias(sel):
    return (sel - 1.0) * (-NEG)


def _flash_update(s, bias, v_dot, m_ref, l_ref, acc_ref):
    s = s + bias
    m_old = m_ref[...]
    m_new = jnp.maximum(m_old, jnp.max(s, axis=0, keepdims=True))
    alpha = jnp.exp2(m_old - m_new)
    p = jnp.exp2(s - m_new)
    l_ref[...] = alpha * l_ref[...] + jnp.sum(p, axis=0, keepdims=True)
    acc_ref[...] = alpha * acc_ref[...] + v_dot(p.astype(BF16))
    m_ref[...] = m_new


def _flash_init(m_ref, l_ref, acc_ref):
    m_ref[...] = jnp.full(m_ref.shape, NEG, F32)
    l_ref[...] = jnp.zeros(l_ref.shape, F32)
    acc_ref[...] = jnp.zeros(acc_ref.shape, F32)


def _attn_prompt_kernel(qt_ref, kc_ref, vct_ref, ks_ref, vst_ref, kw_ref, vwt_ref, gt_ref,
                        o_ref, selb_ref, m_ref, l_ref, acc_ref, out_ref, *, tq, tk):
    qi = pl.program_id(2)
    q0 = qi * tq
    n_blk = kc_ref.shape[2]
    wide = Q_PER_KV * tq
    qt = qt_ref[0, 0, 0]
    pos = q0 + lax.broadcasted_iota(jnp.int32, (1, tq), 1)
    pos_w = q0 + (lax.broadcasted_iota(jnp.int32, (1, wide), 1) & (tq - 1))

    blk_w = lax.broadcasted_iota(jnp.int32, (n_blk, wide), 0)
    p = _masked_softmax0(jnp.dot(kc_ref[0, 0], qt, preferred_element_type=F32),
                         blk_w * BLOCK + (BLOCK - 1) <= pos_w)
    o_c = jnp.dot(vct_ref[0, 0], p.astype(BF16), preferred_element_type=F32)
    out_ref[...] = gt_ref[0, 0, 0, 0:1, :] * o_c
    imp = p[:, 0:tq]
    for r in range(1, Q_PER_KV):
        imp = imp + p[:, r * tq:(r + 1) * tq]
    blk = lax.broadcasted_iota(jnp.int32, (n_blk, tq), 0)
    selb_ref[...] = _sel_to_bias(_select_blocks(imp, blk, pos >> BLOCK_SHIFT, n_blk))

    kpos_tile = lax.broadcasted_iota(jnp.int32, (tk, tq), 0)
    bpt = tk // BLOCK

    def run_branch(k_ref, vt_ref, kt_lo, kt_hi, bias_fn, gate_idx):
        _flash_init(m_ref, l_ref, acc_ref)

        def scores(kt):
            k_tile = k_ref[0, 0, pl.ds(pl.multiple_of(kt * tk, tk), tk), :]
            return jnp.dot(k_tile, qt, preferred_element_type=F32)

        def body(kt, carry):
            bias = jnp.concatenate([bias_fn(kt)] * Q_PER_KV, axis=1)
            vt = vt_ref[0, 0, kt]
            _flash_update(scores(kt), bias, lambda pr: jnp.dot(vt, pr, preferred_element_type=F32),
                          m_ref, l_ref, acc_ref)
            return carry

        lax.fori_loop(kt_lo, kt_hi, body, 0)
        o = acc_ref[...] / jnp.maximum(l_ref[...], 1e-30)
        out_ref[...] = out_ref[...] + gt_ref[0, 0, 0, gate_idx:gate_idx + 1, :] * o

    def sel_bias(kt):
        rows = [jnp.broadcast_to(selb_ref[pl.ds(kt * bpt + j, 1), :], (BLOCK, tq)) for j in range(bpt)]
        return jnp.where(kt * tk + kpos_tile <= pos, jnp.concatenate(rows, axis=0), NEG)

    def win_bias(kt):
        delta = pos - (kt * tk + kpos_tile)
        return jnp.where((delta >= 0) & (delta < WINDOW), 0.0, NEG)

    kt_end = (q0 + tq + tk - 1) // tk
    run_branch(ks_ref, vst_ref, 0, kt_end, sel_bias, 1)
    run_branch(kw_ref, vwt_ref, jnp.maximum(q0 - (WINDOW - 1), 0) // tk, kt_end, win_bias, 2)
    o_ref[0, 0, 0] = out_ref[...].astype(o_ref.dtype)


def _attn_prompt(qt, kc, vct, ks, vst, kw, vwt, gt, tq, tk):
    b, g, nq, d, wide = qt.shape
    t = nq * tq
    assert tq & (tq - 1) == 0 and wide == Q_PER_KV * tq
    n_blk = kc.shape[2]
    nkt = t // tk
    bg = lambda *tail: (lambda i, j, q: (i, j) + tail)
    kern = functools.partial(_attn_prompt_kernel, tq=tq, tk=tk)
    return pl.pallas_call(
        kern,
        grid=(b, g, nq),
        in_specs=[
            pl.BlockSpec((1, 1, 1, d, wide), lambda i, j, q: (i, j, q, 0, 0)),
            pl.BlockSpec((1, 1, n_blk, d), bg(0, 0)),
            pl.BlockSpec((1, 1, d, n_blk), bg(0, 0)),
            pl.BlockSpec((1, 1, t, d), bg(0, 0)),
            pl.BlockSpec((1, 1, nkt, d, tk), bg(0, 0, 0)),
            pl.BlockSpec((1, 1, t, d), bg(0, 0)),
            pl.BlockSpec((1, 1, nkt, d, tk), bg(0, 0, 0)),
            pl.BlockSpec((1, 1, 1, N_BRANCH, wide), lambda i, j, q: (i, j, q, 0, 0)),
        ],
        out_specs=pl.BlockSpec((1, 1, 1, d, wide), lambda i, j, q: (i, j, q, 0, 0)),
        out_shape=jax.ShapeDtypeStruct((b, g, nq, d, wide), BF16),
        scratch_shapes=[
            pltpu.VMEM((n_blk, tq), F32),
            pltpu.VMEM((1, wide), F32),
            pltpu.VMEM((1, wide), F32),
            pltpu.VMEM((d, wide), F32),
            pltpu.VMEM((d, wide), F32),
        ],
        compiler_params=_cparams(("parallel", "parallel", "parallel")),
        name="attn_prompt",
    )(qt, kc, vct, ks, vst, kw, vwt, gt)


def _attn_sample_kernel(pt_ref, *refs, pages_per_step, past_len, n_blk_valid):
    page_refs = refs[:pages_per_step]
    (qbd_ref, kc_ref, vc_ref, new_s_ref, win_ref, new_w_ref, gt_ref, pos_ref,
     o_ref, selb_ref, m_ref, l_ref, acc_ref, out_ref) = refs[pages_per_step:]
    step = pl.program_id(1)
    n_steps = pl.num_programs(1)
    ncol = qbd_ref.shape[2]
    kvw = qbd_ref.shape[1]
    qbd = qbd_ref[0]
    pos = pos_ref[...]
    group_cols = ncol // Q_PER_KV

    def tile_rows(rows, bias):
        k, v = rows[:, :kvw].astype(BF16), rows[:, kvw:].astype(BF16)
        s = jnp.dot(k, qbd, preferred_element_type=F32)
        v_dot = lambda p: lax.dot_general(v, p, TN_DIMS, preferred_element_type=F32)
        _flash_update(s, bias, v_dot, m_ref, l_ref, acc_ref)

    def tile_chan(page, bias):
        kt, vt = page[:kvw].astype(BF16), page[kvw:].astype(BF16)
        s = lax.dot_general(kt, qbd, TN_DIMS, preferred_element_type=F32)
        v_dot = lambda p: jnp.dot(vt, p, preferred_element_type=F32)
        _flash_update(s, bias, v_dot, m_ref, l_ref, acc_ref)

    def window_bias(n_keys, p0):
        delta = pos - (p0 + lax.broadcasted_iota(jnp.int32, (n_keys, ncol), 0))
        return jnp.where((delta >= 0) & (delta < WINDOW), 0.0, NEG)

    @pl.when(step == 0)
    def _():
        n_blk = kc_ref.shape[1]
        blk = lax.broadcasted_iota(jnp.int32, (n_blk, ncol), 0)
        cmask = (blk * BLOCK + (BLOCK - 1) <= pos) & (blk < n_blk_valid)
        s = jnp.dot(kc_ref[0], qbd, preferred_element_type=F32)
        p = _masked_softmax0(s, cmask)
        o_c = lax.dot_general(vc_ref[0], p.astype(BF16), TN_DIMS, preferred_element_type=F32)
        out_ref[...] = gt_ref[0, 0:1, :] * o_c
        imp = p
        for r in range(1, Q_PER_KV):
            imp = imp + pltpu.roll(p, r * group_cols, 1)
        selb_ref[...] = _sel_to_bias(_select_blocks(imp, blk, pos >> BLOCK_SHIFT, n_blk_valid))

        _flash_init(m_ref, l_ref, acc_ref)
        n_win = win_ref.shape[2]
        tile_rows(new_w_ref[0], window_bias(new_w_ref.shape[1], past_len))
        tile_chan(win_ref[0], window_bias(n_win, past_len - n_win))
        o_w = acc_ref[...] / jnp.maximum(l_ref[...], 1e-30)
        out_ref[...] = out_ref[...] + gt_ref[0, 2:3, :] * o_w

        _flash_init(m_ref, l_ref, acc_ref)
        n_new = new_s_ref.shape[1]
        kpos = past_len + lax.broadcasted_iota(jnp.int32, (n_new, ncol), 0)
        chosen = jnp.broadcast_to(selb_ref[pl.ds(past_len // BLOCK, 1), :], (n_new, ncol))
        tile_rows(new_s_ref[0], jnp.where(kpos <= pos, chosen, NEG))

    for j in range(pages_per_step):
        page = page_refs[j][0]
        page_rows = page.shape[1]
        pg = step * pages_per_step + j
        bpp = page_rows // BLOCK
        chosen = jnp.concatenate(
            [jnp.broadcast_to(selb_ref[pl.ds(pg * bpp + i, 1), :], (BLOCK, ncol)) for i in range(bpp)], axis=0)
        kpos = pg * page_rows + lax.broadcasted_iota(jnp.int32, (page_rows, ncol), 0)
        tile_chan(page, jnp.where(kpos <= pos, chosen, NEG))

    @pl.when(step == n_steps - 1)
    def _():
        o_s = acc_ref[...] / jnp.maximum(l_ref[...], 1e-30)
        o_ref[0] = out_ref[...] + gt_ref[0, 1:2, :] * o_s


def _attn_sample(pool_t, page0, page_table, qbd, kc, vc, new_s, win_t, win0, new_w, gt, pos, pages_per_step=8):
    _, c, page_rows = pool_t.shape
    b, n_pages = page_table.shape
    _, kvw, ncol = qbd.shape
    n_blk = kc.shape[1]
    past_len = n_pages * page_rows
    per_b = lambda shape: pl.BlockSpec((1,) + shape, lambda i, s, pt: (i,) + (0,) * len(shape))
    grid_spec = pltpu.PrefetchScalarGridSpec(
        num_scalar_prefetch=1,
        grid=(b, n_pages // pages_per_step),
        in_specs=_page_specs(c, page_rows, pages_per_step, page0) + [
            per_b((kvw, ncol)),
            per_b((n_blk, kvw)),
            per_b((n_blk, kvw)),
            per_b(new_s.shape[1:]),
            pl.BlockSpec((1,) + win_t.shape[1:], lambda i, s, pt: (win0 + i, 0, 0)),
            per_b(new_w.shape[1:]),
            per_b((N_BRANCH, ncol)),
            pl.BlockSpec((1, ncol), lambda i, s, pt: (0, 0)),
        ],
        out_specs=per_b((kvw, ncol)),
        scratch_shapes=[
            pltpu.VMEM((n_blk, ncol), F32),
            pltpu.VMEM((1, ncol), F32),
            pltpu.VMEM((1, ncol), F32),
            pltpu.VMEM((kvw, ncol), F32),
            pltpu.VMEM((kvw, ncol), F32),
        ],
    )
    kern = functools.partial(_attn_sample_kernel, pages_per_step=pages_per_step, past_len=past_len,
                             n_blk_valid=past_len // BLOCK + 1)
    return pl.pallas_call(
        kern,
        grid_spec=grid_spec,
        out_shape=jax.ShapeDtypeStruct((b, kvw, ncol), F32),
        compiler_params=_cparams(("parallel", "arbitrary")),
        name="attn_sample",
    )(page_table, *([pool_t] * pages_per_step), qbd, kc, vc, new_s, win_t, new_w, gt, pos)


def _outproj_kernel(a_ref, c_ref, x_ref, w_ref, g_ref, b_ref, h_ref, *, alpha):
    aw = a_ref.shape[1]
    mix = jnp.dot(a_ref[...], w_ref[:aw, :], preferred_element_type=F32)
    mix = mix + jnp.dot(c_ref[...], w_ref[aw:, :], preferred_element_type=F32)
    h_ref[...] = _layer_norm(alpha * x_ref[...] + mix, g_ref[...], b_ref[...])


def _outproj_ln(attn, conv, x, w_out, g, b, alpha, tm):
    n, d = x.shape
    row = lambda w: pl.BlockSpec((tm, w), lambda i: (i, 0))
    const = lambda shape: pl.BlockSpec(shape, lambda i: (0, 0))
    return pl.pallas_call(
        functools.partial(_outproj_kernel, alpha=alpha),
        grid=(n // tm,),
        in_specs=[row(attn.shape[1]), row(conv.shape[1]), row(d), const(w_out.shape), const((1, d)), const((1, d))],
        out_specs=row(d),
        out_shape=jax.ShapeDtypeStruct((n, d), F32),
        compiler_params=_cparams(("parallel",)),
        name="outproj_ln",
    )(attn, conv, x, w_out, g, b)


def _mlp_kernel(h_ref, w1_ref, w2_ref, g_ref, b_ref, y_ref, acc_ref, hb_ref, *, alpha):
    f = pl.program_id(1)

    @pl.when(f == 0)
    def _():
        hb_ref[...] = h_ref[...].astype(BF16)
        acc_ref[...] = jnp.zeros(acc_ref.shape, F32)

    a = jnp.dot(hb_ref[...], w1_ref[...], preferred_element_type=F32)
    a = jnp.square(jnp.maximum(a, 0.0)).astype(BF16)
    acc_ref[...] += jnp.dot(a, w2_ref[...], preferred_element_type=F32)

    @pl.when(f == pl.num_programs(1) - 1)
    def _():
        y_ref[...] = _layer_norm(alpha * h_ref[...] + acc_ref[...], g_ref[...], b_ref[...])


def _mlp_ln(h, w1, w2, g, b, alpha, tm, tf):
    n, d = h.shape
    dff = w1.shape[1]
    return pl.pallas_call(
        functools.partial(_mlp_kernel, alpha=alpha),
        grid=(n // tm, dff // tf),
        in_specs=[
            pl.BlockSpec((tm, d), lambda i, f: (i, 0)),
            pl.BlockSpec((d, tf), lambda i, f: (0, f)),
            pl.BlockSpec((tf, d), lambda i, f: (f, 0)),
            pl.BlockSpec((1, d), lambda i, f: (0, 0)),
            pl.BlockSpec((1, d), lambda i, f: (0, 0)),
        ],
        out_specs=pl.BlockSpec((tm, d), lambda i, f: (i, 0)),
        out_shape=jax.ShapeDtypeStruct((n, d), F32),
        scratch_shapes=[pltpu.VMEM((tm, d), F32), pltpu.VMEM((tm, d), BF16)],
        compiler_params=_cparams(("parallel", "arbitrary")),
        name="mlp_ln",
    )(h, w1, w2, g, b)


def _split_kv(kv, b, t):
    return kv.reshape(b, t, 2, KV_HEADS, HEAD_DIM)


def _prompt_layer(x, p, b, t, alpha, tm, tq, tk):
    n = b * t
    q, kv_c, kv_s, kv_w, gates = _proj(x, p["w_a"], tm)
    conv, zlast = _conv_prompt(x, p["w_b"], p["conv_w"], t, tm)
    cmp = _cmp_rows(kv_c.reshape(b, t, 2 * KV_WIDTH), p["w_cmp"], 8)
    n_blk = t // BLOCK
    cmp = cmp.reshape(b, n_blk, 2, KV_HEADS, HEAD_DIM).astype(BF16)
    kc = cmp[:, :, 0].transpose(0, 2, 1, 3)
    vct = cmp[:, :, 1].transpose(0, 2, 3, 1)

    def k_layout(kv):
        return _split_kv(kv, b, t)[:, :, 0].astype(BF16).transpose(0, 2, 1, 3)

    def vt_layout(kv):
        v = _split_kv(kv, b, t)[:, :, 1].astype(BF16)
        return v.reshape(b, t // tk, tk, KV_HEADS, HEAD_DIM).transpose(0, 3, 1, 4, 2)

    nq = t // tq

    def tile_major(a, width):
        a = a.reshape(b, nq, tq, KV_HEADS, Q_PER_KV, width).transpose(0, 3, 1, 5, 4, 2)
        return a.reshape(b, KV_HEADS, nq, width, Q_PER_KV * tq)

    qt = tile_major(q, HEAD_DIM)
    gt = tile_major(gates[:, :N_HEADS * N_BRANCH], N_BRANCH)
    ot = _attn_prompt(qt, kc, vct, k_layout(kv_s), vt_layout(kv_s), k_layout(kv_w), vt_layout(kv_w), gt, tq, tk)
    ot = ot.reshape(b, KV_HEADS, nq, HEAD_DIM, Q_PER_KV, tq)
    attn = ot.transpose(0, 2, 5, 1, 4, 3).reshape(n, ATTN_WIDTH)
    h = _outproj_ln(attn, conv, x, p["w_out"], p["ln1_g"], p["ln1_b"], alpha, tm)
    y = _mlp_ln(h, p["w_mlp1"], p["w_mlp2"], p["ln2_g"], p["ln2_b"], alpha, tm, 1024)
    conv_state = zlast.reshape(b, t // tm, SUBLANES, -1)[:, -1, SUBLANES - (CONV_W - 1):]
    keep = min(WINDOW, t)
    return (y, _split_kv(kv_c, b, t), _split_kv(kv_s, b, t), _split_kv(kv_w, b, t)[:, t - keep:], conv_state)


def _sample_layer(x, p, pool_c, pool_s, page0, win_t, win0, cache_win, state_conv, page_table, db, dt, alpha):
    n = db * dt
    page_rows = pool_c.shape[2]
    n_pages = page_table.shape[1]
    past_len = n_pages * page_rows
    c = 2 * KV_WIDTH
    q, kv_c, kv_s, kv_w, gates = _proj(x, p["w_a"], n)

    zp = jnp.pad(state_conv, ((0, 0), (SUBLANES - (CONV_W - 1), 0), (0, 0))).reshape(n, -1)
    conv, z = _conv_sample(x, zp, p["w_b"], p["conv_w"])
    conv_state = jnp.concatenate([state_conv, z.reshape(db, dt, -1)], axis=1)[:, -(CONV_W - 1):]

    cmp_past = _cmp_pages(pool_c, page0, page_table, p["w_cmp_t"])
    tail = jnp.pad(kv_c.reshape(db, dt, c), ((0, 0), (0, SUBLANES * BLOCK - dt), (0, 0)))
    cmp_tail = _cmp_rows(tail, p["w_cmp"], SUBLANES)
    cmp_all = jnp.concatenate([cmp_past, cmp_tail], axis=1).astype(BF16)
    kc, vc = cmp_all[..., :KV_WIDTH], cmp_all[..., KV_WIDTH:]

    q5 = q.reshape(db, dt, KV_HEADS, Q_PER_KV, HEAD_DIM).transpose(0, 2, 4, 3, 1)
    eye = jnp.eye(KV_HEADS, dtype=q.dtype)
    qbd = (q5[:, :, :, :, None, :] * eye[None, :, None, None, :, None]).reshape(db, KV_WIDTH, N_HEADS * dt)
    gt = gates[:, :N_HEADS * N_BRANCH].reshape(db, dt, KV_HEADS, Q_PER_KV, N_BRANCH)
    gt = gt.transpose(0, 4, 3, 2, 1).reshape(db, N_BRANCH, N_HEADS * dt)
    pos = jnp.tile(past_len + jnp.arange(dt, dtype=jnp.int32), N_HEADS).reshape(1, N_HEADS * dt)
    new_rows = lambda kv: jnp.pad(kv.reshape(db, dt, c), ((0, 0), (0, 2 * SUBLANES - dt), (0, 0)))
    o = _attn_sample(pool_s, page0, page_table, qbd, kc, vc, new_rows(kv_s), win_t, win0, new_rows(kv_w), gt, pos)
    o = o.reshape(db, KV_HEADS, HEAD_DIM, Q_PER_KV, KV_HEADS, dt)
    o = jnp.stack([o[:, g, :, :, g, :] for g in range(KV_HEADS)], axis=1)
    attn = o.transpose(0, 4, 1, 3, 2).reshape(n, ATTN_WIDTH).astype(BF16)

    h = _outproj_ln(attn, conv, x, p["w_out"], p["ln1_g"], p["ln1_b"], alpha, n)
    y = _mlp_ln(h, p["w_mlp1"], p["w_mlp2"], p["ln2_g"], p["ln2_b"], alpha, n, 1024)
    win_keep = cache_win.shape[1]
    new_win = jnp.concatenate([cache_win, _split_kv(kv_w, db, dt)], axis=1)[:, -win_keep:]
    return y, _split_kv(kv_c, db, dt), _split_kv(kv_s, db, dt), new_win, conv_state


def kernel(x_prompt, x_sample, cache_cmp, cache_slc, cache_win, state_conv, page_table, w_in, w_cmp_k, w_cmp_v,
           conv_w, w_out, ln1_g, ln1_b, w_mlp1, w_mlp2, ln2_g, ln2_b):
    depth = w_in.shape[0]
    b, t, d = x_prompt.shape
    db, dt, _ = x_sample.shape
    alpha = (2.0 * depth) ** 0.25
    qkvg = ATTN_WIDTH + 6 * KV_WIDTH + N_HEADS * N_BRANCH
    tm = min(512, t)
    tq, tk = min(256, t), min(512, t)

    n_pool, page_rows = cache_cmp.shape[1:3]
    chan_major = lambda c: c.transpose(0, 1, 3, 4, 5, 2).reshape(c.shape[0] * c.shape[1], -1, c.shape[2])
    pool_c, pool_s, win_t = chan_major(cache_cmp), chan_major(cache_slc), chan_major(cache_win)

    yp = x_prompt.reshape(b * t, d)
    ys = x_sample.reshape(db * dt, d)
    outs = [[] for _ in range(8)]
    for l in range(depth):
        w_a = jnp.pad(w_in[l][:, :qkvg], ((0, 0), (0, GATE_PAD - N_HEADS * N_BRANCH))).astype(BF16)
        w_cmp = jnp.concatenate([jnp.tile(w_cmp_k[l], (1, KV_HEADS)), jnp.tile(w_cmp_v[l], (1, KV_HEADS))], axis=1)
        p = {
            "w_a": w_a,
            "w_b": w_in[l][:, qkvg:].astype(BF16),
            "w_cmp": w_cmp,
            "w_cmp_t": jnp.tile(w_cmp.T, (1, page_rows // BLOCK)),
            "conv_w": conv_w[l],
            "w_out": w_out[l].astype(BF16),
            "ln1_g": ln1_g[l].reshape(1, d), "ln1_b": ln1_b[l].reshape(1, d),
            "w_mlp1": w_mlp1[l].astype(BF16), "w_mlp2": w_mlp2[l].astype(BF16),
            "ln2_g": ln2_g[l].reshape(1, d), "ln2_b": ln2_b[l].reshape(1, d),
        }
        yp, c1, s1, w1, v1 = _prompt_layer(yp, p, b, t, alpha, tm, tq, tk)
        ys, c2, s2, w2, v2 = _sample_layer(ys, p, pool_c, pool_s, l * n_pool, win_t, l * db, cache_win[l],
                                           state_conv[l], page_table, db, dt, alpha)
        for lst, v in zip(outs, (c1, s1, w1, v1, c2, s2, w2, v2)):
            lst.append(v)
    stacked = [jnp.stack(o) for o in outs]
    return (yp.reshape(b, t, d), ys.reshape(db, dt, d), *stacked)
```

```python
import functools
import math

import jax
import jax.numpy as jnp
from jax import lax
from jax.experimental import pallas as pl
from jax.experimental.pallas import tpu as pltpu

HEAD_DIM = 64
KV_HEADS = 4
Q_PER_KV = 4
N_HEADS = KV_HEADS * Q_PER_KV
ATTN_WIDTH = N_HEADS * HEAD_DIM
KV_WIDTH = KV_HEADS * HEAD_DIM
N_BRANCH = 3
BLOCK = 64
BLOCK_SHIFT = BLOCK.bit_length() - 1
N_SEL = 16
WINDOW = 512
CONV_W = 3
LN_EPS = 1e-5
NEG = -1e30
FORCED_SCORE = 1e9
Q_SCALE = HEAD_DIM ** -0.5 * math.log2(math.e)
GATE_PAD = 128

SUBLANES = 8
VMEM_LIMIT = 56 * 1024 * 1024

BF16 = jnp.bfloat16
F32 = jnp.float32


def _cparams(sem):
    return pltpu.CompilerParams(dimension_semantics=sem, vmem_limit_bytes=VMEM_LIMIT)


def _layer_norm(v, g, b):
    mu = jnp.mean(v, axis=-1, keepdims=True)
    d = v - mu
    var = jnp.mean(d * d, axis=-1, keepdims=True)
    return d * lax.rsqrt(var + LN_EPS) * g + b


def _proj_kernel(x_ref, w_ref, q_ref, kc_ref, ks_ref, kw_ref, g_ref):
    x = x_ref[...].astype(BF16)

    def mm(c0, c1):
        return jnp.dot(x, w_ref[:, c0:c1], preferred_element_type=F32)

    c = ATTN_WIDTH
    q_ref[...] = (mm(0, c) * Q_SCALE).astype(BF16)
    kc_ref[...] = mm(c, c + 2 * KV_WIDTH)
    ks_ref[...] = mm(c + 2 * KV_WIDTH, c + 4 * KV_WIDTH)
    kw_ref[...] = mm(c + 4 * KV_WIDTH, c + 6 * KV_WIDTH)
    logits = mm(c + 6 * KV_WIDTH, c + 6 * KV_WIDTH + GATE_PAD)
    g_ref[...] = 1.0 / (1.0 + jnp.exp(-logits))


def _proj(x, w_a, tm):
    n, d = x.shape
    pw = w_a.shape[1]
    row = lambda i: (i, 0)
    return pl.pallas_call(
        _proj_kernel,
        grid=(n // tm,),
        in_specs=[pl.BlockSpec((tm, d), row), pl.BlockSpec((d, pw), lambda i: (0, 0))],
        out_specs=[
            pl.BlockSpec((tm, ATTN_WIDTH), row),
            pl.BlockSpec((tm, 2 * KV_WIDTH), row),
            pl.BlockSpec((tm, 2 * KV_WIDTH), row),
            pl.BlockSpec((tm, 2 * KV_WIDTH), row),
            pl.BlockSpec((tm, GATE_PAD), row),
        ],
        out_shape=[
            jax.ShapeDtypeStruct((n, ATTN_WIDTH), BF16),
            jax.ShapeDtypeStruct((n, 2 * KV_WIDTH), F32),
            jax.ShapeDtypeStruct((n, 2 * KV_WIDTH), F32),
            jax.ShapeDtypeStruct((n, 2 * KV_WIDTH), F32),
            jax.ShapeDtypeStruct((n, GATE_PAD), F32),
        ],
        compiler_params=_cparams(("parallel",)),
        name="proj",
    )(x, w_a)


NT_DIMS = (((1,), (1,)), ((), ()))
TN_DIMS = (((0,), (0,)), ((), ()))
K_AUG = 2 * HEAD_DIM
COL_GROUPS = 8


def _block_sums_t(kvt, w_t, n_out=None, blk0=0):
    rows = kvt.shape[1]
    nb = rows // BLOCK if n_out is None else n_out
    pw = kvt * w_t
    hi = pw.astype(BF16)
    lo = (pw - hi.astype(F32)).astype(BF16)
    member = jnp.where(lax.broadcasted_iota(jnp.int32, (nb, rows), 0)
                       == blk0 + (lax.broadcasted_iota(jnp.int32, (nb, rows), 1) >> BLOCK_SHIFT),
                       1.0, 0.0).astype(BF16)
    return (lax.dot_general(member, hi, NT_DIMS, preferred_element_type=F32)
            + lax.dot_general(member, lo, NT_DIMS, preferred_element_type=F32))


def _proj_prompt_kernel(x_ref, wt_ref, wk_ref, wc_ref, kvc_ref, kvs_ref, kvw_ref, vts_ref, vtw_ref,
                        kas_ref, kaw_ref, qt_ref, g_ref, cmp_ref, *, tq, tk):
    tm = x_ref.shape[0]
    x = x_ref[...].astype(BF16)

    def nt(r0, r1):
        return lax.dot_general(wt_ref[r0:r1, :], x, NT_DIMS, preferred_element_type=F32)

    c = ATTN_WIDTH
    kv2 = 2 * KV_WIDTH
    qall = (nt(0, c) * Q_SCALE).astype(BF16)
    pad_rows = jnp.zeros((K_AUG - HEAD_DIM, Q_PER_KV * tq), BF16)
    for g in range(KV_HEADS):
        for j in range(tm // tq):
            heads = [qall[(g * Q_PER_KV + r) * HEAD_DIM:(g * Q_PER_KV + r + 1) * HEAD_DIM, j * tq:(j + 1) * tq]
                     for r in range(Q_PER_KV)]
            qt_ref[0, g, j, 0:HEAD_DIM, :] = jnp.concatenate(heads, axis=1)
            qt_ref[0, g, j, HEAD_DIM:K_AUG, :] = pad_rows

    kvt_c = nt(c, c + kv2)
    kvc_ref[0] = kvt_c
    cmp_ref[0] = _block_sums_t(kvt_c, wc_ref[...])

    lane = lax.broadcasted_iota(jnp.int32, (tm, K_AUG), 1)
    row_blk = (lax.broadcasted_iota(jnp.int32, (tm, K_AUG), 0) >> BLOCK_SHIFT) & (tk // BLOCK - 1)
    one_hot = jnp.where(lane - HEAD_DIM == row_blk, 1.0, 0.0)
    for i, (kv_ref, vt_ref, ka_ref) in enumerate(((kvs_ref, vts_ref, kas_ref), (kvw_ref, vtw_ref, kaw_ref))):
        kvt = nt(c + (i + 1) * kv2, c + (i + 2) * kv2)
        kv_ref[0] = kvt
        for g in range(KV_HEADS):
            v_rows = kvt[KV_WIDTH + g * HEAD_DIM:KV_WIDTH + (g + 1) * HEAD_DIM]
            for jk in range(tm // tk):
                vt_ref[0, g, jk] = v_rows[:, jk * tk:(jk + 1) * tk].astype(BF16)
            w0 = (i * KV_HEADS + g) * K_AUG
            k_rows = jnp.dot(x, wk_ref[:, w0:w0 + K_AUG], preferred_element_type=F32)
            ka_ref[0, g] = (k_rows + one_hot).astype(BF16)

    logits = nt(c + 3 * kv2, c + 3 * kv2 + g_ref.shape[2])
    gates = 1.0 / (1.0 + jnp.exp(-logits))
    for j in range(tm // tq):
        g_ref[0, j] = gates[:, j * tq:(j + 1) * tq]


def _proj_prompt(x, w_t, w_k, w_c, b, t, tm, tq, tk):
    n, d = x.shape
    tiles = t // tm
    g_rows = w_t.shape[0] - ATTN_WIDTH - 6 * KV_WIDTH
    const = lambda a: pl.BlockSpec(a.shape, lambda i, j: (0,) * a.ndim, pipeline_mode=pl.Buffered(1))
    kvt_spec = pl.BlockSpec((1, 2 * KV_WIDTH, tm), lambda i, j: (i, 0, j))
    vt_spec = pl.BlockSpec((1, KV_HEADS, tm // tk, HEAD_DIM, tk), lambda i, j: (i, 0, j, 0, 0))
    ka_spec = pl.BlockSpec((1, KV_HEADS, tm, K_AUG), lambda i, j: (i, 0, j, 0))
    kvt_shape = jax.ShapeDtypeStruct((b, 2 * KV_WIDTH, t), F32)
    vt_shape = jax.ShapeDtypeStruct((b, KV_HEADS, t // tk, HEAD_DIM, tk), BF16)
    ka_shape = jax.ShapeDtypeStruct((b, KV_HEADS, t, K_AUG), BF16)
    wide = Q_PER_KV * tq
    return pl.pallas_call(
        functools.partial(_proj_prompt_kernel, tq=tq, tk=tk),
        grid=(b, tiles),
        in_specs=[pl.BlockSpec((tm, d), lambda i, j: (i * tiles + j, 0)), const(w_t), const(w_k), const(w_c)],
        out_specs=[
            kvt_spec, kvt_spec, kvt_spec, vt_spec, vt_spec, ka_spec, ka_spec,
            pl.BlockSpec((1, KV_HEADS, tm // tq, K_AUG, wide), lambda i, j: (i, 0, j, 0, 0)),
            pl.BlockSpec((1, tm // tq, g_rows, tq), lambda i, j: (i, j, 0, 0)),
            pl.BlockSpec((1, tm // BLOCK, 2 * KV_WIDTH), lambda i, j: (i, j, 0)),
        ],
        out_shape=[
            kvt_shape, kvt_shape, kvt_shape, vt_shape, vt_shape, ka_shape, ka_shape,
            jax.ShapeDtypeStruct((b, KV_HEADS, t // tq, K_AUG, wide), BF16),
            jax.ShapeDtypeStruct((b, t // tq, g_rows, tq), F32),
            jax.ShapeDtypeStruct((b, t // BLOCK, 2 * KV_WIDTH), F32),
        ],
        compiler_params=_cparams(("parallel", "parallel")),
        name="proj_prompt",
    )(x, w_t, w_k, w_c)


def _conv_from_z(z, zp, bgate, cw_ref, c0, c1):
    rows = z.shape[0]
    sub = lax.broadcasted_iota(jnp.int32, z.shape, 0) & (SUBLANES - 1)
    z1 = jnp.where(sub >= 1, pltpu.roll(z, 1, 0), pltpu.roll(zp, rows - (SUBLANES - 1), 0))
    z2 = jnp.where(sub >= 2, pltpu.roll(z, 2, 0), pltpu.roll(zp, rows - (SUBLANES - 2), 0))
    w0 = cw_ref[0:1, c0:c1]
    w1 = cw_ref[1:2, c0:c1]
    w2 = cw_ref[2:3, c0:c1]
    return bgate * (w2 * z + w0 * z2 + w1 * z1)


def _conv_prompt_kernel(x_ref, xh_ref, w_ref, cw_ref, y_ref, zl_ref, *, tiles_per_seq, chunk):
    i = pl.program_id(0)
    cdim = y_ref.shape[1]
    x = x_ref[...].astype(BF16)
    xh = xh_ref[...].astype(BF16)
    keep = jnp.where(i % tiles_per_seq == 0, 0.0, 1.0)
    for c0 in range(0, cdim, chunk):
        c1 = c0 + chunk

        def mm(a, off):
            return jnp.dot(a, w_ref[:, off + c0:off + c1], preferred_element_type=F32)

        z = mm(x, 2 * cdim) * mm(x, 0)
        zh = mm(xh, 2 * cdim) * mm(xh, 0) * keep
        bgate = mm(x, cdim)
        zp = jnp.concatenate([zh, z[:-SUBLANES]], axis=0)
        y_ref[:, c0:c1] = _conv_from_z(z, zp, bgate, cw_ref, c0, c1).astype(BF16)
        zl_ref[:, c0:c1] = z[-SUBLANES:]


def _conv_prompt(x, w_b, cw, seq, tm, chunk=512):
    n, d = x.shape
    cdim = cw.shape[1]
    slabs = tm // SUBLANES
    kern = functools.partial(_conv_prompt_kernel, tiles_per_seq=seq // tm, chunk=chunk)
    return pl.pallas_call(
        kern,
        grid=(n // tm,),
        in_specs=[
            pl.BlockSpec((tm, d), lambda i: (i, 0)),
            pl.BlockSpec((SUBLANES, d), lambda i: (jnp.maximum(i * slabs - 1, 0), 0)),
            pl.BlockSpec((d, 3 * cdim), lambda i: (0, 0)),
            pl.BlockSpec((CONV_W, cdim), lambda i: (0, 0)),
        ],
        out_specs=[
            pl.BlockSpec((tm, cdim), lambda i: (i, 0)),
            pl.BlockSpec((SUBLANES, cdim), lambda i: (i, 0)),
        ],
        out_shape=[
            jax.ShapeDtypeStruct((n, cdim), BF16),
            jax.ShapeDtypeStruct((n // tm * SUBLANES, cdim), F32),
        ],
        compiler_params=_cparams(("parallel",)),
        name="conv_prompt",
    )(x, x, w_b, cw)


def _conv_sample_kernel(x_ref, zp_ref, w_ref, cw_ref, y_ref, z_ref, *, chunk):
    cdim = y_ref.shape[1]
    x = x_ref[...].astype(BF16)
    for c0 in range(0, cdim, chunk):
        c1 = c0 + chunk

        def mm(off):
            return jnp.dot(x, w_ref[:, off + c0:off + c1], preferred_element_type=F32)

        z = mm(2 * cdim) * mm(0)
        bgate = mm(cdim)
        y_ref[:, c0:c1] = _conv_from_z(z, zp_ref[:, c0:c1], bgate, cw_ref, c0, c1).astype(BF16)
        z_ref[:, c0:c1] = z


def _conv_sample(x, zp, w_b, cw, chunk=512):
    n, d = x.shape
    cdim = cw.shape[1]
    full = lambda shape: pl.BlockSpec(shape, lambda i: (0, 0))
    return pl.pallas_call(
        functools.partial(_conv_sample_kernel, chunk=chunk),
        grid=(1,),
        in_specs=[full((n, d)), full((n, cdim)), full((d, 3 * cdim)), full((CONV_W, cdim))],
        out_specs=[full((n, cdim)), full((n, cdim))],
        out_shape=[jax.ShapeDtypeStruct((n, cdim), BF16), jax.ShapeDtypeStruct((n, cdim), F32)],
        compiler_params=_cparams(("arbitrary",)),
        name="conv_sample",
    )(x, zp, w_b, cw)


def _cmp_rows_kernel(kv_ref, w_ref, o_ref):
    nb = o_ref.shape[1]
    x = kv_ref[0].reshape(nb, BLOCK, kv_ref.shape[2])
    o_ref[0] = jnp.sum(x * w_ref[...][None], axis=1)


def _cmp_rows(kv, w_cmp, nb):
    b, l, c = kv.shape
    return pl.pallas_call(
        _cmp_rows_kernel,
        grid=(b, l // (nb * BLOCK)),
        in_specs=[
            pl.BlockSpec((1, nb * BLOCK, c), lambda i, j: (i, j, 0)),
            pl.BlockSpec((BLOCK, c), lambda i, j: (0, 0)),
        ],
        out_specs=pl.BlockSpec((1, nb, c), lambda i, j: (i, j, 0)),
        out_shape=jax.ShapeDtypeStruct((b, l // BLOCK, c), F32),
        compiler_params=_cparams(("parallel", "parallel")),
        name="cmp_rows",
    )(kv, w_cmp)


def _cmp_pages_kernel(pt_ref, *refs, pages_per_step):
    page_refs = refs[:pages_per_step]
    w_ref = refs[pages_per_step]
    o_ref = refs[pages_per_step + 1]
    w = w_ref[...]
    nb = o_ref.shape[1]
    bpp = w.shape[1] // BLOCK
    acc = jnp.zeros(o_ref.shape[1:], F32)
    for j in range(pages_per_step):
        acc = acc + _block_sums_t(page_refs[j][0], w, nb, j * bpp)
    o_ref[0] = acc


def _page_specs(c, page_rows, pages_per_step, page0):
    def spec(j):
        return pl.BlockSpec((1, c, page_rows), lambda b, s, pt: (page0 + pt[b, s * pages_per_step + j], 0, 0))
    return [spec(j) for j in range(pages_per_step)]


def _cmp_pages(pool_t, page0, page_table, w_cmp_t, pages_per_step=8):
    _, c, page_rows = pool_t.shape
    b, n_pages = page_table.shape
    nb = pages_per_step * page_rows // BLOCK
    grid_spec = pltpu.PrefetchScalarGridSpec(
        num_scalar_prefetch=1,
        grid=(b, n_pages // pages_per_step),
        in_specs=_page_specs(c, page_rows, pages_per_step, page0)
        + [pl.BlockSpec((c, page_rows), lambda i, s, pt: (0, 0))],
        out_specs=pl.BlockSpec((1, nb, c), lambda i, s, pt: (i, s, 0)),
    )
    return pl.pallas_call(
        functools.partial(_cmp_pages_kernel, pages_per_step=pages_per_step),
        grid_spec=grid_spec,
        out_shape=jax.ShapeDtypeStruct((b, n_pages * page_rows // BLOCK, c), F32),
        compiler_params=_cparams(("parallel", "parallel")),
        name="cmp_pages",
    )(page_table, *([pool_t] * pages_per_step), w_cmp_t)


def _masked_softmax0(s, mask):
    s = jnp.where(mask, s, NEG)
    m = jnp.max(s, axis=0, keepdims=True)
    e = jnp.where(mask, jnp.exp2(s - m), 0.0)
    return e / jnp.maximum(jnp.sum(e, axis=0, keepdims=True), 1e-30)


def _topk_mask0(score, blk, n_sel):
    big = jnp.int32(2 ** 30)
    sel = jnp.zeros(score.shape, F32)
    for _ in range(n_sel):
        m = jnp.max(score, axis=0, keepdims=True)
        idx = jnp.min(jnp.where(score == m, blk, big), axis=0, keepdims=True)
        pick = blk == idx
        sel = jnp.where(pick, 1.0, sel)
        score = jnp.where(pick, -jnp.inf, score)
    return sel


def _select_blocks(imp, blk, cur, n_valid_blk):
    forced = (blk == 0) | (blk == cur) | (blk == cur - 1)
    cand = blk <= cur
    score = jnp.where(forced, FORCED_SCORE, jnp.where(cand, imp, -FORCED_SCORE))
    score = jnp.where(blk < n_valid_blk, score, -jnp.inf)
    return _topk_mask0(score, blk, min(N_SEL, n_valid_blk))


def _sel_to_bias(sel):
    return (sel - 1.0) * (-NEG)


def _flash_update(s, bias, v_dot, m_ref, l_ref, acc_ref):
    if bias is not None:
        s = bias + s
    m_old = m_ref[...]
    m_new = jnp.maximum(m_old, jnp.max(s, axis=0, keepdims=True))
    alpha = jnp.exp2(m_old - m_new)
    p = jnp.exp2(s - m_new)
    l_ref[...] = alpha * l_ref[...] + jnp.sum(p, axis=0, keepdims=True)
    acc_ref[...] = alpha * acc_ref[...] + v_dot(p.astype(BF16))
    m_ref[...] = m_new


def _flash_init(m_ref, l_ref, acc_ref):
    m_ref[...] = jnp.full(m_ref.shape, NEG, F32)
    l_ref[...] = jnp.zeros(l_ref.shape, F32)
    acc_ref[...] = jnp.zeros(acc_ref.shape, F32)


def _attn_prompt_kernel(qt_ref, kc_ref, vct_ref, ks_ref, vst_ref, kw_ref, vwt_ref, g_ref,
                        o_ref, selb_ref, m_ref, l_ref, acc_ref, out_ref, *, tq, tk):
    gi = pl.program_id(1)
    qi = pl.program_id(2)
    q0 = qi * tq
    n_blk = kc_ref.shape[2]
    wide = Q_PER_KV * tq
    qt = qt_ref[0, 0, 0]
    q_rows = qt[0:HEAD_DIM]
    pos = q0 + lax.broadcasted_iota(jnp.int32, (1, tq), 1)
    pos_w = q0 + (lax.broadcasted_iota(jnp.int32, (1, wide), 1) & (tq - 1))

    def gate(branch):
        rows = [g_ref[0, 0, pl.ds((gi * Q_PER_KV + r) * N_BRANCH + branch, 1), :] for r in range(Q_PER_KV)]
        return jnp.concatenate(rows, axis=1)

    blk_w = lax.broadcasted_iota(jnp.int32, (n_blk, wide), 0)
    p = _masked_softmax0(jnp.dot(kc_ref[0, 0], q_rows, preferred_element_type=F32),
                         blk_w * BLOCK + (BLOCK - 1) <= pos_w)
    o_c = jnp.dot(vct_ref[0, 0], p.astype(BF16), preferred_element_type=F32)
    out_ref[...] = gate(0) * o_c
    imp = p[:, 0:tq]
    for r in range(1, Q_PER_KV):
        imp = imp + p[:, r * tq:(r + 1) * tq]
    blk = lax.broadcasted_iota(jnp.int32, (n_blk, tq), 0)
    selb = _sel_to_bias(_select_blocks(imp, blk, pos >> BLOCK_SHIFT, n_blk))
    for r in range(Q_PER_KV):
        selb_ref[:, r * tq:(r + 1) * tq] = selb

    kpos_tile = lax.broadcasted_iota(jnp.int32, (tk, tq), 0)
    bpt = tk // BLOCK
    bias_pad = jnp.zeros((K_AUG - HEAD_DIM - bpt, wide), F32)

    def tile_update(k_ref, vt_ref, kt, q_op, bias):
        k_tile = k_ref[0, 0, pl.ds(pl.multiple_of(kt * tk, tk), tk), :]
        vt = vt_ref[0, 0, kt]
        v_dot = lambda pr: jnp.dot(vt, pr, preferred_element_type=F32)
        cw = wide // COL_GROUPS

        def group_bias(h):
            if bias is None:
                return None
            if cw >= tq:
                return jnp.concatenate([bias] * (cw // tq), axis=1)
            return bias[:, (h * cw) % tq:(h * cw) % tq + cw]

        groups = [slice(h * cw, (h + 1) * cw) for h in range(COL_GROUPS)]
        scores = [jnp.dot(k_tile, q_op[:, cols], preferred_element_type=F32) for cols in groups]
        for h, (s, cols) in enumerate(zip(scores, groups)):
            _flash_update(s, group_bias(h), v_dot, m_ref.at[:, cols], l_ref.at[:, cols], acc_ref.at[:, cols])

    def finish(branch):
        o = acc_ref[...] / jnp.maximum(l_ref[...], 1e-30)
        out_ref[...] = out_ref[...] + gate(branch) * o

    def widen(bias):
        return bias

    def q_with_block_bias(kt):
        tile_bias = selb_ref[pl.ds(pl.multiple_of(kt * bpt, bpt), bpt), :]
        return jnp.concatenate([q_rows, jnp.concatenate([tile_bias, bias_pad], axis=0).astype(BF16)], axis=0)

    kt_diag = (q0 + tq - 1) // tk
    _flash_init(m_ref, l_ref, acc_ref)

    def sel_body(kt, carry):
        tile_update(ks_ref, vst_ref, kt, q_with_block_bias(kt), None)
        return carry

    lax.fori_loop(0, kt_diag, sel_body, 0)
    causal = jnp.where(kt_diag * tk + kpos_tile <= pos, 0.0, NEG)
    tile_update(ks_ref, vst_ref, kt_diag, q_with_block_bias(kt_diag), widen(causal))
    finish(1)

    _flash_init(m_ref, l_ref, acc_ref)

    def win_body(kt, carry):
        delta = pos - (kt * tk + kpos_tile)
        tile_update(kw_ref, vwt_ref, kt, qt, widen(jnp.where((delta >= 0) & (delta < WINDOW), 0.0, NEG)))
        return carry

    lax.fori_loop(jnp.maximum(q0 - (WINDOW - 1), 0) // tk, kt_diag + 1, win_body, 0)
    finish(2)

    for pair in range(Q_PER_KV // 2):
        two = jnp.concatenate([out_ref[:, (2 * pair + h) * tq:(2 * pair + h + 1) * tq] for h in range(2)], axis=0)
        o_ref[:, pair * 2 * HEAD_DIM:(pair + 1) * 2 * HEAD_DIM] = two.T.astype(o_ref.dtype)


def _attn_prompt(qt, kc, vct, kas, vst, kaw, vwt, gates, tq, tk):
    b, g, nq, ka, wide = qt.shape
    t = nq * tq
    assert tq & (tq - 1) == 0 and wide == Q_PER_KV * tq and ka == K_AUG
    n_blk = kc.shape[2]
    nkt = t // tk
    d = HEAD_DIM
    bg = lambda *tail: (lambda i, j, q: (i, j) + tail)
    kern = functools.partial(_attn_prompt_kernel, tq=tq, tk=tk)
    return pl.pallas_call(
        kern,
        grid=(b, g, nq),
        in_specs=[
            pl.BlockSpec((1, 1, 1, ka, wide), lambda i, j, q: (i, j, q, 0, 0)),
            pl.BlockSpec((1, 1, n_blk, d), bg(0, 0)),
            pl.BlockSpec((1, 1, d, n_blk), bg(0, 0)),
            pl.BlockSpec((1, 1, t, ka), bg(0, 0)),
            pl.BlockSpec((1, 1, nkt, d, tk), bg(0, 0, 0)),
            pl.BlockSpec((1, 1, t, ka), bg(0, 0)),
            pl.BlockSpec((1, 1, nkt, d, tk), bg(0, 0, 0)),
            pl.BlockSpec((1, 1, gates.shape[2], tq), lambda i, j, q: (i, q, 0, 0)),
        ],
        out_specs=pl.BlockSpec((tq, Q_PER_KV * d), lambda i, j, q: (i * nq + q, j)),
        out_shape=jax.ShapeDtypeStruct((b * t, ATTN_WIDTH), BF16),
        scratch_shapes=[
            pltpu.VMEM((n_blk, wide), F32),
            pltpu.VMEM((1, wide), F32),
            pltpu.VMEM((1, wide), F32),
            pltpu.VMEM((d, wide), F32),
            pltpu.VMEM((d, wide), F32),
        ],
        compiler_params=_cparams(("parallel", "parallel", "parallel")),
        name="attn_prompt",
    )(qt, kc, vct, kas, vst, kaw, vwt, gates)


def _attn_sample_kernel(pt_ref, *refs, pages_per_step, past_len, n_blk_valid):
    page_refs = refs[:pages_per_step]
    (qbd_ref, kc_ref, vc_ref, new_s_ref, win_ref, new_w_ref, gt_ref, pos_ref,
     o_ref, selb_ref, m_ref, l_ref, acc_ref, out_ref) = refs[pages_per_step:]
    step = pl.program_id(1)
    n_steps = pl.num_programs(1)
    ncol = qbd_ref.shape[2]
    kvw = qbd_ref.shape[1]
    qbd = qbd_ref[0]
    pos = pos_ref[...]
    group_cols = ncol // Q_PER_KV

    def tile_rows(rows, bias):
        k, v = rows[:, :kvw].astype(BF16), rows[:, kvw:].astype(BF16)
        s = jnp.dot(k, qbd, preferred_element_type=F32)
        v_dot = lambda p: lax.dot_general(v, p, TN_DIMS, preferred_element_type=F32)
        _flash_update(s, bias, v_dot, m_ref, l_ref, acc_ref)

    def tile_chan(page, bias):
        kt, vt = page[:kvw].astype(BF16), page[kvw:].astype(BF16)
        s = lax.dot_general(kt, qbd, TN_DIMS, preferred_element_type=F32)
        v_dot = lambda p: jnp.dot(vt, p, preferred_element_type=F32)
        _flash_update(s, bias, v_dot, m_ref, l_ref, acc_ref)

    def window_bias(n_keys, p0):
        delta = pos - (p0 + lax.broadcasted_iota(jnp.int32, (n_keys, ncol), 0))
        return jnp.where((delta >= 0) & (delta < WINDOW), 0.0, NEG)

    @pl.when(step == 0)
    def _():
        n_blk = kc_ref.shape[1]
        blk = lax.broadcasted_iota(jnp.int32, (n_blk, ncol), 0)
        cmask = (blk * BLOCK + (BLOCK - 1) <= pos) & (blk < n_blk_valid)
        s = jnp.dot(kc_ref[0], qbd, preferred_element_type=F32)
        p = _masked_softmax0(s, cmask)
        o_c = lax.dot_general(vc_ref[0], p.astype(BF16), TN_DIMS, preferred_element_type=F32)
        out_ref[...] = gt_ref[0, 0:1, :] * o_c
        imp = p
        for r in range(1, Q_PER_KV):
            imp = imp + pltpu.roll(p, r * group_cols, 1)
        selb_ref[...] = _sel_to_bias(_select_blocks(imp, blk, pos >> BLOCK_SHIFT, n_blk_valid))

        _flash_init(m_ref, l_ref, acc_ref)
        n_win = win_ref.shape[2]
        tile_rows(new_w_ref[0], window_bias(new_w_ref.shape[1], past_len))
        tile_chan(win_ref[0], window_bias(n_win, past_len - n_win))
        o_w = acc_ref[...] / jnp.maximum(l_ref[...], 1e-30)
        out_ref[...] = out_ref[...] + gt_ref[0, 2:3, :] * o_w

        _flash_init(m_ref, l_ref, acc_ref)
        n_new = new_s_ref.shape[1]
        kpos = past_len + lax.broadcasted_iota(jnp.int32, (n_new, ncol), 0)
        chosen = jnp.broadcast_to(selb_ref[pl.ds(past_len // BLOCK, 1), :], (n_new, ncol))
        tile_rows(new_s_ref[0], jnp.where(kpos <= pos, chosen, NEG))

    for j in range(pages_per_step):
        page = page_refs[j][0]
        page_rows = page.shape[1]
        pg = step * pages_per_step + j
        bpp = page_rows // BLOCK
        chosen = jnp.concatenate(
            [jnp.broadcast_to(selb_ref[pl.ds(pg * bpp + i, 1), :], (BLOCK, ncol)) for i in range(bpp)], axis=0)
        kpos = pg * page_rows + lax.broadcasted_iota(jnp.int32, (page_rows, ncol), 0)
        tile_chan(page, jnp.where(kpos <= pos, chosen, NEG))

    @pl.when(step == n_steps - 1)
    def _():
        o_s = acc_ref[...] / jnp.maximum(l_ref[...], 1e-30)
        o_ref[0] = out_ref[...] + gt_ref[0, 1:2, :] * o_s


def _attn_sample(pool_t, page0, page_table, qbd, kc, vc, new_s, win_t, win0, new_w, gt, pos, pages_per_step=8):
    _, c, page_rows = pool_t.shape
    b, n_pages = page_table.shape
    _, kvw, ncol = qbd.shape
    n_blk = kc.shape[1]
    past_len = n_pages * page_rows
    per_b = lambda shape: pl.BlockSpec((1,) + shape, lambda i, s, pt: (i,) + (0,) * len(shape))
    grid_spec = pltpu.PrefetchScalarGridSpec(
        num_scalar_prefetch=1,
        grid=(b, n_pages // pages_per_step),
        in_specs=_page_specs(c, page_rows, pages_per_step, page0) + [
            per_b((kvw, ncol)),
            per_b((n_blk, kvw)),
            per_b((n_blk, kvw)),
            per_b(new_s.shape[1:]),
            pl.BlockSpec((1,) + win_t.shape[1:], lambda i, s, pt: (win0 + i, 0, 0)),
            per_b(new_w.shape[1:]),
            per_b((N_BRANCH, ncol)),
            pl.BlockSpec((1, ncol), lambda i, s, pt: (0, 0)),
        ],
        out_specs=per_b((kvw, ncol)),
        scratch_shapes=[
            pltpu.VMEM((n_blk, ncol), F32),
            pltpu.VMEM((1, ncol), F32),
            pltpu.VMEM((1, ncol), F32),
            pltpu.VMEM((kvw, ncol), F32),
            pltpu.VMEM((kvw, ncol), F32),
        ],
    )
    kern = functools.partial(_attn_sample_kernel, pages_per_step=pages_per_step, past_len=past_len,
                             n_blk_valid=past_len // BLOCK + 1)
    return pl.pallas_call(
        kern,
        grid_spec=grid_spec,
        out_shape=jax.ShapeDtypeStruct((b, kvw, ncol), F32),
        compiler_params=_cparams(("parallel", "arbitrary")),
        name="attn_sample",
    )(page_table, *([pool_t] * pages_per_step), qbd, kc, vc, new_s, win_t, new_w, gt, pos)


def _outproj_kernel(a_ref, c_ref, x_ref, w_ref, g_ref, b_ref, h_ref, *, alpha):
    aw = a_ref.shape[1]
    mix = jnp.dot(a_ref[...], w_ref[:aw, :], preferred_element_type=F32)
    mix = mix + jnp.dot(c_ref[...], w_ref[aw:, :], preferred_element_type=F32)
    h_ref[...] = _layer_norm(alpha * x_ref[...] + mix, g_ref[...], b_ref[...])


def _outproj_ln(attn, conv, x, w_out, g, b, alpha, tm):
    n, d = x.shape
    row = lambda w: pl.BlockSpec((tm, w), lambda i: (i, 0))
    const = lambda shape: pl.BlockSpec(shape, lambda i: (0, 0))
    return pl.pallas_call(
        functools.partial(_outproj_kernel, alpha=alpha),
        grid=(n // tm,),
        in_specs=[row(attn.shape[1]), row(conv.shape[1]), row(d), const(w_out.shape), const((1, d)), const((1, d))],
        out_specs=row(d),
        out_shape=jax.ShapeDtypeStruct((n, d), F32),
        compiler_params=_cparams(("parallel",)),
        name="outproj_ln",
    )(attn, conv, x, w_out, g, b)


def _mlp_kernel(h_ref, w1_ref, w2_ref, g_ref, b_ref, y_ref, acc_ref, hb_ref, *, alpha):
    f = pl.program_id(1)

    @pl.when(f == 0)
    def _():
        hb_ref[...] = h_ref[...].astype(BF16)
        acc_ref[...] = jnp.zeros(acc_ref.shape, F32)

    a = jnp.dot(hb_ref[...], w1_ref[...], preferred_element_type=F32)
    a = jnp.square(jnp.maximum(a, 0.0)).astype(BF16)
    acc_ref[...] += jnp.dot(a, w2_ref[...], preferred_element_type=F32)

    @pl.when(f == pl.num_programs(1) - 1)
    def _():
        y_ref[...] = _layer_norm(alpha * h_ref[...] + acc_ref[...], g_ref[...], b_ref[...])


def _mlp_ln(h, w1, w2, g, b, alpha, tm, tf):
    n, d = h.shape
    dff = w1.shape[1]
    return pl.pallas_call(
        functools.partial(_mlp_kernel, alpha=alpha),
        grid=(n // tm, dff // tf),
        in_specs=[
            pl.BlockSpec((tm, d), lambda i, f: (i, 0)),
            pl.BlockSpec((d, tf), lambda i, f: (0, f)),
            pl.BlockSpec((tf, d), lambda i, f: (f, 0)),
            pl.BlockSpec((1, d), lambda i, f: (0, 0)),
            pl.BlockSpec((1, d), lambda i, f: (0, 0)),
        ],
        out_specs=pl.BlockSpec((tm, d), lambda i, f: (i, 0)),
        out_shape=jax.ShapeDtypeStruct((n, d), F32),
        scratch_shapes=[pltpu.VMEM((tm, d), F32), pltpu.VMEM((tm, d), BF16)],
        compiler_params=_cparams(("parallel", "arbitrary")),
        name="mlp_ln",
    )(h, w1, w2, g, b)


def _split_kv(kv, b, t):
    return kv.reshape(b, t, 2, KV_HEADS, HEAD_DIM)


def _prompt_layer(x, p, b, t, alpha, tm, tq, tk):
    w_c = jnp.tile(p["w_cmp"].T, (1, tm // BLOCK))
    kvt_c, kvt_s, kvt_w, vst, vwt, kas, kaw, qt, gates, cmp = _proj_prompt(
        x, p["w_t"], p["w_k"], w_c, b, t, tm, tq, tk)
    conv, zlast = _conv_prompt(x, p["w_b"], p["conv_w"], t, tm)
    n_blk = t // BLOCK
    cmp = cmp.reshape(b, n_blk, 2, KV_HEADS, HEAD_DIM).astype(BF16)
    kc = cmp[:, :, 0].transpose(0, 2, 1, 3)
    vct = cmp[:, :, 1].transpose(0, 2, 3, 1)
    attn = _attn_prompt(qt, kc, vct, kas, vst, kaw, vwt, gates, tq, tk)
    h = _outproj_ln(attn, conv, x, p["w_out"], p["ln1_g"], p["ln1_b"], alpha, tm)
    y = _mlp_ln(h, p["w_mlp1"], p["w_mlp2"], p["ln2_g"], p["ln2_b"], alpha, tm, 1024)
    conv_state = zlast.reshape(b, t // tm, SUBLANES, -1)[:, -1, SUBLANES - (CONV_W - 1):]
    keep = min(WINDOW, t)
    rows_major = lambda kvt: kvt.reshape(b, 2, KV_HEADS, HEAD_DIM, t).transpose(0, 4, 1, 2, 3)
    return (y, rows_major(kvt_c), rows_major(kvt_s), rows_major(kvt_w)[:, t - keep:], conv_state)


def _sample_layer(x, p, pool_c, pool_s, page0, win_t, win0, cache_win, state_conv, page_table, db, dt, alpha):
    n = db * dt
    page_rows = pool_c.shape[2]
    n_pages = page_table.shape[1]
    past_len = n_pages * page_rows
    c = 2 * KV_WIDTH
    q, kv_c, kv_s, kv_w, gates = _proj(x, p["w_a"], n)

    zp = jnp.pad(state_conv, ((0, 0), (SUBLANES - (CONV_W - 1), 0), (0, 0))).reshape(n, -1)
    conv, z = _conv_sample(x, zp, p["w_b"], p["conv_w"])
    conv_state = jnp.concatenate([state_conv, z.reshape(db, dt, -1)], axis=1)[:, -(CONV_W - 1):]

    cmp_past = _cmp_pages(pool_c, page0, page_table, p["w_cmp_t"])
    tail = jnp.pad(kv_c.reshape(db, dt, c), ((0, 0), (0, SUBLANES * BLOCK - dt), (0, 0)))
    cmp_tail = _cmp_rows(tail, p["w_cmp"], SUBLANES)
    cmp_all = jnp.concatenate([cmp_past, cmp_tail], axis=1).astype(BF16)
    kc, vc = cmp_all[..., :KV_WIDTH], cmp_all[..., KV_WIDTH:]

    q5 = q.reshape(db, dt, KV_HEADS, Q_PER_KV, HEAD_DIM).transpose(0, 2, 4, 3, 1)
    eye = jnp.eye(KV_HEADS, dtype=q.dtype)
    qbd = (q5[:, :, :, :, None, :] * eye[None, :, None, None, :, None]).reshape(db, KV_WIDTH, N_HEADS * dt)
    gt = gates[:, :N_HEADS * N_BRANCH].reshape(db, dt, KV_HEADS, Q_PER_KV, N_BRANCH)
    gt = gt.transpose(0, 4, 3, 2, 1).reshape(db, N_BRANCH, N_HEADS * dt)
    pos = jnp.tile(past_len + jnp.arange(dt, dtype=jnp.int32), N_HEADS).reshape(1, N_HEADS * dt)
    new_rows = lambda kv: jnp.pad(kv.reshape(db, dt, c), ((0, 0), (0, 2 * SUBLANES - dt), (0, 0)))
    o = _attn_sample(pool_s, page0, page_table, qbd, kc, vc, new_rows(kv_s), win_t, win0, new_rows(kv_w), gt, pos)
    o = o.reshape(db, KV_HEADS, HEAD_DIM, Q_PER_KV, KV_HEADS, dt)
    o = jnp.stack([o[:, g, :, :, g, :] for g in range(KV_HEADS)], axis=1)
    attn = o.transpose(0, 4, 1, 3, 2).reshape(n, ATTN_WIDTH).astype(BF16)

    h = _outproj_ln(attn, conv, x, p["w_out"], p["ln1_g"], p["ln1_b"], alpha, n)
    y = _mlp_ln(h, p["w_mlp1"], p["w_mlp2"], p["ln2_g"], p["ln2_b"], alpha, n, 1024)
    win_keep = cache_win.shape[1]
    new_win = jnp.concatenate([cache_win, _split_kv(kv_w, db, dt)], axis=1)[:, -win_keep:]
    return y, _split_kv(kv_c, db, dt), _split_kv(kv_s, db, dt), new_win, conv_state


def kernel(x_prompt, x_sample, cache_cmp, cache_slc, cache_win, state_conv, page_table, w_in, w_cmp_k, w_cmp_v,
           conv_w, w_out, ln1_g, ln1_b, w_mlp1, w_mlp2, ln2_g, ln2_b):
    depth = w_in.shape[0]
    b, t, d = x_prompt.shape
    db, dt, _ = x_sample.shape
    alpha = (2.0 * depth) ** 0.25
    qkvg = ATTN_WIDTH + 6 * KV_WIDTH + N_HEADS * N_BRANCH
    tm = min(512, t)
    tq, tk = min(512, t), min(512, t)

    n_pool, page_rows = cache_cmp.shape[1:3]
    chan_major = lambda c: c.transpose(0, 1, 3, 4, 5, 2).reshape(c.shape[0] * c.shape[1], -1, c.shape[2])
    pool_c, pool_s, win_t = chan_major(cache_cmp), chan_major(cache_slc), chan_major(cache_win)

    yp = x_prompt.reshape(b * t, d)
    ys = x_sample.reshape(db * dt, d)
    outs = [[] for _ in range(8)]
    for l in range(depth):
        w_a = jnp.pad(w_in[l][:, :qkvg], ((0, 0), (0, GATE_PAD - N_HEADS * N_BRANCH))).astype(BF16)
        w_cmp = jnp.concatenate([jnp.tile(w_cmp_k[l], (1, KV_HEADS)), jnp.tile(w_cmp_v[l], (1, KV_HEADS))], axis=1)
        w_t = jnp.pad(w_in[l].T[:qkvg], ((0, (-qkvg) % 64), (0, 0))).astype(BF16)
        k_cols = w_in[l][:, ATTN_WIDTH + 2 * KV_WIDTH:ATTN_WIDTH + 6 * KV_WIDTH]
        k_cols = k_cols.reshape(d, 2, 2, KV_HEADS, HEAD_DIM)[:, :, 0]
        w_k = jnp.pad(k_cols, ((0, 0), (0, 0), (0, 0), (0, K_AUG - HEAD_DIM))).reshape(d, -1).astype(BF16)
        p = {
            "w_a": w_a,
            "w_t": w_t,
            "w_k": w_k,
            "w_b": w_in[l][:, qkvg:].astype(BF16),
            "w_cmp": w_cmp,
            "w_cmp_t": jnp.tile(w_cmp.T, (1, page_rows // BLOCK)),
            "conv_w": conv_w[l],
            "w_out": w_out[l].astype(BF16),
            "ln1_g": ln1_g[l].reshape(1, d), "ln1_b": ln1_b[l].reshape(1, d),
            "w_mlp1": w_mlp1[l].astype(BF16), "w_mlp2": w_mlp2[l].astype(BF16),
            "ln2_g": ln2_g[l].reshape(1, d), "ln2_b": ln2_b[l].reshape(1, d),
        }
        yp, c1, s1, w1, v1 = _prompt_layer(yp, p, b, t, alpha, tm, tq, tk)
        ys, c2, s2, w2, v2 = _sample_layer(ys, p, pool_c, pool_s, l * n_pool, win_t, l * db, cache_win[l],
                                           state_conv[l], page_table, db, dt, alpha)
        for lst, v in zip(outs, (c1, s1, w1, v1, c2, s2, w2, v2)):
            lst.append(v)
    stacked = [jnp.stack(o) for o in outs]
    return (yp.reshape(b, t, d), ys.reshape(db, dt, d), *stacked)
```

```python
import functools
import math

import jax
import jax.numpy as jnp
from jax import lax
from jax.experimental import pallas as pl
from jax.experimental.pallas import tpu as pltpu

HEAD_DIM = 64
KV_HEADS = 4
Q_PER_KV = 4
N_HEADS = KV_HEADS * Q_PER_KV
ATTN_WIDTH = N_HEADS * HEAD_DIM
KV_WIDTH = KV_HEADS * HEAD_DIM
N_BRANCH = 3
BLOCK = 64
BLOCK_SHIFT = BLOCK.bit_length() - 1
N_SEL = 16
WINDOW = 512
CONV_W = 3
LN_EPS = 1e-5
NEG = -1e30
FORCED_SCORE = 1e9
Q_SCALE = HEAD_DIM ** -0.5 * math.log2(math.e)
GATE_PAD = 128

SUBLANES = 8
VMEM_LIMIT = 56 * 1024 * 1024

BF16 = jnp.bfloat16
F32 = jnp.float32


def _cparams(sem):
    return pltpu.CompilerParams(dimension_semantics=sem, vmem_limit_bytes=VMEM_LIMIT)


def _layer_norm(v, g, b):
    mu = jnp.mean(v, axis=-1, keepdims=True)
    d = v - mu
    var = jnp.mean(d * d, axis=-1, keepdims=True)
    return d * lax.rsqrt(var + LN_EPS) * g + b


def _proj_kernel(x_ref, w_ref, q_ref, kc_ref, ks_ref, kw_ref, g_ref):
    x = x_ref[...].astype(BF16)

    def mm(c0, c1):
        return jnp.dot(x, w_ref[:, c0:c1], preferred_element_type=F32)

    c = ATTN_WIDTH
    q_ref[...] = (mm(0, c) * Q_SCALE).astype(BF16)
    kc_ref[...] = mm(c, c + 2 * KV_WIDTH)
    ks_ref[...] = mm(c + 2 * KV_WIDTH, c + 4 * KV_WIDTH)
    kw_ref[...] = mm(c + 4 * KV_WIDTH, c + 6 * KV_WIDTH)
    logits = mm(c + 6 * KV_WIDTH, c + 6 * KV_WIDTH + GATE_PAD)
    g_ref[...] = 1.0 / (1.0 + jnp.exp(-logits))


def _proj(x, w_a, tm):
    n, d = x.shape
    pw = w_a.shape[1]
    row = lambda i: (i, 0)
    return pl.pallas_call(
        _proj_kernel,
        grid=(n // tm,),
        in_specs=[pl.BlockSpec((tm, d), row), pl.BlockSpec((d, pw), lambda i: (0, 0))],
        out_specs=[
            pl.BlockSpec((tm, ATTN_WIDTH), row),
            pl.BlockSpec((tm, 2 * KV_WIDTH), row),
            pl.BlockSpec((tm, 2 * KV_WIDTH), row),
            pl.BlockSpec((tm, 2 * KV_WIDTH), row),
            pl.BlockSpec((tm, GATE_PAD), row),
        ],
        out_shape=[
            jax.ShapeDtypeStruct((n, ATTN_WIDTH), BF16),
            jax.ShapeDtypeStruct((n, 2 * KV_WIDTH), F32),
            jax.ShapeDtypeStruct((n, 2 * KV_WIDTH), F32),
            jax.ShapeDtypeStruct((n, 2 * KV_WIDTH), F32),
            jax.ShapeDtypeStruct((n, GATE_PAD), F32),
        ],
        compiler_params=_cparams(("parallel",)),
        name="proj",
    )(x, w_a)


NT_DIMS = (((1,), (1,)), ((), ()))
TN_DIMS = (((0,), (0,)), ((), ()))
K_AUG = 2 * HEAD_DIM
V_AUG = HEAD_DIM + 16
PAGES_PER_STEP = 16
COL_GROUPS = 8


def _block_sums_t(kvt, w_t, n_out=None, blk0=0):
    rows = kvt.shape[1]
    nb = rows // BLOCK if n_out is None else n_out
    pw = kvt * w_t
    hi = pw.astype(BF16)
    lo = (pw - hi.astype(F32)).astype(BF16)
    member = jnp.where(lax.broadcasted_iota(jnp.int32, (nb, rows), 0)
                       == blk0 + (lax.broadcasted_iota(jnp.int32, (nb, rows), 1) >> BLOCK_SHIFT),
                       1.0, 0.0).astype(BF16)
    return (lax.dot_general(member, hi, NT_DIMS, preferred_element_type=F32)
            + lax.dot_general(member, lo, NT_DIMS, preferred_element_type=F32))


def _proj_prompt_kernel(x_ref, wt_ref, wk_ref, wc_ref, kvc_ref, kvs_ref, kvw_ref, vts_ref, vtw_ref,
                        kas_ref, kaw_ref, qt_ref, g_ref, cmp_ref, *, tq, tk):
    tm = x_ref.shape[0]
    x = x_ref[...].astype(BF16)

    def nt(r0, r1):
        return lax.dot_general(wt_ref[r0:r1, :], x, NT_DIMS, preferred_element_type=F32)

    c = ATTN_WIDTH
    kv2 = 2 * KV_WIDTH
    qall = (nt(0, c) * Q_SCALE).astype(BF16)
    pad_rows = jnp.zeros((K_AUG - HEAD_DIM, Q_PER_KV * tq), BF16)
    for g in range(KV_HEADS):
        for j in range(tm // tq):
            heads = [qall[(g * Q_PER_KV + r) * HEAD_DIM:(g * Q_PER_KV + r + 1) * HEAD_DIM, j * tq:(j + 1) * tq]
                     for r in range(Q_PER_KV)]
            qt_ref[0, g, j, 0:HEAD_DIM, :] = jnp.concatenate(heads, axis=1)
            qt_ref[0, g, j, HEAD_DIM:K_AUG, :] = pad_rows

    kvt_c = nt(c, c + kv2)
    kvc_ref[0] = kvt_c
    cmp_ref[0] = _block_sums_t(kvt_c, wc_ref[...])

    lane = lax.broadcasted_iota(jnp.int32, (tm, K_AUG), 1)
    row_blk = (lax.broadcasted_iota(jnp.int32, (tm, K_AUG), 0) >> BLOCK_SHIFT) & (tk // BLOCK - 1)
    one_hot = jnp.where(lane - HEAD_DIM == row_blk, 1.0, 0.0)
    sum_rows = jnp.where(lax.broadcasted_iota(jnp.int32, (V_AUG - HEAD_DIM, tk), 0) == 0, 1.0, 0.0).astype(BF16)
    for i, (kv_ref, vt_ref, ka_ref) in enumerate(((kvs_ref, vts_ref, kas_ref), (kvw_ref, vtw_ref, kaw_ref))):
        kvt = nt(c + (i + 1) * kv2, c + (i + 2) * kv2)
        kv_ref[0] = kvt
        for g in range(KV_HEADS):
            v_rows = kvt[KV_WIDTH + g * HEAD_DIM:KV_WIDTH + (g + 1) * HEAD_DIM]
            for jk in range(tm // tk):
                vt_ref[0, g, jk, 0:HEAD_DIM, :] = v_rows[:, jk * tk:(jk + 1) * tk].astype(BF16)
                vt_ref[0, g, jk, HEAD_DIM:V_AUG, :] = sum_rows
            w0 = (i * KV_HEADS + g) * K_AUG
            k_rows = jnp.dot(x, wk_ref[:, w0:w0 + K_AUG], preferred_element_type=F32)
            ka_ref[0, g] = (k_rows + one_hot).astype(BF16)

    logits = nt(c + 3 * kv2, c + 3 * kv2 + g_ref.shape[2])
    gates = 1.0 / (1.0 + jnp.exp(-logits))
    for j in range(tm // tq):
        g_ref[0, j] = gates[:, j * tq:(j + 1) * tq]


def _proj_prompt(x, w_t, w_k, w_c, b, t, tm, tq, tk):
    n, d = x.shape
    tiles = t // tm
    g_rows = w_t.shape[0] - ATTN_WIDTH - 6 * KV_WIDTH
    const = lambda a: pl.BlockSpec(a.shape, lambda i, j: (0,) * a.ndim, pipeline_mode=pl.Buffered(1))
    kvt_spec = pl.BlockSpec((1, 2 * KV_WIDTH, tm), lambda i, j: (i, 0, j))
    vt_spec = pl.BlockSpec((1, KV_HEADS, tm // tk, V_AUG, tk), lambda i, j: (i, 0, j, 0, 0))
    ka_spec = pl.BlockSpec((1, KV_HEADS, tm, K_AUG), lambda i, j: (i, 0, j, 0))
    kvt_shape = jax.ShapeDtypeStruct((b, 2 * KV_WIDTH, t), F32)
    vt_shape = jax.ShapeDtypeStruct((b, KV_HEADS, t // tk, V_AUG, tk), BF16)
    ka_shape = jax.ShapeDtypeStruct((b, KV_HEADS, t, K_AUG), BF16)
    wide = Q_PER_KV * tq
    return pl.pallas_call(
        functools.partial(_proj_prompt_kernel, tq=tq, tk=tk),
        grid=(b, tiles),
        in_specs=[pl.BlockSpec((tm, d), lambda i, j: (i * tiles + j, 0)), const(w_t), const(w_k), const(w_c)],
        out_specs=[
            kvt_spec, kvt_spec, kvt_spec, vt_spec, vt_spec, ka_spec, ka_spec,
            pl.BlockSpec((1, KV_HEADS, tm // tq, K_AUG, wide), lambda i, j: (i, 0, j, 0, 0)),
            pl.BlockSpec((1, tm // tq, g_rows, tq), lambda i, j: (i, j, 0, 0)),
            pl.BlockSpec((1, tm // BLOCK, 2 * KV_WIDTH), lambda i, j: (i, j, 0)),
        ],
        out_shape=[
            kvt_shape, kvt_shape, kvt_shape, vt_shape, vt_shape, ka_shape, ka_shape,
            jax.ShapeDtypeStruct((b, KV_HEADS, t // tq, K_AUG, wide), BF16),
            jax.ShapeDtypeStruct((b, t // tq, g_rows, tq), F32),
            jax.ShapeDtypeStruct((b, t // BLOCK, 2 * KV_WIDTH), F32),
        ],
        compiler_params=_cparams(("parallel", "parallel")),
        name="proj_prompt",
    )(x, w_t, w_k, w_c)


def _conv_from_z(z, zp, bgate, cw_ref, c0, c1):
    rows = z.shape[0]
    sub = lax.broadcasted_iota(jnp.int32, z.shape, 0) & (SUBLANES - 1)
    z1 = jnp.where(sub >= 1, pltpu.roll(z, 1, 0), pltpu.roll(zp, rows - (SUBLANES - 1), 0))
    z2 = jnp.where(sub >= 2, pltpu.roll(z, 2, 0), pltpu.roll(zp, rows - (SUBLANES - 2), 0))
    w0 = cw_ref[0:1, c0:c1]
    w1 = cw_ref[1:2, c0:c1]
    w2 = cw_ref[2:3, c0:c1]
    return bgate * (w2 * z + w0 * z2 + w1 * z1)


def _conv_prompt_kernel(x_ref, xh_ref, w_ref, cw_ref, y_ref, zl_ref, *, tiles_per_seq, chunk):
    i = pl.program_id(0)
    cdim = y_ref.shape[1]
    x = x_ref[...].astype(BF16)
    xh = xh_ref[...].astype(BF16)
    keep = jnp.where(i % tiles_per_seq == 0, 0.0, 1.0)
    for c0 in range(0, cdim, chunk):
        c1 = c0 + chunk

        def mm(a, off):
            return jnp.dot(a, w_ref[:, off + c0:off + c1], preferred_element_type=F32)

        z = mm(x, 2 * cdim) * mm(x, 0)
        zh = mm(xh, 2 * cdim) * mm(xh, 0) * keep
        bgate = mm(x, cdim)
        zp = jnp.concatenate([zh, z[:-SUBLANES]], axis=0)
        y_ref[:, c0:c1] = _conv_from_z(z, zp, bgate, cw_ref, c0, c1).astype(BF16)
        zl_ref[:, c0:c1] = z[-SUBLANES:]


def _conv_prompt(x, w_b, cw, seq, tm, chunk=512):
    n, d = x.shape
    cdim = cw.shape[1]
    slabs = tm // SUBLANES
    kern = functools.partial(_conv_prompt_kernel, tiles_per_seq=seq // tm, chunk=chunk)
    return pl.pallas_call(
        kern,
        grid=(n // tm,),
        in_specs=[
            pl.BlockSpec((tm, d), lambda i: (i, 0)),
            pl.BlockSpec((SUBLANES, d), lambda i: (jnp.maximum(i * slabs - 1, 0), 0)),
            pl.BlockSpec((d, 3 * cdim), lambda i: (0, 0)),
            pl.BlockSpec((CONV_W, cdim), lambda i: (0, 0)),
        ],
        out_specs=[
            pl.BlockSpec((tm, cdim), lambda i: (i, 0)),
            pl.BlockSpec((SUBLANES, cdim), lambda i: (i, 0)),
        ],
        out_shape=[
            jax.ShapeDtypeStruct((n, cdim), BF16),
            jax.ShapeDtypeStruct((n // tm * SUBLANES, cdim), F32),
        ],
        compiler_params=_cparams(("parallel",)),
        name="conv_prompt",
    )(x, x, w_b, cw)


def _conv_sample_kernel(x_ref, zp_ref, w_ref, cw_ref, y_ref, z_ref, *, chunk):
    cdim = y_ref.shape[1]
    x = x_ref[...].astype(BF16)
    for c0 in range(0, cdim, chunk):
        c1 = c0 + chunk

        def mm(off):
            return jnp.dot(x, w_ref[:, off + c0:off + c1], preferred_element_type=F32)

        z = mm(2 * cdim) * mm(0)
        bgate = mm(cdim)
        y_ref[:, c0:c1] = _conv_from_z(z, zp_ref[:, c0:c1], bgate, cw_ref, c0, c1).astype(BF16)
        z_ref[:, c0:c1] = z


def _conv_sample(x, zp, w_b, cw, chunk=512):
    n, d = x.shape
    cdim = cw.shape[1]
    full = lambda shape: pl.BlockSpec(shape, lambda i: (0, 0))
    return pl.pallas_call(
        functools.partial(_conv_sample_kernel, chunk=chunk),
        grid=(1,),
        in_specs=[full((n, d)), full((n, cdim)), full((d, 3 * cdim)), full((CONV_W, cdim))],
        out_specs=[full((n, cdim)), full((n, cdim))],
        out_shape=[jax.ShapeDtypeStruct((n, cdim), BF16), jax.ShapeDtypeStruct((n, cdim), F32)],
        compiler_params=_cparams(("arbitrary",)),
        name="conv_sample",
    )(x, zp, w_b, cw)


def _cmp_rows_kernel(kv_ref, w_ref, o_ref):
    nb = o_ref.shape[1]
    x = kv_ref[0].reshape(nb, BLOCK, kv_ref.shape[2])
    o_ref[0] = jnp.sum(x * w_ref[...][None], axis=1)


def _cmp_rows(kv, w_cmp, nb):
    b, l, c = kv.shape
    return pl.pallas_call(
        _cmp_rows_kernel,
        grid=(b, l // (nb * BLOCK)),
        in_specs=[
            pl.BlockSpec((1, nb * BLOCK, c), lambda i, j: (i, j, 0)),
            pl.BlockSpec((BLOCK, c), lambda i, j: (0, 0)),
        ],
        out_specs=pl.BlockSpec((1, nb, c), lambda i, j: (i, j, 0)),
        out_shape=jax.ShapeDtypeStruct((b, l // BLOCK, c), F32),
        compiler_params=_cparams(("parallel", "parallel")),
        name="cmp_rows",
    )(kv, w_cmp)


def _cmp_pages_kernel(pt_ref, *refs, pages_per_step):
    page_refs = refs[:pages_per_step]
    w_ref = refs[pages_per_step]
    o_ref = refs[pages_per_step + 1]
    w = w_ref[...]
    nb = o_ref.shape[1]
    bpp = w.shape[1] // BLOCK
    acc = jnp.zeros(o_ref.shape[1:], F32)
    for j in range(pages_per_step):
        acc = acc + _block_sums_t(page_refs[j][0], w, nb, j * bpp)
    o_ref[0] = acc


def _page_specs(c, page_rows, pages_per_step, page0):
    def spec(j):
        return pl.BlockSpec((1, c, page_rows), lambda b, s, pt: (page0 + pt[b, s * pages_per_step + j], 0, 0))
    return [spec(j) for j in range(pages_per_step)]


def _cmp_pages(pool_t, page0, page_table, w_cmp_t, pages_per_step=PAGES_PER_STEP):
    _, c, page_rows = pool_t.shape
    b, n_pages = page_table.shape
    pages_per_step = min(pages_per_step, n_pages)
    nb = pages_per_step * page_rows // BLOCK
    grid_spec = pltpu.PrefetchScalarGridSpec(
        num_scalar_prefetch=1,
        grid=(b, n_pages // pages_per_step),
        in_specs=_page_specs(c, page_rows, pages_per_step, page0)
        + [pl.BlockSpec((c, page_rows), lambda i, s, pt: (0, 0))],
        out_specs=pl.BlockSpec((1, nb, c), lambda i, s, pt: (i, s, 0)),
    )
    return pl.pallas_call(
        functools.partial(_cmp_pages_kernel, pages_per_step=pages_per_step),
        grid_spec=grid_spec,
        out_shape=jax.ShapeDtypeStruct((b, n_pages * page_rows // BLOCK, c), F32),
        compiler_params=_cparams(("parallel", "parallel")),
        name="cmp_pages",
    )(page_table, *([pool_t] * pages_per_step), w_cmp_t)


def _masked_softmax0(s, mask):
    s = jnp.where(mask, s, NEG)
    m = jnp.max(s, axis=0, keepdims=True)
    e = jnp.where(mask, jnp.exp2(s - m), 0.0)
    return e / jnp.maximum(jnp.sum(e, axis=0, keepdims=True), 1e-30)


def _topk_mask0(score, blk, n_sel):
    big = jnp.int32(2 ** 30)
    sel = jnp.zeros(score.shape, F32)
    for _ in range(n_sel):
        m = jnp.max(score, axis=0, keepdims=True)
        idx = jnp.min(jnp.where(score == m, blk, big), axis=0, keepdims=True)
        pick = blk == idx
        sel = jnp.where(pick, 1.0, sel)
        score = jnp.where(pick, -jnp.inf, score)
    return sel


def _select_blocks(imp, blk, cur, n_valid_blk):
    forced = (blk == 0) | (blk == cur) | (blk == cur - 1)
    cand = blk <= cur
    score = jnp.where(forced, FORCED_SCORE, jnp.where(cand, imp, -FORCED_SCORE))
    score = jnp.where(blk < n_valid_blk, score, -jnp.inf)
    return _topk_mask0(score, blk, min(N_SEL, n_valid_blk))


def _sel_to_bias(sel):
    return (sel - 1.0) * (-NEG)


def _flash_update(s, bias, v_dot, m_ref, l_ref, acc_ref):
    if bias is not None:
        s = bias + s
    m_old = m_ref[...]
    m_new = jnp.maximum(m_old, jnp.max(s, axis=0, keepdims=True))
    alpha = jnp.exp2(m_old - m_new)
    p = jnp.exp2(s - m_new)
    if l_ref is not None:
        l_ref[...] = alpha * l_ref[...] + jnp.sum(p, axis=0, keepdims=True)
    acc_ref[...] = alpha * acc_ref[...] + v_dot(p.astype(BF16))
    m_ref[...] = m_new


def _flash_init(m_ref, l_ref, acc_ref):
    m_ref[...] = jnp.full(m_ref.shape, NEG, F32)
    if l_ref is not None:
        l_ref[...] = jnp.zeros(l_ref.shape, F32)
    acc_ref[...] = jnp.zeros(acc_ref.shape, F32)


def _attn_prompt_kernel(qt_ref, kc_ref, vct_ref, ks_ref, vst_ref, kw_ref, vwt_ref, g_ref,
                        o_ref, selb_ref, m_ref, acc_ref, out_ref, *, tq, tk):
    gi = pl.program_id(1)
    qi = pl.program_id(2)
    q0 = qi * tq
    n_blk = kc_ref.shape[2]
    wide = Q_PER_KV * tq
    qt = qt_ref[0, 0, 0]
    q_rows = qt[0:HEAD_DIM]
    pos = q0 + lax.broadcasted_iota(jnp.int32, (1, tq), 1)
    pos_w = q0 + (lax.broadcasted_iota(jnp.int32, (1, wide), 1) & (tq - 1))

    def gate(branch):
        rows = [g_ref[0, 0, pl.ds((gi * Q_PER_KV + r) * N_BRANCH + branch, 1), :] for r in range(Q_PER_KV)]
        return jnp.concatenate(rows, axis=1)

    blk_w = lax.broadcasted_iota(jnp.int32, (n_blk, wide), 0)
    p = _masked_softmax0(jnp.dot(kc_ref[0, 0], q_rows, preferred_element_type=F32),
                         blk_w * BLOCK + (BLOCK - 1) <= pos_w)
    o_c = jnp.dot(vct_ref[0, 0], p.astype(BF16), preferred_element_type=F32)
    out_ref[...] = gate(0) * o_c
    imp = p[:, 0:tq]
    for r in range(1, Q_PER_KV):
        imp = imp + p[:, r * tq:(r + 1) * tq]
    blk = lax.broadcasted_iota(jnp.int32, (n_blk, tq), 0)
    selb = _sel_to_bias(_select_blocks(imp, blk, pos >> BLOCK_SHIFT, n_blk))
    for r in range(Q_PER_KV):
        selb_ref[:, r * tq:(r + 1) * tq] = selb

    kpos_tile = lax.broadcasted_iota(jnp.int32, (tk, tq), 0)
    bpt = tk // BLOCK
    bias_pad = jnp.zeros((K_AUG - HEAD_DIM - bpt, wide), F32)

    def tile_update(k_ref, vt_ref, kt, q_op, bias):
        k_tile = k_ref[0, 0, pl.ds(pl.multiple_of(kt * tk, tk), tk), :]
        vt = vt_ref[0, 0, kt]
        v_dot = lambda pr: jnp.dot(vt, pr, preferred_element_type=F32)
        cw = wide // COL_GROUPS

        def group_bias(h):
            if bias is None:
                return None
            if cw >= tq:
                return jnp.concatenate([bias] * (cw // tq), axis=1)
            return bias[:, (h * cw) % tq:(h * cw) % tq + cw]

        groups = [slice(h * cw, (h + 1) * cw) for h in range(COL_GROUPS)]
        scores = [jnp.dot(k_tile, q_op[:, cols], preferred_element_type=F32) for cols in groups]
        for h, (s, cols) in enumerate(zip(scores, groups)):
            _flash_update(s, group_bias(h), v_dot, m_ref.at[:, cols], None, acc_ref.at[:, cols])

    def finish(branch):
        o = acc_ref[0:HEAD_DIM, :] / jnp.maximum(acc_ref[HEAD_DIM:HEAD_DIM + 1, :], 1e-30)
        out_ref[...] = out_ref[...] + gate(branch) * o

    def q_with_block_bias(kt):
        tile_bias = selb_ref[pl.ds(pl.multiple_of(kt * bpt, bpt), bpt), :]
        return jnp.concatenate([q_rows, jnp.concatenate([tile_bias, bias_pad], axis=0).astype(BF16)], axis=0)

    kt_diag = (q0 + tq - 1) // tk
    _flash_init(m_ref, None, acc_ref)

    def sel_body(kt, carry):
        tile_update(ks_ref, vst_ref, kt, q_with_block_bias(kt), None)
        return carry

    lax.fori_loop(0, kt_diag, sel_body, 0)
    causal = jnp.where(kt_diag * tk + kpos_tile <= pos, 0.0, NEG)
    tile_update(ks_ref, vst_ref, kt_diag, q_with_block_bias(kt_diag), causal)
    finish(1)

    _flash_init(m_ref, None, acc_ref)

    def win_body(kt, carry):
        delta = pos - (kt * tk + kpos_tile)
        tile_update(kw_ref, vwt_ref, kt, qt, jnp.where((delta >= 0) & (delta < WINDOW), 0.0, NEG))
        return carry

    lax.fori_loop(jnp.maximum(q0 - (WINDOW - 1), 0) // tk, kt_diag + 1, win_body, 0)
    finish(2)

    for pair in range(Q_PER_KV // 2):
        two = jnp.concatenate([out_ref[:, (2 * pair + h) * tq:(2 * pair + h + 1) * tq] for h in range(2)], axis=0)
        o_ref[:, pair * 2 * HEAD_DIM:(pair + 1) * 2 * HEAD_DIM] = two.T.astype(o_ref.dtype)


def _attn_prompt(qt, kc, vct, kas, vst, kaw, vwt, gates, tq, tk):
    b, g, nq, ka, wide = qt.shape
    t = nq * tq
    assert tq & (tq - 1) == 0 and wide == Q_PER_KV * tq and ka == K_AUG
    n_blk = kc.shape[2]
    nkt = t // tk
    d = HEAD_DIM
    bg = lambda *tail: (lambda i, j, q: (i, j) + tail)
    kern = functools.partial(_attn_prompt_kernel, tq=tq, tk=tk)
    return pl.pallas_call(
        kern,
        grid=(b, g, nq),
        in_specs=[
            pl.BlockSpec((1, 1, 1, ka, wide), lambda i, j, q: (i, j, q, 0, 0)),
            pl.BlockSpec((1, 1, n_blk, d), bg(0, 0)),
            pl.BlockSpec((1, 1, d, n_blk), bg(0, 0)),
            pl.BlockSpec((1, 1, t, ka), bg(0, 0)),
            pl.BlockSpec((1, 1, nkt, V_AUG, tk), bg(0, 0, 0)),
            pl.BlockSpec((1, 1, t, ka), bg(0, 0)),
            pl.BlockSpec((1, 1, nkt, V_AUG, tk), bg(0, 0, 0)),
            pl.BlockSpec((1, 1, gates.shape[2], tq), lambda i, j, q: (i, q, 0, 0)),
        ],
        out_specs=pl.BlockSpec((tq, Q_PER_KV * d), lambda i, j, q: (i * nq + q, j)),
        out_shape=jax.ShapeDtypeStruct((b * t, ATTN_WIDTH), BF16),
        scratch_shapes=[
            pltpu.VMEM((n_blk, wide), F32),
            pltpu.VMEM((1, wide), F32),
            pltpu.VMEM((V_AUG, wide), F32),
            pltpu.VMEM((d, wide), F32),
        ],
        compiler_params=_cparams(("parallel", "parallel", "parallel")),
        name="attn_prompt",
    )(qt, kc, vct, kas, vst, kaw, vwt, gates)


def _attn_sample_kernel(pt_ref, *refs, pages_per_step, past_len, n_blk_valid):
    page_refs = refs[:pages_per_step]
    (qbd_ref, kc_ref, vc_ref, new_s_ref, win_ref, new_w_ref, gt_ref, pos_ref,
     o_ref, selb_ref, m_ref, l_ref, acc_ref, out_ref) = refs[pages_per_step:]
    step = pl.program_id(1)
    n_steps = pl.num_programs(1)
    ncol = qbd_ref.shape[2]
    kvw = qbd_ref.shape[1]
    qbd = qbd_ref[0]
    pos = pos_ref[...]
    group_cols = ncol // Q_PER_KV

    def tile_rows(rows, bias):
        k, v = rows[:, :kvw].astype(BF16), rows[:, kvw:].astype(BF16)
        s = jnp.dot(k, qbd, preferred_element_type=F32)
        v_dot = lambda p: lax.dot_general(v, p, TN_DIMS, preferred_element_type=F32)
        _flash_update(s, bias, v_dot, m_ref, l_ref, acc_ref)

    def tile_chan(page, bias):
        kt, vt = page[:kvw].astype(BF16), page[kvw:].astype(BF16)
        s = lax.dot_general(kt, qbd, TN_DIMS, preferred_element_type=F32)
        v_dot = lambda p: jnp.dot(vt, p, preferred_element_type=F32)
        _flash_update(s, bias, v_dot, m_ref, l_ref, acc_ref)

    def window_bias(n_keys, p0):
        delta = pos - (p0 + lax.broadcasted_iota(jnp.int32, (n_keys, ncol), 0))
        return jnp.where((delta >= 0) & (delta < WINDOW), 0.0, NEG)

    @pl.when(step == 0)
    def _():
        n_blk = kc_ref.shape[1]
        blk = lax.broadcasted_iota(jnp.int32, (n_blk, ncol), 0)
        cmask = (blk * BLOCK + (BLOCK - 1) <= pos) & (blk < n_blk_valid)
        s = jnp.dot(kc_ref[0], qbd, preferred_element_type=F32)
        p = _masked_softmax0(s, cmask)
        o_c = lax.dot_general(vc_ref[0], p.astype(BF16), TN_DIMS, preferred_element_type=F32)
        out_ref[...] = gt_ref[0, 0:1, :] * o_c
        imp = p
        for r in range(1, Q_PER_KV):
            imp = imp + pltpu.roll(p, r * group_cols, 1)
        selb_ref[...] = _sel_to_bias(_select_blocks(imp, blk, pos >> BLOCK_SHIFT, n_blk_valid))

        _flash_init(m_ref, l_ref, acc_ref)
        n_win = win_ref.shape[2]
        tile_rows(new_w_ref[0], window_bias(new_w_ref.shape[1], past_len))
        tile_chan(win_ref[0], window_bias(n_win, past_len - n_win))
        o_w = acc_ref[...] / jnp.maximum(l_ref[...], 1e-30)
        out_ref[...] = out_ref[...] + gt_ref[0, 2:3, :] * o_w

        _flash_init(m_ref, l_ref, acc_ref)
        n_new = new_s_ref.shape[1]
        kpos = past_len + lax.broadcasted_iota(jnp.int32, (n_new, ncol), 0)
        chosen = jnp.broadcast_to(selb_ref[pl.ds(past_len // BLOCK, 1), :], (n_new, ncol))
        tile_rows(new_s_ref[0], jnp.where(kpos <= pos, chosen, NEG))

    page_rows = page_refs[0].shape[2]
    step_keys = pages_per_step * page_rows
    step_blocks = step_keys // BLOCK
    keys = jnp.concatenate([page_refs[j][0] for j in range(pages_per_step)], axis=1)
    chosen = jnp.concatenate(
        [jnp.broadcast_to(selb_ref[pl.ds(step * step_blocks + i, 1), :], (BLOCK, ncol)) for i in range(step_blocks)],
        axis=0)
    tile_chan(keys, chosen)

    @pl.when(step == n_steps - 1)
    def _():
        o_s = acc_ref[...] / jnp.maximum(l_ref[...], 1e-30)
        o_ref[0] = out_ref[...] + gt_ref[0, 1:2, :] * o_s


def _attn_sample(pool_t, page0, page_table, qbd, kc, vc, new_s, win_t, win0, new_w, gt, pos,
                 pages_per_step=PAGES_PER_STEP):
    _, c, page_rows = pool_t.shape
    b, n_pages = page_table.shape
    pages_per_step = min(pages_per_step, n_pages)
    _, kvw, ncol = qbd.shape
    n_blk = kc.shape[1]
    past_len = n_pages * page_rows
    per_b = lambda shape: pl.BlockSpec((1,) + shape, lambda i, s, pt: (i,) + (0,) * len(shape))
    grid_spec = pltpu.PrefetchScalarGridSpec(
        num_scalar_prefetch=1,
        grid=(b, n_pages // pages_per_step),
        in_specs=_page_specs(c, page_rows, pages_per_step, page0) + [
            per_b((kvw, ncol)),
            per_b((n_blk, kvw)),
            per_b((n_blk, kvw)),
            per_b(new_s.shape[1:]),
            pl.BlockSpec((1,) + win_t.shape[1:], lambda i, s, pt: (win0 + i, 0, 0)),
            per_b(new_w.shape[1:]),
            per_b((N_BRANCH, ncol)),
            pl.BlockSpec((1, ncol), lambda i, s, pt: (0, 0)),
        ],
        out_specs=per_b((kvw, ncol)),
        scratch_shapes=[
            pltpu.VMEM((n_blk, ncol), F32),
            pltpu.VMEM((1, ncol), F32),
            pltpu.VMEM((1, ncol), F32),
            pltpu.VMEM((kvw, ncol), F32),
            pltpu.VMEM((kvw, ncol), F32),
        ],
    )
    kern = functools.partial(_attn_sample_kernel, pages_per_step=pages_per_step, past_len=past_len,
                             n_blk_valid=past_len // BLOCK + 1)
    return pl.pallas_call(
        kern,
        grid_spec=grid_spec,
        out_shape=jax.ShapeDtypeStruct((b, kvw, ncol), F32),
        compiler_params=_cparams(("parallel", "arbitrary")),
        name="attn_sample",
    )(page_table, *([pool_t] * pages_per_step), qbd, kc, vc, new_s, win_t, new_w, gt, pos)


def _outproj_kernel(a_ref, c_ref, x_ref, w_ref, g_ref, b_ref, h_ref, *, alpha):
    aw = a_ref.shape[1]
    mix = jnp.dot(a_ref[...], w_ref[:aw, :], preferred_element_type=F32)
    mix = mix + jnp.dot(c_ref[...], w_ref[aw:, :], preferred_element_type=F32)
    h_ref[...] = _layer_norm(alpha * x_ref[...] + mix, g_ref[...], b_ref[...])


def _outproj_ln(attn, conv, x, w_out, layer, g, b, alpha, tm):
    n, d = x.shape
    row = lambda w: pl.BlockSpec((tm, w), lambda i: (i, 0))
    const = lambda shape: pl.BlockSpec(shape, lambda i: (0, 0))
    w_spec = pl.BlockSpec((None,) + w_out.shape[1:], lambda i: (layer, 0, 0))
    return pl.pallas_call(
        functools.partial(_outproj_kernel, alpha=alpha),
        grid=(n // tm,),
        in_specs=[row(attn.shape[1]), row(conv.shape[1]), row(d), w_spec, const((1, d)), const((1, d))],
        out_specs=row(d),
        out_shape=jax.ShapeDtypeStruct((n, d), F32),
        compiler_params=_cparams(("parallel",)),
        name="outproj_ln",
    )(attn, conv, x, w_out, g, b)


def _mlp_kernel(h_ref, w1_ref, w2_ref, g_ref, b_ref, y_ref, acc_ref, hb_ref, *, alpha):
    f = pl.program_id(1)

    @pl.when(f == 0)
    def _():
        hb_ref[...] = h_ref[...].astype(BF16)
        acc_ref[...] = jnp.zeros(acc_ref.shape, F32)

    a = jnp.dot(hb_ref[...], w1_ref[...], preferred_element_type=F32)
    a = jnp.square(jnp.maximum(a, 0.0)).astype(BF16)
    acc_ref[...] += jnp.dot(a, w2_ref[...], preferred_element_type=F32)

    @pl.when(f == pl.num_programs(1) - 1)
    def _():
        y_ref[...] = _layer_norm(alpha * h_ref[...] + acc_ref[...], g_ref[...], b_ref[...])


def _mlp_ln(h, w1, w2, layer, g, b, alpha, tm, tf):
    n, d = h.shape
    dff = w1.shape[2]
    return pl.pallas_call(
        functools.partial(_mlp_kernel, alpha=alpha),
        grid=(n // tm, dff // tf),
        in_specs=[
            pl.BlockSpec((tm, d), lambda i, f: (i, 0)),
            pl.BlockSpec((None, d, tf), lambda i, f: (layer, 0, f)),
            pl.BlockSpec((None, tf, d), lambda i, f: (layer, f, 0)),
            pl.BlockSpec((1, d), lambda i, f: (0, 0)),
            pl.BlockSpec((1, d), lambda i, f: (0, 0)),
        ],
        out_specs=pl.BlockSpec((tm, d), lambda i, f: (i, 0)),
        out_shape=jax.ShapeDtypeStruct((n, d), F32),
        scratch_shapes=[pltpu.VMEM((tm, d), F32), pltpu.VMEM((tm, d), BF16)],
        compiler_params=_cparams(("parallel", "arbitrary")),
        name="mlp_ln",
    )(h, w1, w2, g, b)


def _split_kv(kv, b, t):
    return kv.reshape(b, t, 2, KV_HEADS, HEAD_DIM)


def _prompt_layer(x, p, b, t, alpha, tm, tq, tk):
    w_c = jnp.tile(p["w_cmp"].T, (1, tm // BLOCK))
    kvt_c, kvt_s, kvt_w, vst, vwt, kas, kaw, qt, gates, cmp = _proj_prompt(
        x, p["w_t"], p["w_k"], w_c, b, t, tm, tq, tk)
    conv, zlast = _conv_prompt(x, p["w_b"], p["conv_w"], t, tm)
    n_blk = t // BLOCK
    cmp = cmp.reshape(b, n_blk, 2, KV_HEADS, HEAD_DIM).astype(BF16)
    kc = cmp[:, :, 0].transpose(0, 2, 1, 3)
    vct = cmp[:, :, 1].transpose(0, 2, 3, 1)
    attn = _attn_prompt(qt, kc, vct, kas, vst, kaw, vwt, gates, tq, tk)
    h = _outproj_ln(attn, conv, x, p["w_out"], p["layer"], p["ln1_g"], p["ln1_b"], alpha, tm)
    y = _mlp_ln(h, p["w_mlp1"], p["w_mlp2"], p["layer"], p["ln2_g"], p["ln2_b"], alpha, tm, 1024)
    conv_state = zlast.reshape(b, t // tm, SUBLANES, -1)[:, -1, SUBLANES - (CONV_W - 1):]
    keep = min(WINDOW, t)
    rows_major = lambda kvt: kvt.reshape(b, 2, KV_HEADS, HEAD_DIM, t).transpose(0, 4, 1, 2, 3)
    return (y, rows_major(kvt_c), rows_major(kvt_s), rows_major(kvt_w)[:, t - keep:], conv_state)


def _sample_layer(x, p, pool_c, pool_s, page0, win_t, win0, cache_win, state_conv, page_table, db, dt, alpha):
    n = db * dt
    page_rows = pool_c.shape[2]
    n_pages = page_table.shape[1]
    past_len = n_pages * page_rows
    c = 2 * KV_WIDTH
    q, kv_c, kv_s, kv_w, gates = _proj(x, p["w_a"], n)

    zp = jnp.pad(state_conv, ((0, 0), (SUBLANES - (CONV_W - 1), 0), (0, 0))).reshape(n, -1)
    conv, z = _conv_sample(x, zp, p["w_b"], p["conv_w"])
    conv_state = jnp.concatenate([state_conv, z.reshape(db, dt, -1)], axis=1)[:, -(CONV_W - 1):]

    cmp_past = _cmp_pages(pool_c, page0, page_table, p["w_cmp_t"])
    tail = jnp.pad(kv_c.reshape(db, dt, c), ((0, 0), (0, SUBLANES * BLOCK - dt), (0, 0)))
    cmp_tail = _cmp_rows(tail, p["w_cmp"], SUBLANES)
    cmp_all = jnp.concatenate([cmp_past, cmp_tail], axis=1).astype(BF16)
    kc, vc = cmp_all[..., :KV_WIDTH], cmp_all[..., KV_WIDTH:]

    q5 = q.reshape(db, dt, KV_HEADS, Q_PER_KV, HEAD_DIM).transpose(0, 2, 4, 3, 1)
    eye = jnp.eye(KV_HEADS, dtype=q.dtype)
    qbd = (q5[:, :, :, :, None, :] * eye[None, :, None, None, :, None]).reshape(db, KV_WIDTH, N_HEADS * dt)
    gt = gates[:, :N_HEADS * N_BRANCH].reshape(db, dt, KV_HEADS, Q_PER_KV, N_BRANCH)
    gt = gt.transpose(0, 4, 3, 2, 1).reshape(db, N_BRANCH, N_HEADS * dt)
    pos = jnp.tile(past_len + jnp.arange(dt, dtype=jnp.int32), N_HEADS).reshape(1, N_HEADS * dt)
    new_rows = lambda kv: jnp.pad(kv.reshape(db, dt, c), ((0, 0), (0, 2 * SUBLANES - dt), (0, 0)))
    o = _attn_sample(pool_s, page0, page_table, qbd, kc, vc, new_rows(kv_s), win_t, win0, new_rows(kv_w), gt, pos)
    o = o.reshape(db, KV_HEADS, HEAD_DIM, Q_PER_KV, KV_HEADS, dt)
    o = jnp.stack([o[:, g, :, :, g, :] for g in range(KV_HEADS)], axis=1)
    attn = o.transpose(0, 4, 1, 3, 2).reshape(n, ATTN_WIDTH).astype(BF16)

    h = _outproj_ln(attn, conv, x, p["w_out"], p["layer"], p["ln1_g"], p["ln1_b"], alpha, n)
    y = _mlp_ln(h, p["w_mlp1"], p["w_mlp2"], p["layer"], p["ln2_g"], p["ln2_b"], alpha, n, 1024)
    win_keep = cache_win.shape[1]
    new_win = jnp.concatenate([cache_win, _split_kv(kv_w, db, dt)], axis=1)[:, -win_keep:]
    return y, _split_kv(kv_c, db, dt), _split_kv(kv_s, db, dt), new_win, conv_state


def kernel(x_prompt, x_sample, cache_cmp, cache_slc, cache_win, state_conv, page_table, w_in, w_cmp_k, w_cmp_v,
           conv_w, w_out, ln1_g, ln1_b, w_mlp1, w_mlp2, ln2_g, ln2_b):
    depth = w_in.shape[0]
    b, t, d = x_prompt.shape
    db, dt, _ = x_sample.shape
    alpha = (2.0 * depth) ** 0.25
    qkvg = ATTN_WIDTH + 6 * KV_WIDTH + N_HEADS * N_BRANCH
    tm = min(512, t)
    tq, tk = min(512, t), min(512, t)

    n_pool, page_rows = cache_cmp.shape[1:3]
    chan_major = lambda c: c.transpose(0, 1, 3, 4, 5, 2).reshape(c.shape[0] * c.shape[1], -1, c.shape[2])
    pool_c, pool_s, win_t = chan_major(cache_cmp), chan_major(cache_slc), chan_major(cache_win)

    w_out_b, w_mlp1_b, w_mlp2_b = w_out.astype(BF16), w_mlp1.astype(BF16), w_mlp2.astype(BF16)
    yp = x_prompt.reshape(b * t, d)
    ys = x_sample.reshape(db * dt, d)
    outs = [[] for _ in range(8)]
    for l in range(depth):
        w_a = jnp.pad(w_in[l][:, :qkvg], ((0, 0), (0, GATE_PAD - N_HEADS * N_BRANCH))).astype(BF16)
        w_cmp = jnp.concatenate([jnp.tile(w_cmp_k[l], (1, KV_HEADS)), jnp.tile(w_cmp_v[l], (1, KV_HEADS))], axis=1)
        w_t = jnp.pad(w_in[l].T[:qkvg], ((0, (-qkvg) % 64), (0, 0))).astype(BF16)
        k_cols = w_in[l][:, ATTN_WIDTH + 2 * KV_WIDTH:ATTN_WIDTH + 6 * KV_WIDTH]
        k_cols = k_cols.reshape(d, 2, 2, KV_HEADS, HEAD_DIM)[:, :, 0]
        w_k = jnp.pad(k_cols, ((0, 0), (0, 0), (0, 0), (0, K_AUG - HEAD_DIM))).reshape(d, -1).astype(BF16)
        p = {
            "w_a": w_a,
            "w_t": w_t,
            "w_k": w_k,
            "w_b": w_in[l][:, qkvg:].astype(BF16),
            "w_cmp": w_cmp,
            "w_cmp_t": jnp.tile(w_cmp.T, (1, page_rows // BLOCK)),
            "conv_w": conv_w[l],
            "layer": l,
            "w_out": w_out_b,
            "ln1_g": ln1_g[l].reshape(1, d), "ln1_b": ln1_b[l].reshape(1, d),
            "w_mlp1": w_mlp1_b, "w_mlp2": w_mlp2_b,
            "ln2_g": ln2_g[l].reshape(1, d), "ln2_b": ln2_b[l].reshape(1, d),
        }
        yp, c1, s1, w1, v1 = _prompt_layer(yp, p, b, t, alpha, tm, tq, tk)
        ys, c2, s2, w2, v2 = _sample_layer(ys, p, pool_c, pool_s, l * n_pool, win_t, l * db, cache_win[l],
                                           state_conv[l], page_table, db, dt, alpha)
        for lst, v in zip(outs, (c1, s1, w1, v1, c2, s2, w2, v2)):
            lst.append(v)
    stacked = [jnp.stack(o) for o in outs]
    return (yp.reshape(b, t, d), ys.reshape(db, dt, d), *stacked)
```

```python
import functools
import math

import jax
import jax.numpy as jnp
from jax import lax
from jax.experimental import pallas as pl
from jax.experimental.pallas import tpu as pltpu

HEAD_DIM = 64
KV_HEADS = 4
Q_PER_KV = 4
N_HEADS = KV_HEADS * Q_PER_KV
ATTN_WIDTH = N_HEADS * HEAD_DIM
KV_WIDTH = KV_HEADS * HEAD_DIM
N_BRANCH = 3
BLOCK = 64
BLOCK_SHIFT = BLOCK.bit_length() - 1
N_SEL = 16
WINDOW = 512
CONV_W = 3
LN_EPS = 1e-5
NEG = -1e30
FORCED_SCORE = 1e9
Q_SCALE = HEAD_DIM ** -0.5 * math.log2(math.e)
GATE_PAD = 128

SUBLANES = 8
VMEM_LIMIT = 56 * 1024 * 1024

BF16 = jnp.bfloat16
F32 = jnp.float32


def _cparams(sem):
    return pltpu.CompilerParams(dimension_semantics=sem, vmem_limit_bytes=VMEM_LIMIT)


def _layer_norm(v, g, b):
    mu = jnp.mean(v, axis=-1, keepdims=True)
    d = v - mu
    var = jnp.mean(d * d, axis=-1, keepdims=True)
    return d * lax.rsqrt(var + LN_EPS) * g + b


def _proj_kernel(x_ref, w_ref, q_ref, kc_ref, ks_ref, kw_ref, g_ref):
    x = x_ref[...].astype(BF16)

    def mm(c0, c1):
        return jnp.dot(x, w_ref[:, c0:c1], preferred_element_type=F32)

    c = ATTN_WIDTH
    q_ref[...] = (mm(0, c) * Q_SCALE).astype(BF16)
    kc_ref[...] = mm(c, c + 2 * KV_WIDTH)
    ks_ref[...] = mm(c + 2 * KV_WIDTH, c + 4 * KV_WIDTH)
    kw_ref[...] = mm(c + 4 * KV_WIDTH, c + 6 * KV_WIDTH)
    logits = mm(c + 6 * KV_WIDTH, c + 6 * KV_WIDTH + GATE_PAD)
    g_ref[...] = 1.0 / (1.0 + jnp.exp(-logits))


def _proj(x, w_a, tm):
    n, d = x.shape
    pw = w_a.shape[1]
    row = lambda i: (i, 0)
    return pl.pallas_call(
        _proj_kernel,
        grid=(n // tm,),
        in_specs=[pl.BlockSpec((tm, d), row), pl.BlockSpec((d, pw), lambda i: (0, 0))],
        out_specs=[
            pl.BlockSpec((tm, ATTN_WIDTH), row),
            pl.BlockSpec((tm, 2 * KV_WIDTH), row),
            pl.BlockSpec((tm, 2 * KV_WIDTH), row),
            pl.BlockSpec((tm, 2 * KV_WIDTH), row),
            pl.BlockSpec((tm, GATE_PAD), row),
        ],
        out_shape=[
            jax.ShapeDtypeStruct((n, ATTN_WIDTH), BF16),
            jax.ShapeDtypeStruct((n, 2 * KV_WIDTH), F32),
            jax.ShapeDtypeStruct((n, 2 * KV_WIDTH), F32),
            jax.ShapeDtypeStruct((n, 2 * KV_WIDTH), F32),
            jax.ShapeDtypeStruct((n, GATE_PAD), F32),
        ],
        compiler_params=_cparams(("parallel",)),
        name="proj",
    )(x, w_a)


NT_DIMS = (((1,), (1,)), ((), ()))
TN_DIMS = (((0,), (0,)), ((), ()))
K_AUG = 2 * HEAD_DIM
V_AUG = HEAD_DIM + 16
MLP_ROWS, MLP_COLS = 512, 1024
PAGES_PER_STEP = 16
COL_GROUPS = 8


def _block_sums_t(kvt, w_t, n_out=None, blk0=0):
    rows = kvt.shape[1]
    nb = rows // BLOCK if n_out is None else n_out
    pw = kvt * w_t
    hi = pw.astype(BF16)
    lo = (pw - hi.astype(F32)).astype(BF16)
    member = jnp.where(lax.broadcasted_iota(jnp.int32, (nb, rows), 0)
                       == blk0 + (lax.broadcasted_iota(jnp.int32, (nb, rows), 1) >> BLOCK_SHIFT),
                       1.0, 0.0).astype(BF16)
    return (lax.dot_general(member, hi, NT_DIMS, preferred_element_type=F32)
            + lax.dot_general(member, lo, NT_DIMS, preferred_element_type=F32))


def _proj_prompt_kernel(x_ref, wt_ref, wk_ref, wc_ref, kvc_ref, kvs_ref, kvw_ref, vts_ref, vtw_ref,
                        kas_ref, kaw_ref, qt_ref, g_ref, cmp_ref, *, tq, tk):
    tm = x_ref.shape[0]
    x = x_ref[...].astype(BF16)

    def nt(r0, r1):
        return lax.dot_general(wt_ref[r0:r1, :], x, NT_DIMS, preferred_element_type=F32)

    c = ATTN_WIDTH
    kv2 = 2 * KV_WIDTH
    qall = (nt(0, c) * Q_SCALE).astype(BF16)
    pad_rows = jnp.zeros((K_AUG - HEAD_DIM, Q_PER_KV * tq), BF16)
    for g in range(KV_HEADS):
        for j in range(tm // tq):
            heads = [qall[(g * Q_PER_KV + r) * HEAD_DIM:(g * Q_PER_KV + r + 1) * HEAD_DIM, j * tq:(j + 1) * tq]
                     for r in range(Q_PER_KV)]
            qt_ref[0, g, j, 0:HEAD_DIM, :] = jnp.concatenate(heads, axis=1)
            qt_ref[0, g, j, HEAD_DIM:K_AUG, :] = pad_rows

    kvt_c = nt(c, c + kv2)
    kvc_ref[0] = kvt_c
    cmp_ref[0] = _block_sums_t(kvt_c, wc_ref[...])

    lane = lax.broadcasted_iota(jnp.int32, (tm, K_AUG), 1)
    row_blk = (lax.broadcasted_iota(jnp.int32, (tm, K_AUG), 0) >> BLOCK_SHIFT) & (tk // BLOCK - 1)
    one_hot = jnp.where(lane - HEAD_DIM == row_blk, 1.0, 0.0)
    sum_rows = jnp.where(lax.broadcasted_iota(jnp.int32, (V_AUG - HEAD_DIM, tk), 0) == 0, 1.0, 0.0).astype(BF16)
    for i, (kv_ref, vt_ref, ka_ref) in enumerate(((kvs_ref, vts_ref, kas_ref), (kvw_ref, vtw_ref, kaw_ref))):
        kvt = nt(c + (i + 1) * kv2, c + (i + 2) * kv2)
        kv_ref[0] = kvt
        for g in range(KV_HEADS):
            v_rows = kvt[KV_WIDTH + g * HEAD_DIM:KV_WIDTH + (g + 1) * HEAD_DIM]
            for jk in range(tm // tk):
                vt_ref[0, g, jk, 0:HEAD_DIM, :] = v_rows[:, jk * tk:(jk + 1) * tk].astype(BF16)
                vt_ref[0, g, jk, HEAD_DIM:V_AUG, :] = sum_rows
        for g in range(0, KV_HEADS, 2):
            w0 = (i * KV_HEADS + g) * K_AUG
            k_rows = jnp.dot(x, wk_ref[:, w0:w0 + 2 * K_AUG], preferred_element_type=F32)
            ka_ref[0, g] = (k_rows[:, :K_AUG] + one_hot).astype(BF16)
            ka_ref[0, g + 1] = (k_rows[:, K_AUG:] + one_hot).astype(BF16)

    logits = nt(c + 3 * kv2, c + 3 * kv2 + g_ref.shape[2])
    gates = 1.0 / (1.0 + jnp.exp(-logits))
    for j in range(tm // tq):
        g_ref[0, j] = gates[:, j * tq:(j + 1) * tq]


def _proj_prompt(x, w_t, w_k, w_c, b, t, tm, tq, tk):
    n, d = x.shape
    tiles = t // tm
    g_rows = w_t.shape[0] - ATTN_WIDTH - 6 * KV_WIDTH
    const = lambda a: pl.BlockSpec(a.shape, lambda i, j: (0,) * a.ndim, pipeline_mode=pl.Buffered(1))
    kvt_spec = pl.BlockSpec((1, 2 * KV_WIDTH, tm), lambda i, j: (i, 0, j))
    vt_spec = pl.BlockSpec((1, KV_HEADS, tm // tk, V_AUG, tk), lambda i, j: (i, 0, j, 0, 0))
    ka_spec = pl.BlockSpec((1, KV_HEADS, tm, K_AUG), lambda i, j: (i, 0, j, 0))
    kvt_shape = jax.ShapeDtypeStruct((b, 2 * KV_WIDTH, t), F32)
    vt_shape = jax.ShapeDtypeStruct((b, KV_HEADS, t // tk, V_AUG, tk), BF16)
    ka_shape = jax.ShapeDtypeStruct((b, KV_HEADS, t, K_AUG), BF16)
    wide = Q_PER_KV * tq
    return pl.pallas_call(
        functools.partial(_proj_prompt_kernel, tq=tq, tk=tk),
        grid=(b, tiles),
        in_specs=[pl.BlockSpec((tm, d), lambda i, j: (i * tiles + j, 0)), const(w_t), const(w_k), const(w_c)],
        out_specs=[
            kvt_spec, kvt_spec, kvt_spec, vt_spec, vt_spec, ka_spec, ka_spec,
            pl.BlockSpec((1, KV_HEADS, tm // tq, K_AUG, wide), lambda i, j: (i, 0, j, 0, 0)),
            pl.BlockSpec((1, tm // tq, g_rows, tq), lambda i, j: (i, j, 0, 0)),
            pl.BlockSpec((1, tm // BLOCK, 2 * KV_WIDTH), lambda i, j: (i, j, 0)),
        ],
        out_shape=[
            kvt_shape, kvt_shape, kvt_shape, vt_shape, vt_shape, ka_shape, ka_shape,
            jax.ShapeDtypeStruct((b, KV_HEADS, t // tq, K_AUG, wide), BF16),
            jax.ShapeDtypeStruct((b, t // tq, g_rows, tq), F32),
            jax.ShapeDtypeStruct((b, t // BLOCK, 2 * KV_WIDTH), F32),
        ],
        compiler_params=_cparams(("parallel", "parallel")),
        name="proj_prompt",
    )(x, w_t, w_k, w_c)


def _conv_from_z(z, zp, bgate, cw_ref, c0, c1):
    rows = z.shape[0]
    sub = lax.broadcasted_iota(jnp.int32, z.shape, 0) & (SUBLANES - 1)
    z1 = jnp.where(sub >= 1, pltpu.roll(z, 1, 0), pltpu.roll(zp, rows - (SUBLANES - 1), 0))
    z2 = jnp.where(sub >= 2, pltpu.roll(z, 2, 0), pltpu.roll(zp, rows - (SUBLANES - 2), 0))
    w0 = cw_ref[0:1, c0:c1]
    w1 = cw_ref[1:2, c0:c1]
    w2 = cw_ref[2:3, c0:c1]
    return bgate * (w2 * z + w0 * z2 + w1 * z1)


def _conv_prompt_kernel(x_ref, w_ref, cw_ref, y_ref, zl_ref, carry_ref, *, chunk):
    cdim = y_ref.shape[1]
    x = x_ref[...].astype(BF16)

    @pl.when(pl.program_id(1) == 0)
    def _():
        carry_ref[...] = jnp.zeros(carry_ref.shape, F32)

    for c0 in range(0, cdim, chunk):
        c1 = c0 + chunk

        def mm(off):
            return jnp.dot(x, w_ref[:, off + c0:off + c1], preferred_element_type=F32)

        z = mm(2 * cdim) * mm(0)
        bgate = mm(cdim)
        zp = jnp.concatenate([carry_ref[:, c0:c1], z[:-SUBLANES]], axis=0)
        y_ref[:, c0:c1] = _conv_from_z(z, zp, bgate, cw_ref, c0, c1).astype(BF16)
        carry_ref[:, c0:c1] = z[-SUBLANES:]
        zl_ref[:, c0:c1] = z[-SUBLANES:]


def _conv_prompt(x, w_b, cw, seq, tm, chunk=512):
    n, d = x.shape
    cdim = cw.shape[1]
    tiles = seq // tm
    return pl.pallas_call(
        functools.partial(_conv_prompt_kernel, chunk=chunk),
        grid=(n // seq, tiles),
        in_specs=[
            pl.BlockSpec((tm, d), lambda i, j: (i * tiles + j, 0)),
            pl.BlockSpec((d, 3 * cdim), lambda i, j: (0, 0), pipeline_mode=pl.Buffered(1)),
            pl.BlockSpec((CONV_W, cdim), lambda i, j: (0, 0)),
        ],
        out_specs=[
            pl.BlockSpec((tm, cdim), lambda i, j: (i * tiles + j, 0)),
            pl.BlockSpec((SUBLANES, cdim), lambda i, j: (i * tiles + j, 0)),
        ],
        out_shape=[
            jax.ShapeDtypeStruct((n, cdim), BF16),
            jax.ShapeDtypeStruct((n // tm * SUBLANES, cdim), F32),
        ],
        scratch_shapes=[pltpu.VMEM((SUBLANES, cdim), F32)],
        compiler_params=_cparams(("parallel", "arbitrary")),
        name="conv_prompt",
    )(x, w_b, cw)


def _conv_sample_kernel(x_ref, zp_ref, w_ref, cw_ref, y_ref, z_ref, *, chunk):
    cdim = y_ref.shape[1]
    x = x_ref[...].astype(BF16)
    for c0 in range(0, cdim, chunk):
        c1 = c0 + chunk

        def mm(off):
            return jnp.dot(x, w_ref[:, off + c0:off + c1], preferred_element_type=F32)

        z = mm(2 * cdim) * mm(0)
        bgate = mm(cdim)
        y_ref[:, c0:c1] = _conv_from_z(z, zp_ref[:, c0:c1], bgate, cw_ref, c0, c1).astype(BF16)
        z_ref[:, c0:c1] = z


def _conv_sample(x, zp, w_b, cw, chunk=512):
    n, d = x.shape
    cdim = cw.shape[1]
    full = lambda shape: pl.BlockSpec(shape, lambda i: (0, 0))
    return pl.pallas_call(
        functools.partial(_conv_sample_kernel, chunk=chunk),
        grid=(1,),
        in_specs=[full((n, d)), full((n, cdim)), full((d, 3 * cdim)), full((CONV_W, cdim))],
        out_specs=[full((n, cdim)), full((n, cdim))],
        out_shape=[jax.ShapeDtypeStruct((n, cdim), BF16), jax.ShapeDtypeStruct((n, cdim), F32)],
        compiler_params=_cparams(("arbitrary",)),
        name="conv_sample",
    )(x, zp, w_b, cw)


def _cmp_rows_kernel(kv_ref, w_ref, o_ref):
    nb = o_ref.shape[1]
    x = kv_ref[0].reshape(nb, BLOCK, kv_ref.shape[2])
    o_ref[0] = jnp.sum(x * w_ref[...][None], axis=1)


def _cmp_rows(kv, w_cmp, nb):
    b, l, c = kv.shape
    return pl.pallas_call(
        _cmp_rows_kernel,
        grid=(b, l // (nb * BLOCK)),
        in_specs=[
            pl.BlockSpec((1, nb * BLOCK, c), lambda i, j: (i, j, 0)),
            pl.BlockSpec((BLOCK, c), lambda i, j: (0, 0)),
        ],
        out_specs=pl.BlockSpec((1, nb, c), lambda i, j: (i, j, 0)),
        out_shape=jax.ShapeDtypeStruct((b, l // BLOCK, c), F32),
        compiler_params=_cparams(("parallel", "parallel")),
        name="cmp_rows",
    )(kv, w_cmp)


def _cmp_pages_kernel(pt_ref, *refs, pages_per_step):
    page_refs = refs[:pages_per_step]
    w_ref = refs[pages_per_step]
    o_ref = refs[pages_per_step + 1]
    w = w_ref[...]
    nb = o_ref.shape[1]
    bpp = w.shape[1] // BLOCK
    acc = jnp.zeros(o_ref.shape[1:], F32)
    for j in range(pages_per_step):
        acc = acc + _block_sums_t(page_refs[j][0], w, nb, j * bpp)
    o_ref[0] = acc


def _page_specs(c, page_rows, pages_per_step, page0):
    def spec(j):
        return pl.BlockSpec((1, c, page_rows), lambda b, s, pt: (page0 + pt[b, s * pages_per_step + j], 0, 0))
    return [spec(j) for j in range(pages_per_step)]


def _cmp_pages(pool_t, page0, page_table, w_cmp_t, pages_per_step=PAGES_PER_STEP):
    _, c, page_rows = pool_t.shape
    b, n_pages = page_table.shape
    pages_per_step = min(pages_per_step, n_pages)
    nb = pages_per_step * page_rows // BLOCK
    grid_spec = pltpu.PrefetchScalarGridSpec(
        num_scalar_prefetch=1,
        grid=(b, n_pages // pages_per_step),
        in_specs=_page_specs(c, page_rows, pages_per_step, page0)
        + [pl.BlockSpec((c, page_rows), lambda i, s, pt: (0, 0))],
        out_specs=pl.BlockSpec((1, nb, c), lambda i, s, pt: (i, s, 0)),
    )
    return pl.pallas_call(
        functools.partial(_cmp_pages_kernel, pages_per_step=pages_per_step),
        grid_spec=grid_spec,
        out_shape=jax.ShapeDtypeStruct((b, n_pages * page_rows // BLOCK, c), F32),
        compiler_params=_cparams(("parallel", "parallel")),
        name="cmp_pages",
    )(page_table, *([pool_t] * pages_per_step), w_cmp_t)


def _masked_softmax0(s, mask):
    s = jnp.where(mask, s, NEG)
    m = jnp.max(s, axis=0, keepdims=True)
    e = jnp.where(mask, jnp.exp2(s - m), 0.0)
    return e / jnp.maximum(jnp.sum(e, axis=0, keepdims=True), 1e-30)


def _topk_mask0(score, blk, n_sel):
    big = jnp.int32(2 ** 30)
    sel = jnp.zeros(score.shape, F32)
    for _ in range(n_sel):
        m = jnp.max(score, axis=0, keepdims=True)
        idx = jnp.min(jnp.where(score == m, blk, big), axis=0, keepdims=True)
        pick = blk == idx
        sel = jnp.where(pick, 1.0, sel)
        score = jnp.where(pick, -jnp.inf, score)
    return sel


def _select_blocks(imp, blk, cur, n_valid_blk):
    forced = (blk == 0) | (blk == cur) | (blk == cur - 1)
    cand = blk <= cur
    score = jnp.where(forced, FORCED_SCORE, jnp.where(cand, imp, -FORCED_SCORE))
    score = jnp.where(blk < n_valid_blk, score, -jnp.inf)
    return _topk_mask0(score, blk, min(N_SEL, n_valid_blk))


def _sel_to_bias(sel):
    return (sel - 1.0) * (-NEG)


def _flash_update(s, bias, v_dot, m_ref, l_ref, acc_ref):
    if bias is not None:
        s = bias + s
    m_old = m_ref[...]
    m_new = jnp.maximum(m_old, jnp.max(s, axis=0, keepdims=True))
    alpha = jnp.exp2(m_old - m_new)
    p = jnp.exp2(s - m_new)
    if l_ref is not None:
        l_ref[...] = alpha * l_ref[...] + jnp.sum(p, axis=0, keepdims=True)
    acc_ref[...] = alpha * acc_ref[...] + v_dot(p.astype(BF16))
    m_ref[...] = m_new


def _flash_init(m_ref, l_ref, acc_ref):
    m_ref[...] = jnp.full(m_ref.shape, NEG, F32)
    if l_ref is not None:
        l_ref[...] = jnp.zeros(l_ref.shape, F32)
    acc_ref[...] = jnp.zeros(acc_ref.shape, F32)


def _attn_prompt_kernel(qt_ref, kc_ref, vct_ref, ks_ref, vst_ref, kw_ref, vwt_ref, g_ref,
                        o_ref, selb_ref, m_ref, acc_ref, out_ref, *, tq, tk):
    gi = pl.program_id(1)
    qi = pl.program_id(2)
    q0 = qi * tq
    n_blk = kc_ref.shape[2]
    wide = Q_PER_KV * tq
    qt = qt_ref[0, 0, 0]
    q_rows = qt[0:HEAD_DIM]
    pos = q0 + lax.broadcasted_iota(jnp.int32, (1, tq), 1)
    pos_w = q0 + (lax.broadcasted_iota(jnp.int32, (1, wide), 1) & (tq - 1))

    def gate(branch):
        rows = [g_ref[0, 0, pl.ds((gi * Q_PER_KV + r) * N_BRANCH + branch, 1), :] for r in range(Q_PER_KV)]
        return jnp.concatenate(rows, axis=1)

    blk_w = lax.broadcasted_iota(jnp.int32, (n_blk, wide), 0)
    p = _masked_softmax0(jnp.dot(kc_ref[0, 0], q_rows, preferred_element_type=F32),
                         blk_w * BLOCK + (BLOCK - 1) <= pos_w)
    o_c = jnp.dot(vct_ref[0, 0], p.astype(BF16), preferred_element_type=F32)
    out_ref[...] = gate(0) * o_c
    imp = p[:, 0:tq]
    for r in range(1, Q_PER_KV):
        imp = imp + p[:, r * tq:(r + 1) * tq]
    blk = lax.broadcasted_iota(jnp.int32, (n_blk, tq), 0)
    selb = _sel_to_bias(_select_blocks(imp, blk, pos >> BLOCK_SHIFT, n_blk))
    for r in range(Q_PER_KV):
        selb_ref[:, r * tq:(r + 1) * tq] = selb

    kpos_tile = lax.broadcasted_iota(jnp.int32, (tk, tq), 0)
    bpt = tk // BLOCK
    bias_pad = jnp.zeros((K_AUG - HEAD_DIM - bpt, wide), F32)

    cw = wide // COL_GROUPS
    q_lo = [(h * cw) % tq for h in range(COL_GROUPS)]

    def tile_update(k_ref, vt_ref, kt, q_op, bias, key_rows=None):
        k_tile = k_ref[0, 0, pl.ds(pl.multiple_of(kt * tk, tk), tk), :]
        vt = vt_ref[0, 0, kt]
        rows = [(0, tk) if key_rows is None else key_rows(h) for h in range(COL_GROUPS)]
        groups = [slice(h * cw, (h + 1) * cw) for h in range(COL_GROUPS)]
        scores = [jnp.dot(k_tile[r0:r1], q_op[:, cols], preferred_element_type=F32)
                  for (r0, r1), cols in zip(rows, groups)]
        for h, (s, (r0, r1), cols) in enumerate(zip(scores, rows, groups)):
            group_bias = None if bias is None else bias[r0:r1, q_lo[h]:q_lo[h] + cw]
            v_dot = lambda pr, r0=r0, r1=r1: jnp.dot(vt[:, r0:r1], pr, preferred_element_type=F32)
            _flash_update(s, group_bias, v_dot, m_ref.at[:, cols], None, acc_ref.at[:, cols])

    def finish(branch):
        o = acc_ref[0:HEAD_DIM, :] / jnp.maximum(acc_ref[HEAD_DIM:HEAD_DIM + 1, :], 1e-30)
        out_ref[...] = out_ref[...] + gate(branch) * o

    def q_with_block_bias(kt):
        tile_bias = selb_ref[pl.ds(pl.multiple_of(kt * bpt, bpt), bpt), :]
        return jnp.concatenate([q_rows, jnp.concatenate([tile_bias, bias_pad], axis=0).astype(BF16)], axis=0)

    kt_diag = (q0 + tq - 1) // tk
    _flash_init(m_ref, None, acc_ref)

    def sel_body(kt, carry):
        tile_update(ks_ref, vst_ref, kt, q_with_block_bias(kt), None)
        return carry

    lax.fori_loop(0, kt_diag, sel_body, 0)
    causal = jnp.where(kt_diag * tk + kpos_tile <= pos, 0.0, NEG)
    below_diag = lambda h: (0, q_lo[h] + cw)
    tile_update(ks_ref, vst_ref, kt_diag, q_with_block_bias(kt_diag), causal, below_diag)
    finish(1)

    _flash_init(m_ref, None, acc_ref)

    @pl.when(qi > 0)
    def _():
        older = jnp.where(pos - ((kt_diag - 1) * tk + kpos_tile) < WINDOW, 0.0, NEG)
        tile_update(kw_ref, vwt_ref, kt_diag - 1, qt, older, lambda h: (q_lo[h], tk))

    tile_update(kw_ref, vwt_ref, kt_diag, qt, causal, below_diag)
    finish(2)

    for pair in range(Q_PER_KV // 2):
        two = jnp.concatenate([out_ref[:, (2 * pair + h) * tq:(2 * pair + h + 1) * tq] for h in range(2)], axis=0)
        o_ref[:, pair * 2 * HEAD_DIM:(pair + 1) * 2 * HEAD_DIM] = two.T.astype(o_ref.dtype)


def _attn_prompt(qt, kc, vct, kas, vst, kaw, vwt, gates, tq, tk):
    b, g, nq, ka, wide = qt.shape
    t = nq * tq
    assert tq & (tq - 1) == 0 and wide == Q_PER_KV * tq and ka == K_AUG
    assert tq == tk == WINDOW and 2 * wide // COL_GROUPS <= tq
    n_blk = kc.shape[2]
    nkt = t // tk
    d = HEAD_DIM
    bg = lambda *tail: (lambda i, j, q: (i, j) + tail)
    kern = functools.partial(_attn_prompt_kernel, tq=tq, tk=tk)
    return pl.pallas_call(
        kern,
        grid=(b, g, nq),
        in_specs=[
            pl.BlockSpec((1, 1, 1, ka, wide), lambda i, j, q: (i, j, q, 0, 0)),
            pl.BlockSpec((1, 1, n_blk, d), bg(0, 0)),
            pl.BlockSpec((1, 1, d, n_blk), bg(0, 0)),
            pl.BlockSpec((1, 1, t, ka), bg(0, 0)),
            pl.BlockSpec((1, 1, nkt, V_AUG, tk), bg(0, 0, 0)),
            pl.BlockSpec((1, 1, t, ka), bg(0, 0)),
            pl.BlockSpec((1, 1, nkt, V_AUG, tk), bg(0, 0, 0)),
            pl.BlockSpec((1, 1, gates.shape[2], tq), lambda i, j, q: (i, q, 0, 0)),
        ],
        out_specs=pl.BlockSpec((tq, Q_PER_KV * d), lambda i, j, q: (i * nq + q, j)),
        out_shape=jax.ShapeDtypeStruct((b * t, ATTN_WIDTH), BF16),
        scratch_shapes=[
            pltpu.VMEM((n_blk, wide), F32),
            pltpu.VMEM((1, wide), F32),
            pltpu.VMEM((V_AUG, wide), F32),
            pltpu.VMEM((d, wide), F32),
        ],
        compiler_params=_cparams(("parallel", "parallel", "parallel")),
        name="attn_prompt",
    )(qt, kc, vct, kas, vst, kaw, vwt, gates)


def _attn_sample_kernel(pt_ref, *refs, pages_per_step, past_len, n_blk_valid):
    page_refs = refs[:pages_per_step]
    (qbd_ref, kc_ref, vc_ref, new_s_ref, win_ref, new_w_ref, gt_ref, pos_ref,
     o_ref, selb_ref, m_ref, l_ref, acc_ref, out_ref) = refs[pages_per_step:]
    step = pl.program_id(1)
    n_steps = pl.num_programs(1)
    ncol = qbd_ref.shape[2]
    kvw = qbd_ref.shape[1]
    qbd = qbd_ref[0]
    pos = pos_ref[...]
    group_cols = ncol // Q_PER_KV

    def tile_rows(rows, bias):
        k, v = rows[:, :kvw].astype(BF16), rows[:, kvw:].astype(BF16)
        s = jnp.dot(k, qbd, preferred_element_type=F32)
        v_dot = lambda p: lax.dot_general(v, p, TN_DIMS, preferred_element_type=F32)
        _flash_update(s, bias, v_dot, m_ref, l_ref, acc_ref)

    def tile_chan(page, bias):
        kt, vt = page[:kvw].astype(BF16), page[kvw:].astype(BF16)
        s = lax.dot_general(kt, qbd, TN_DIMS, preferred_element_type=F32)
        v_dot = lambda p: jnp.dot(vt, p, preferred_element_type=F32)
        _flash_update(s, bias, v_dot, m_ref, l_ref, acc_ref)

    def window_bias(n_keys, p0):
        delta = pos - (p0 + lax.broadcasted_iota(jnp.int32, (n_keys, ncol), 0))
        return jnp.where((delta >= 0) & (delta < WINDOW), 0.0, NEG)

    @pl.when(step == 0)
    def _():
        n_blk = kc_ref.shape[1]
        blk = lax.broadcasted_iota(jnp.int32, (n_blk, ncol), 0)
        cmask = (blk * BLOCK + (BLOCK - 1) <= pos) & (blk < n_blk_valid)
        s = jnp.dot(kc_ref[0], qbd, preferred_element_type=F32)
        p = _masked_softmax0(s, cmask)
        o_c = lax.dot_general(vc_ref[0], p.astype(BF16), TN_DIMS, preferred_element_type=F32)
        out_ref[...] = gt_ref[0, 0:1, :] * o_c
        imp = p
        for r in range(1, Q_PER_KV):
            imp = imp + pltpu.roll(p, r * group_cols, 1)
        selb_ref[...] = _sel_to_bias(_select_blocks(imp, blk, pos >> BLOCK_SHIFT, n_blk_valid))

        _flash_init(m_ref, l_ref, acc_ref)
        n_win = win_ref.shape[2]
        tile_rows(new_w_ref[0], window_bias(new_w_ref.shape[1], past_len))
        tile_chan(win_ref[0], window_bias(n_win, past_len - n_win))
        o_w = acc_ref[...] / jnp.maximum(l_ref[...], 1e-30)
        out_ref[...] = out_ref[...] + gt_ref[0, 2:3, :] * o_w

        _flash_init(m_ref, l_ref, acc_ref)
        n_new = new_s_ref.shape[1]
        kpos = past_len + lax.broadcasted_iota(jnp.int32, (n_new, ncol), 0)
        chosen = jnp.broadcast_to(selb_ref[pl.ds(past_len // BLOCK, 1), :], (n_new, ncol))
        tile_rows(new_s_ref[0], jnp.where(kpos <= pos, chosen, NEG))

    page_rows = page_refs[0].shape[2]
    step_keys = pages_per_step * page_rows
    step_blocks = step_keys // BLOCK
    keys = jnp.concatenate([page_refs[j][0] for j in range(pages_per_step)], axis=1)
    chosen = jnp.concatenate(
        [jnp.broadcast_to(selb_ref[pl.ds(step * step_blocks + i, 1), :], (BLOCK, ncol)) for i in range(step_blocks)],
        axis=0)
    tile_chan(keys, chosen)

    @pl.when(step == n_steps - 1)
    def _():
        o_s = acc_ref[...] / jnp.maximum(l_ref[...], 1e-30)
        o_ref[0] = out_ref[...] + gt_ref[0, 1:2, :] * o_s


def _attn_sample(pool_t, page0, page_table, qbd, kc, vc, new_s, win_t, win0, new_w, gt, pos,
                 pages_per_step=PAGES_PER_STEP):
    _, c, page_rows = pool_t.shape
    b, n_pages = page_table.shape
    pages_per_step = min(pages_per_step, n_pages)
    _, kvw, ncol = qbd.shape
    n_blk = kc.shape[1]
    past_len = n_pages * page_rows
    per_b = lambda shape: pl.BlockSpec((1,) + shape, lambda i, s, pt: (i,) + (0,) * len(shape))
    grid_spec = pltpu.PrefetchScalarGridSpec(
        num_scalar_prefetch=1,
        grid=(b, n_pages // pages_per_step),
        in_specs=_page_specs(c, page_rows, pages_per_step, page0) + [
            per_b((kvw, ncol)),
            per_b((n_blk, kvw)),
            per_b((n_blk, kvw)),
            per_b(new_s.shape[1:]),
            pl.BlockSpec((1,) + win_t.shape[1:], lambda i, s, pt: (win0 + i, 0, 0)),
            per_b(new_w.shape[1:]),
            per_b((N_BRANCH, ncol)),
            pl.BlockSpec((1, ncol), lambda i, s, pt: (0, 0)),
        ],
        out_specs=per_b((kvw, ncol)),
        scratch_shapes=[
            pltpu.VMEM((n_blk, ncol), F32),
            pltpu.VMEM((1, ncol), F32),
            pltpu.VMEM((1, ncol), F32),
            pltpu.VMEM((kvw, ncol), F32),
            pltpu.VMEM((kvw, ncol), F32),
        ],
    )
    kern = functools.partial(_attn_sample_kernel, pages_per_step=pages_per_step, past_len=past_len,
                             n_blk_valid=past_len // BLOCK + 1)
    return pl.pallas_call(
        kern,
        grid_spec=grid_spec,
        out_shape=jax.ShapeDtypeStruct((b, kvw, ncol), F32),
        compiler_params=_cparams(("parallel", "arbitrary")),
        name="attn_sample",
    )(page_table, *([pool_t] * pages_per_step), qbd, kc, vc, new_s, win_t, new_w, gt, pos)


def _outproj_kernel(a_ref, c_ref, x_ref, w_ref, g_ref, b_ref, h_ref, *, alpha):
    aw = a_ref.shape[1]
    mix = jnp.dot(a_ref[...], w_ref[:aw, :], preferred_element_type=F32)
    mix = mix + jnp.dot(c_ref[...], w_ref[aw:, :], preferred_element_type=F32)
    h_ref[...] = _layer_norm(alpha * x_ref[...] + mix, g_ref[...], b_ref[...])


def _outproj_ln(attn, conv, x, w_out, layer, g, b, alpha, tm):
    n, d = x.shape
    row = lambda w: pl.BlockSpec((tm, w), lambda i: (i, 0))
    const = lambda shape: pl.BlockSpec(shape, lambda i: (0, 0))
    w_spec = pl.BlockSpec((None,) + w_out.shape[1:], lambda i: (layer, 0, 0))
    return pl.pallas_call(
        functools.partial(_outproj_kernel, alpha=alpha),
        grid=(n // tm,),
        in_specs=[row(attn.shape[1]), row(conv.shape[1]), row(d), w_spec, const((1, d)), const((1, d))],
        out_specs=row(d),
        out_shape=jax.ShapeDtypeStruct((n, d), F32),
        compiler_params=_cparams(("parallel",)),
        name="outproj_ln",
    )(attn, conv, x, w_out, g, b)


def _mlp_kernel(h_ref, w1_ref, w2_ref, g_ref, b_ref, y_ref, acc_ref, hb_ref, *, alpha):
    f = pl.program_id(1)

    @pl.when(f == 0)
    def _():
        hb_ref[...] = h_ref[...].astype(BF16)
        acc_ref[...] = jnp.zeros(acc_ref.shape, F32)

    a = jnp.dot(hb_ref[...], w1_ref[...], preferred_element_type=F32)
    a = jnp.square(jnp.maximum(a, 0.0)).astype(BF16)
    acc_ref[...] += jnp.dot(a, w2_ref[...], preferred_element_type=F32)

    @pl.when(f == pl.num_programs(1) - 1)
    def _():
        y_ref[...] = _layer_norm(alpha * h_ref[...] + acc_ref[...], g_ref[...], b_ref[...])


def _mlp_ln(h, w1, w2, layer, g, b, alpha, tm, tf):
    n, d = h.shape
    dff = w1.shape[2]
    return pl.pallas_call(
        functools.partial(_mlp_kernel, alpha=alpha),
        grid=(n // tm, dff // tf),
        in_specs=[
            pl.BlockSpec((tm, d), lambda i, f: (i, 0)),
            pl.BlockSpec((None, d, tf), lambda i, f: (layer, 0, f)),
            pl.BlockSpec((None, tf, d), lambda i, f: (layer, f, 0)),
            pl.BlockSpec((1, d), lambda i, f: (0, 0)),
            pl.BlockSpec((1, d), lambda i, f: (0, 0)),
        ],
        out_specs=pl.BlockSpec((tm, d), lambda i, f: (i, 0)),
        out_shape=jax.ShapeDtypeStruct((n, d), F32),
        scratch_shapes=[pltpu.VMEM((tm, d), F32), pltpu.VMEM((tm, d), BF16)],
        compiler_params=_cparams(("parallel", "arbitrary")),
        name="mlp_ln",
    )(h, w1, w2, g, b)


def _split_kv(kv, b, t):
    return kv.reshape(b, t, 2, KV_HEADS, HEAD_DIM)


def _prompt_layer(x, p, b, t, alpha, tm, tq, tk):
    w_c = jnp.tile(p["w_cmp"].T, (1, tm // BLOCK))
    kvt_c, kvt_s, kvt_w, vst, vwt, kas, kaw, qt, gates, cmp = _proj_prompt(
        x, p["w_t"], p["w_k"], w_c, b, t, tm, tq, tk)
    conv, zlast = _conv_prompt(x, p["w_b"], p["conv_w"], t, tm)
    n_blk = t // BLOCK
    cmp = cmp.reshape(b, n_blk, 2, KV_HEADS, HEAD_DIM).astype(BF16)
    kc = cmp[:, :, 0].transpose(0, 2, 1, 3)
    vct = cmp[:, :, 1].transpose(0, 2, 3, 1)
    attn = _attn_prompt(qt, kc, vct, kas, vst, kaw, vwt, gates, tq, tk)
    h = _outproj_ln(attn, conv, x, p["w_out"], p["layer"], p["ln1_g"], p["ln1_b"], alpha, tm)
    y = _mlp_ln(h, p["w_mlp1"], p["w_mlp2"], p["layer"], p["ln2_g"], p["ln2_b"], alpha, min(MLP_ROWS, b * t), MLP_COLS)
    conv_state = zlast.reshape(b, t // tm, SUBLANES, -1)[:, -1, SUBLANES - (CONV_W - 1):]
    keep = min(WINDOW, t)
    rows_major = lambda kvt: kvt.reshape(b, 2, KV_HEADS, HEAD_DIM, t).transpose(0, 4, 1, 2, 3)
    return (y, rows_major(kvt_c), rows_major(kvt_s), rows_major(kvt_w)[:, t - keep:], conv_state)


def _sample_layer(x, p, pool_c, pool_s, page0, win_t, win0, cache_win, state_conv, page_table, db, dt, alpha):
    n = db * dt
    page_rows = pool_c.shape[2]
    n_pages = page_table.shape[1]
    past_len = n_pages * page_rows
    c = 2 * KV_WIDTH
    q, kv_c, kv_s, kv_w, gates = _proj(x, p["w_a"], n)

    zp = jnp.pad(state_conv, ((0, 0), (SUBLANES - (CONV_W - 1), 0), (0, 0))).reshape(n, -1)
    conv, z = _conv_sample(x, zp, p["w_b"], p["conv_w"])
    conv_state = jnp.concatenate([state_conv, z.reshape(db, dt, -1)], axis=1)[:, -(CONV_W - 1):]

    cmp_past = _cmp_pages(pool_c, page0, page_table, p["w_cmp_t"])
    tail = jnp.pad(kv_c.reshape(db, dt, c), ((0, 0), (0, SUBLANES * BLOCK - dt), (0, 0)))
    cmp_tail = _cmp_rows(tail, p["w_cmp"], SUBLANES)
    cmp_all = jnp.concatenate([cmp_past, cmp_tail], axis=1).astype(BF16)
    kc, vc = cmp_all[..., :KV_WIDTH], cmp_all[..., KV_WIDTH:]

    q5 = q.reshape(db, dt, KV_HEADS, Q_PER_KV, HEAD_DIM).transpose(0, 2, 4, 3, 1)
    eye = jnp.eye(KV_HEADS, dtype=q.dtype)
    qbd = (q5[:, :, :, :, None, :] * eye[None, :, None, None, :, None]).reshape(db, KV_WIDTH, N_HEADS * dt)
    gt = gates[:, :N_HEADS * N_BRANCH].reshape(db, dt, KV_HEADS, Q_PER_KV, N_BRANCH)
    gt = gt.transpose(0, 4, 3, 2, 1).reshape(db, N_BRANCH, N_HEADS * dt)
    pos = jnp.tile(past_len + jnp.arange(dt, dtype=jnp.int32), N_HEADS).reshape(1, N_HEADS * dt)
    new_rows = lambda kv: jnp.pad(kv.reshape(db, dt, c), ((0, 0), (0, 2 * SUBLANES - dt), (0, 0)))
    o = _attn_sample(pool_s, page0, page_table, qbd, kc, vc, new_rows(kv_s), win_t, win0, new_rows(kv_w), gt, pos)
    o = o.reshape(db, KV_HEADS, HEAD_DIM, Q_PER_KV, KV_HEADS, dt)
    o = jnp.stack([o[:, g, :, :, g, :] for g in range(KV_HEADS)], axis=1)
    attn = o.transpose(0, 4, 1, 3, 2).reshape(n, ATTN_WIDTH).astype(BF16)

    h = _outproj_ln(attn, conv, x, p["w_out"], p["layer"], p["ln1_g"], p["ln1_b"], alpha, n)
    y = _mlp_ln(h, p["w_mlp1"], p["w_mlp2"], p["layer"], p["ln2_g"], p["ln2_b"], alpha, n, 1024)
    win_keep = cache_win.shape[1]
    new_win = jnp.concatenate([cache_win, _split_kv(kv_w, db, dt)], axis=1)[:, -win_keep:]
    return y, _split_kv(kv_c, db, dt), _split_kv(kv_s, db, dt), new_win, conv_state


def kernel(x_prompt, x_sample, cache_cmp, cache_slc, cache_win, state_conv, page_table, w_in, w_cmp_k, w_cmp_v,
           conv_w, w_out, ln1_g, ln1_b, w_mlp1, w_mlp2, ln2_g, ln2_b):
    depth = w_in.shape[0]
    b, t, d = x_prompt.shape
    db, dt, _ = x_sample.shape
    alpha = (2.0 * depth) ** 0.25
    qkvg = ATTN_WIDTH + 6 * KV_WIDTH + N_HEADS * N_BRANCH
    tm = min(512, t)
    tq, tk = min(512, t), min(512, t)

    n_pool, page_rows = cache_cmp.shape[1:3]
    chan_major = lambda c: c.transpose(0, 1, 3, 4, 5, 2).reshape(c.shape[0] * c.shape[1], -1, c.shape[2])
    pool_c, pool_s, win_t = chan_major(cache_cmp), chan_major(cache_slc), chan_major(cache_win)

    w_out_b, w_mlp1_b, w_mlp2_b = w_out.astype(BF16), w_mlp1.astype(BF16), w_mlp2.astype(BF16)
    yp = x_prompt.reshape(b * t, d)
    ys = x_sample.reshape(db * dt, d)
    outs = [[] for _ in range(8)]
    for l in range(depth):
        w_a = jnp.pad(w_in[l][:, :qkvg], ((0, 0), (0, GATE_PAD - N_HEADS * N_BRANCH))).astype(BF16)
        w_cmp = jnp.concatenate([jnp.tile(w_cmp_k[l], (1, KV_HEADS)), jnp.tile(w_cmp_v[l], (1, KV_HEADS))], axis=1)
        w_t = jnp.pad(w_in[l].T[:qkvg], ((0, (-qkvg) % 64), (0, 0))).astype(BF16)
        k_cols = w_in[l][:, ATTN_WIDTH + 2 * KV_WIDTH:ATTN_WIDTH + 6 * KV_WIDTH]
        k_cols = k_cols.reshape(d, 2, 2, KV_HEADS, HEAD_DIM)[:, :, 0]
        w_k = jnp.pad(k_cols, ((0, 0), (0, 0), (0, 0), (0, K_AUG - HEAD_DIM))).reshape(d, -1).astype(BF16)
        p = {
            "w_a": w_a,
            "w_t": w_t,
            "w_k": w_k,
            "w_b": w_in[l][:, qkvg:].astype(BF16),
            "w_cmp": w_cmp,
            "w_cmp_t": jnp.tile(w_cmp.T, (1, page_rows // BLOCK)),
            "conv_w": conv_w[l],
            "layer": l,
            "w_out": w_out_b,
            "ln1_g": ln1_g[l].reshape(1, d), "ln1_b": ln1_b[l].reshape(1, d),
            "w_mlp1": w_mlp1_b, "w_mlp2": w_mlp2_b,
            "ln2_g": ln2_g[l].reshape(1, d), "ln2_b": ln2_b[l].reshape(1, d),
        }
        yp, c1, s1, w1, v1 = _prompt_layer(yp, p, b, t, alpha, tm, tq, tk)
        ys, c2, s2, w2, v2 = _sample_layer(ys, p, pool_c, pool_s, l * n_pool, win_t, l * db, cache_win[l],
                                           state_conv[l], page_table, db, dt, alpha)
        for lst, v in zip(outs, (c1, s1, w1, v1, c2, s2, w2, v2)):
            lst.append(v)
    stacked = [jnp.stack(o) for o in outs]
    return (yp.reshape(b, t, d), ys.reshape(db, dt, d), *stacked)
```

```python
import functools
import math

import jax
import jax.numpy as jnp
from jax import lax
from jax.experimental import pallas as pl
from jax.experimental.pallas import tpu as pltpu

HEAD_DIM = 64
KV_HEADS = 4
Q_PER_KV = 4
N_HEADS = KV_HEADS * Q_PER_KV
ATTN_WIDTH = N_HEADS * HEAD_DIM
KV_WIDTH = KV_HEADS * HEAD_DIM
N_BRANCH = 3
BLOCK = 64
BLOCK_SHIFT = BLOCK.bit_length() - 1
N_SEL = 16
WINDOW = 512
CONV_W = 3
LN_EPS = 1e-5
NEG = -1e30
FORCED_SCORE = 1e9
Q_SCALE = HEAD_DIM ** -0.5 * math.log2(math.e)
GATE_PAD = 128

SUBLANES = 8
VMEM_LIMIT = 56 * 1024 * 1024

BF16 = jnp.bfloat16
F32 = jnp.float32
NT_DIMS = (((1,), (1,)), ((), ()))
TN_DIMS = (((0,), (0,)), ((), ()))

ROW_Q = 0
ROW_KV = ATTN_WIDTH
ROW_GATE = ROW_KV + 6 * KV_WIDTH
ROW_CONV = ROW_GATE + N_HEADS * N_BRANCH


def _w_in_spec(w_all, layer):
    return pl.BlockSpec((None,) + w_all.shape[1:], lambda *_: (layer, 0, 0), pipeline_mode=pl.Buffered(1))


def _cparams(sem):
    return pltpu.CompilerParams(dimension_semantics=sem, vmem_limit_bytes=VMEM_LIMIT)


def _layer_norm(v, g, b):
    mu = jnp.mean(v, axis=-1, keepdims=True)
    d = v - mu
    var = jnp.mean(d * d, axis=-1, keepdims=True)
    return d * lax.rsqrt(var + LN_EPS) * g + b


def _proj_kernel(x_ref, w_ref, q_ref, kc_ref, ks_ref, kw_ref, g_ref):
    x = x_ref[...].astype(BF16)

    def mm(r0, r1):
        return lax.dot_general(x, w_ref[r0:r1, :], NT_DIMS, preferred_element_type=F32)

    kv2 = 2 * KV_WIDTH
    q_ref[...] = (mm(ROW_Q, ROW_KV) * Q_SCALE).astype(BF16)
    kc_ref[...] = mm(ROW_KV, ROW_KV + kv2)
    ks_ref[...] = mm(ROW_KV + kv2, ROW_KV + 2 * kv2)
    kw_ref[...] = mm(ROW_KV + 2 * kv2, ROW_KV + 3 * kv2)
    logits = mm(ROW_GATE, ROW_GATE + GATE_PAD)
    g_ref[...] = 1.0 / (1.0 + jnp.exp(-logits))


def _proj(x, w_all, layer, tm):
    n, d = x.shape
    row = lambda i: (i, 0)
    return pl.pallas_call(
        _proj_kernel,
        grid=(n // tm,),
        in_specs=[pl.BlockSpec((tm, d), row), _w_in_spec(w_all, layer)],
        out_specs=[
            pl.BlockSpec((tm, ATTN_WIDTH), row),
            pl.BlockSpec((tm, 2 * KV_WIDTH), row),
            pl.BlockSpec((tm, 2 * KV_WIDTH), row),
            pl.BlockSpec((tm, 2 * KV_WIDTH), row),
            pl.BlockSpec((tm, GATE_PAD), row),
        ],
        out_shape=[
            jax.ShapeDtypeStruct((n, ATTN_WIDTH), BF16),
            jax.ShapeDtypeStruct((n, 2 * KV_WIDTH), F32),
            jax.ShapeDtypeStruct((n, 2 * KV_WIDTH), F32),
            jax.ShapeDtypeStruct((n, 2 * KV_WIDTH), F32),
            jax.ShapeDtypeStruct((n, GATE_PAD), F32),
        ],
        compiler_params=_cparams(("parallel",)),
        name="proj",
    )(x, w_all)


K_AUG = 2 * HEAD_DIM
V_AUG = HEAD_DIM + 16
MLP_ROWS, MLP_COLS = 512, 1024
PAGES_PER_STEP = 16
COL_GROUPS = 8


def _block_sums_t(kvt, w_t, n_out=None, blk0=0):
    rows = kvt.shape[1]
    nb = rows // BLOCK if n_out is None else n_out
    pw = kvt * w_t
    hi = pw.astype(BF16)
    lo = (pw - hi.astype(F32)).astype(BF16)
    member = jnp.where(lax.broadcasted_iota(jnp.int32, (nb, rows), 0)
                       == blk0 + (lax.broadcasted_iota(jnp.int32, (nb, rows), 1) >> BLOCK_SHIFT),
                       1.0, 0.0).astype(BF16)
    return (lax.dot_general(member, hi, NT_DIMS, preferred_element_type=F32)
            + lax.dot_general(member, lo, NT_DIMS, preferred_element_type=F32))


def _proj_prompt_kernel(x_ref, wt_ref, wc_ref, kvc_ref, kvs_ref, kvw_ref, vts_ref, vtw_ref,
                        kas_ref, kaw_ref, qt_ref, g_ref, cmp_ref, *, tq, tk):
    tm = x_ref.shape[0]
    x = x_ref[...].astype(BF16)

    def nt(r0, r1):
        return lax.dot_general(wt_ref[r0:r1, :], x, NT_DIMS, preferred_element_type=F32)

    c = ROW_KV
    kv2 = 2 * KV_WIDTH
    qall = (nt(ROW_Q, ROW_KV) * Q_SCALE).astype(BF16)
    pad_rows = jnp.zeros((K_AUG - HEAD_DIM, Q_PER_KV * tq), BF16)
    for g in range(KV_HEADS):
        for j in range(tm // tq):
            heads = [qall[(g * Q_PER_KV + r) * HEAD_DIM:(g * Q_PER_KV + r + 1) * HEAD_DIM, j * tq:(j + 1) * tq]
                     for r in range(Q_PER_KV)]
            qt_ref[0, g, j, 0:HEAD_DIM, :] = jnp.concatenate(heads, axis=1)
            qt_ref[0, g, j, HEAD_DIM:K_AUG, :] = pad_rows

    kvt_c = nt(c, c + kv2)
    kvc_ref[0] = kvt_c
    cmp_ref[0] = _block_sums_t(kvt_c, wc_ref[...])

    hot_row = lax.broadcasted_iota(jnp.int32, (K_AUG - HEAD_DIM, tm), 0)
    row_blk = (lax.broadcasted_iota(jnp.int32, (K_AUG - HEAD_DIM, tm), 1) >> BLOCK_SHIFT) & (tk // BLOCK - 1)
    one_hot_t = jnp.where(hot_row == row_blk, 1.0, 0.0)
    sum_rows = jnp.where(lax.broadcasted_iota(jnp.int32, (V_AUG - HEAD_DIM, tk), 0) == 0, 1.0, 0.0).astype(BF16)
    for i, (kv_ref, vt_ref, ka_ref) in enumerate(((kvs_ref, vts_ref, kas_ref), (kvw_ref, vtw_ref, kaw_ref))):
        kvt = nt(c + (i + 1) * kv2, c + (i + 2) * kv2)
        kv_ref[0] = kvt
        for g in range(KV_HEADS):
            v_rows = kvt[KV_WIDTH + g * HEAD_DIM:KV_WIDTH + (g + 1) * HEAD_DIM]
            for jk in range(tm // tk):
                vt_ref[0, g, jk, 0:HEAD_DIM, :] = v_rows[:, jk * tk:(jk + 1) * tk].astype(BF16)
                vt_ref[0, g, jk, HEAD_DIM:V_AUG, :] = sum_rows
            k_t = kvt[g * HEAD_DIM:(g + 1) * HEAD_DIM]
            ka_ref[0, g] = jnp.concatenate([k_t, one_hot_t], axis=0).T.astype(BF16)

    logits = nt(ROW_GATE, ROW_GATE + g_ref.shape[2])
    gates = 1.0 / (1.0 + jnp.exp(-logits))
    for j in range(tm // tq):
        g_ref[0, j] = gates[:, j * tq:(j + 1) * tq]


def _proj_prompt(x, w_all, layer, w_c, b, t, tm, tq, tk):
    n, d = x.shape
    tiles = t // tm
    g_rows = 64
    const = lambda a: pl.BlockSpec(a.shape, lambda i, j: (0,) * a.ndim, pipeline_mode=pl.Buffered(1))
    kvt_spec = pl.BlockSpec((1, 2 * KV_WIDTH, tm), lambda i, j: (i, 0, j))
    vt_spec = pl.BlockSpec((1, KV_HEADS, tm // tk, V_AUG, tk), lambda i, j: (i, 0, j, 0, 0))
    ka_spec = pl.BlockSpec((1, KV_HEADS, tm, K_AUG), lambda i, j: (i, 0, j, 0))
    kvt_shape = jax.ShapeDtypeStruct((b, 2 * KV_WIDTH, t), F32)
    vt_shape = jax.ShapeDtypeStruct((b, KV_HEADS, t // tk, V_AUG, tk), BF16)
    ka_shape = jax.ShapeDtypeStruct((b, KV_HEADS, t, K_AUG), BF16)
    wide = Q_PER_KV * tq
    return pl.pallas_call(
        functools.partial(_proj_prompt_kernel, tq=tq, tk=tk),
        grid=(b, tiles),
        in_specs=[pl.BlockSpec((tm, d), lambda i, j: (i * tiles + j, 0)), _w_in_spec(w_all, layer), const(w_c)],
        out_specs=[
            kvt_spec, kvt_spec, kvt_spec, vt_spec, vt_spec, ka_spec, ka_spec,
            pl.BlockSpec((1, KV_HEADS, tm // tq, K_AUG, wide), lambda i, j: (i, 0, j, 0, 0)),
            pl.BlockSpec((1, tm // tq, g_rows, tq), lambda i, j: (i, j, 0, 0)),
            pl.BlockSpec((1, tm // BLOCK, 2 * KV_WIDTH), lambda i, j: (i, j, 0)),
        ],
        out_shape=[
            kvt_shape, kvt_shape, kvt_shape, vt_shape, vt_shape, ka_shape, ka_shape,
            jax.ShapeDtypeStruct((b, KV_HEADS, t // tq, K_AUG, wide), BF16),
            jax.ShapeDtypeStruct((b, t // tq, g_rows, tq), F32),
            jax.ShapeDtypeStruct((b, t // BLOCK, 2 * KV_WIDTH), F32),
        ],
        compiler_params=_cparams(("parallel", "parallel")),
        name="proj_prompt",
    )(x, w_all, w_c)


def _conv_from_z(z, zp, bgate, cw_ref, c0, c1):
    rows = z.shape[0]
    sub = lax.broadcasted_iota(jnp.int32, z.shape, 0) & (SUBLANES - 1)
    z1 = jnp.where(sub >= 1, pltpu.roll(z, 1, 0), pltpu.roll(zp, rows - (SUBLANES - 1), 0))
    z2 = jnp.where(sub >= 2, pltpu.roll(z, 2, 0), pltpu.roll(zp, rows - (SUBLANES - 2), 0))
    w0 = cw_ref[0:1, c0:c1]
    w1 = cw_ref[1:2, c0:c1]
    w2 = cw_ref[2:3, c0:c1]
    return bgate * (w2 * z + w0 * z2 + w1 * z1)


def _conv_prompt_kernel(x_ref, w_ref, cw_ref, y_ref, zl_ref, carry_ref, *, chunk):
    cdim = y_ref.shape[1]
    x = x_ref[...].astype(BF16)

    @pl.when(pl.program_id(1) == 0)
    def _():
        carry_ref[...] = jnp.zeros(carry_ref.shape, F32)

    for c0 in range(0, cdim, chunk):
        c1 = c0 + chunk

        def mm(off):
            rows = w_ref[ROW_CONV + off + c0:ROW_CONV + off + c1, :]
            return lax.dot_general(x, rows, NT_DIMS, preferred_element_type=F32)

        z = mm(2 * cdim) * mm(0)
        bgate = mm(cdim)
        zp = jnp.concatenate([carry_ref[:, c0:c1], z[:-SUBLANES]], axis=0)
        y_ref[:, c0:c1] = _conv_from_z(z, zp, bgate, cw_ref, c0, c1).astype(BF16)
        carry_ref[:, c0:c1] = z[-SUBLANES:]
        zl_ref[:, c0:c1] = z[-SUBLANES:]


def _conv_prompt(x, w_all, layer, cw, seq, tm, chunk=512):
    n, d = x.shape
    cdim = cw.shape[1]
    tiles = seq // tm
    return pl.pallas_call(
        functools.partial(_conv_prompt_kernel, chunk=chunk),
        grid=(n // seq, tiles),
        in_specs=[
            pl.BlockSpec((tm, d), lambda i, j: (i * tiles + j, 0)),
            _w_in_spec(w_all, layer),
            pl.BlockSpec((CONV_W, cdim), lambda i, j: (0, 0)),
        ],
        out_specs=[
            pl.BlockSpec((tm, cdim), lambda i, j: (i * tiles + j, 0)),
            pl.BlockSpec((SUBLANES, cdim), lambda i, j: (i * tiles + j, 0)),
        ],
        out_shape=[
            jax.ShapeDtypeStruct((n, cdim), BF16),
            jax.ShapeDtypeStruct((n // tm * SUBLANES, cdim), F32),
        ],
        scratch_shapes=[pltpu.VMEM((SUBLANES, cdim), F32)],
        compiler_params=_cparams(("parallel", "arbitrary")),
        name="conv_prompt",
    )(x, w_all, cw)


def _conv_sample_kernel(x_ref, zp_ref, w_ref, cw_ref, y_ref, z_ref, *, chunk):
    cdim = y_ref.shape[1]
    x = x_ref[...].astype(BF16)
    for c0 in range(0, cdim, chunk):
        c1 = c0 + chunk

        def mm(off):
            rows = w_ref[ROW_CONV + off + c0:ROW_CONV + off + c1, :]
            return lax.dot_general(x, rows, NT_DIMS, preferred_element_type=F32)

        z = mm(2 * cdim) * mm(0)
        bgate = mm(cdim)
        y_ref[:, c0:c1] = _conv_from_z(z, zp_ref[:, c0:c1], bgate, cw_ref, c0, c1).astype(BF16)
        z_ref[:, c0:c1] = z


def _conv_sample(x, zp, w_all, layer, cw, chunk=512):
    n, d = x.shape
    cdim = cw.shape[1]
    full = lambda shape: pl.BlockSpec(shape, lambda i: (0, 0))
    return pl.pallas_call(
        functools.partial(_conv_sample_kernel, chunk=chunk),
        grid=(1,),
        in_specs=[full((n, d)), full((n, cdim)), _w_in_spec(w_all, layer), full((CONV_W, cdim))],
        out_specs=[full((n, cdim)), full((n, cdim))],
        out_shape=[jax.ShapeDtypeStruct((n, cdim), BF16), jax.ShapeDtypeStruct((n, cdim), F32)],
        compiler_params=_cparams(("arbitrary",)),
        name="conv_sample",
    )(x, zp, w_all, cw)


def _cmp_rows_kernel(kv_ref, w_ref, o_ref):
    nb = o_ref.shape[1]
    x = kv_ref[0].reshape(nb, BLOCK, kv_ref.shape[2])
    o_ref[0] = jnp.sum(x * w_ref[...][None], axis=1)


def _cmp_rows(kv, w_cmp, nb):
    b, l, c = kv.shape
    return pl.pallas_call(
        _cmp_rows_kernel,
        grid=(b, l // (nb * BLOCK)),
        in_specs=[
            pl.BlockSpec((1, nb * BLOCK, c), lambda i, j: (i, j, 0)),
            pl.BlockSpec((BLOCK, c), lambda i, j: (0, 0)),
        ],
        out_specs=pl.BlockSpec((1, nb, c), lambda i, j: (i, j, 0)),
        out_shape=jax.ShapeDtypeStruct((b, l // BLOCK, c), F32),
        compiler_params=_cparams(("parallel", "parallel")),
        name="cmp_rows",
    )(kv, w_cmp)


def _cmp_pages_kernel(pt_ref, *refs, pages_per_step):
    page_refs = refs[:pages_per_step]
    w_ref = refs[pages_per_step]
    o_ref = refs[pages_per_step + 1]
    w = w_ref[...]
    nb = o_ref.shape[1]
    bpp = w.shape[1] // BLOCK
    acc = jnp.zeros(o_ref.shape[1:], F32)
    for j in range(pages_per_step):
        acc = acc + _block_sums_t(page_refs[j][0], w, nb, j * bpp)
    o_ref[0] = acc


def _page_specs(c, page_rows, pages_per_step, page0):
    def spec(j):
        return pl.BlockSpec((1, c, page_rows), lambda b, s, pt: (page0 + pt[b, s * pages_per_step + j], 0, 0))
    return [spec(j) for j in range(pages_per_step)]


def _cmp_pages(pool_t, page0, page_table, w_cmp_t, pages_per_step=PAGES_PER_STEP):
    _, c, page_rows = pool_t.shape
    b, n_pages = page_table.shape
    pages_per_step = min(pages_per_step, n_pages)
    nb = pages_per_step * page_rows // BLOCK
    grid_spec = pltpu.PrefetchScalarGridSpec(
        num_scalar_prefetch=1,
        grid=(b, n_pages // pages_per_step),
        in_specs=_page_specs(c, page_rows, pages_per_step, page0)
        + [pl.BlockSpec((c, page_rows), lambda i, s, pt: (0, 0))],
        out_specs=pl.BlockSpec((1, nb, c), lambda i, s, pt: (i, s, 0)),
    )
    return pl.pallas_call(
        functools.partial(_cmp_pages_kernel, pages_per_step=pages_per_step),
        grid_spec=grid_spec,
        out_shape=jax.ShapeDtypeStruct((b, n_pages * page_rows // BLOCK, c), F32),
        compiler_params=_cparams(("parallel", "parallel")),
        name="cmp_pages",
    )(page_table, *([pool_t] * pages_per_step), w_cmp_t)


def _masked_softmax0(s, mask):
    s = jnp.where(mask, s, NEG)
    m = jnp.max(s, axis=0, keepdims=True)
    e = jnp.where(mask, jnp.exp2(s - m), 0.0)
    return e / jnp.maximum(jnp.sum(e, axis=0, keepdims=True), 1e-30)


def _topk_mask0(score, blk, n_sel):
    big = jnp.int32(2 ** 30)
    sel = jnp.zeros(score.shape, F32)
    for _ in range(n_sel):
        m = jnp.max(score, axis=0, keepdims=True)
        idx = jnp.min(jnp.where(score == m, blk, big), axis=0, keepdims=True)
        pick = blk == idx
        sel = jnp.where(pick, 1.0, sel)
        score = jnp.where(pick, -jnp.inf, score)
    return sel


def _select_blocks(imp, blk, cur, n_valid_blk):
    forced = (blk == 0) | (blk == cur) | (blk == cur - 1)
    cand = blk <= cur
    score = jnp.where(forced, FORCED_SCORE, jnp.where(cand, imp, -FORCED_SCORE))
    score = jnp.where(blk < n_valid_blk, score, -jnp.inf)
    return _topk_mask0(score, blk, min(N_SEL, n_valid_blk))


def _sel_to_bias(sel):
    return (sel - 1.0) * (-NEG)


def _flash_update(s, bias, v_dot, m_ref, l_ref, acc_ref):
    if bias is not None:
        s = bias + s
    m_old = m_ref[...]
    m_new = jnp.maximum(m_old, jnp.max(s, axis=0, keepdims=True))
    alpha = jnp.exp2(m_old - m_new)
    p = jnp.exp2(s - m_new)
    if l_ref is not None:
        l_ref[...] = alpha * l_ref[...] + jnp.sum(p, axis=0, keepdims=True)
    acc_ref[...] = alpha * acc_ref[...] + v_dot(p.astype(BF16))
    m_ref[...] = m_new


def _flash_init(m_ref, l_ref, acc_ref):
    m_ref[...] = jnp.full(m_ref.shape, NEG, F32)
    if l_ref is not None:
        l_ref[...] = jnp.zeros(l_ref.shape, F32)
    acc_ref[...] = jnp.zeros(acc_ref.shape, F32)


def _attn_prompt_kernel(qt_ref, kc_ref, vct_ref, ks_ref, vst_ref, kw_ref, vwt_ref, g_ref,
                        o_ref, selb_ref, m_ref, acc_ref, out_ref, *, tq, tk):
    gi = pl.program_id(1)
    qi = pl.program_id(2)
    q0 = qi * tq
    n_blk = kc_ref.shape[2]
    wide = Q_PER_KV * tq
    qt = qt_ref[0, 0, 0]
    q_rows = qt[0:HEAD_DIM]
    pos = q0 + lax.broadcasted_iota(jnp.int32, (1, tq), 1)
    pos_w = q0 + (lax.broadcasted_iota(jnp.int32, (1, wide), 1) & (tq - 1))

    def gate(branch):
        rows = [g_ref[0, 0, pl.ds((gi * Q_PER_KV + r) * N_BRANCH + branch, 1), :] for r in range(Q_PER_KV)]
        return jnp.concatenate(rows, axis=1)

    blk_w = lax.broadcasted_iota(jnp.int32, (n_blk, wide), 0)
    p = _masked_softmax0(jnp.dot(kc_ref[0, 0], q_rows, preferred_element_type=F32),
                         blk_w * BLOCK + (BLOCK - 1) <= pos_w)
    o_c = jnp.dot(vct_ref[0, 0], p.astype(BF16), preferred_element_type=F32)
    out_ref[...] = gate(0) * o_c
    imp = p[:, 0:tq]
    for r in range(1, Q_PER_KV):
        imp = imp + p[:, r * tq:(r + 1) * tq]
    blk = lax.broadcasted_iota(jnp.int32, (n_blk, tq), 0)
    selb = _sel_to_bias(_select_blocks(imp, blk, pos >> BLOCK_SHIFT, n_blk))
    for r in range(Q_PER_KV):
        selb_ref[:, r * tq:(r + 1) * tq] = selb

    kpos_tile = lax.broadcasted_iota(jnp.int32, (tk, tq), 0)
    bpt = tk // BLOCK
    bias_pad = jnp.zeros((K_AUG - HEAD_DIM - bpt, wide), F32)

    cw = wide // COL_GROUPS
    q_lo = [(h * cw) % tq for h in range(COL_GROUPS)]

    def tile_update(k_ref, vt_ref, kt, q_op, bias, key_rows=None):
        k_tile = k_ref[0, 0, pl.ds(pl.multiple_of(kt * tk, tk), tk), :]
        vt = vt_ref[0, 0, kt]
        rows = [(0, tk) if key_rows is None else key_rows(h) for h in range(COL_GROUPS)]
        groups = [slice(h * cw, (h + 1) * cw) for h in range(COL_GROUPS)]
        scores = [jnp.dot(k_tile[r0:r1], q_op[:, cols], preferred_element_type=F32)
                  for (r0, r1), cols in zip(rows, groups)]
        for h, (s, (r0, r1), cols) in enumerate(zip(scores, rows, groups)):
            group_bias = None if bias is None else bias[r0:r1, q_lo[h]:q_lo[h] + cw]
            v_dot = lambda pr, r0=r0, r1=r1: jnp.dot(vt[:, r0:r1], pr, preferred_element_type=F32)
            _flash_update(s, group_bias, v_dot, m_ref.at[:, cols], None, acc_ref.at[:, cols])

    def finish(branch):
        o = acc_ref[0:HEAD_DIM, :] / jnp.maximum(acc_ref[HEAD_DIM:HEAD_DIM + 1, :], 1e-30)
        out_ref[...] = out_ref[...] + gate(branch) * o

    def q_with_block_bias(kt):
        tile_bias = selb_ref[pl.ds(pl.multiple_of(kt * bpt, bpt), bpt), :]
        return jnp.concatenate([q_rows, jnp.concatenate([tile_bias, bias_pad], axis=0).astype(BF16)], axis=0)

    kt_diag = (q0 + tq - 1) // tk
    _flash_init(m_ref, None, acc_ref)

    def sel_body(kt, carry):
        tile_update(ks_ref, vst_ref, kt, q_with_block_bias(kt), None)
        return carry

    lax.fori_loop(0, kt_diag, sel_body, 0)
    causal = jnp.where(kt_diag * tk + kpos_tile <= pos, 0.0, NEG)
    below_diag = lambda h: (0, q_lo[h] + cw)
    tile_update(ks_ref, vst_ref, kt_diag, q_with_block_bias(kt_diag), causal, below_diag)
    finish(1)

    _flash_init(m_ref, None, acc_ref)

    @pl.when(qi > 0)
    def _():
        older = jnp.where(pos - ((kt_diag - 1) * tk + kpos_tile) < WINDOW, 0.0, NEG)
        tile_update(kw_ref, vwt_ref, kt_diag - 1, qt, older, lambda h: (q_lo[h], tk))

    tile_update(kw_ref, vwt_ref, kt_diag, qt, causal, below_diag)
    finish(2)

    for pair in range(Q_PER_KV // 2):
        two = jnp.concatenate([out_ref[:, (2 * pair + h) * tq:(2 * pair + h + 1) * tq] for h in range(2)], axis=0)
        o_ref[:, pair * 2 * HEAD_DIM:(pair + 1) * 2 * HEAD_DIM] = two.T.astype(o_ref.dtype)


def _attn_prompt(qt, kc, vct, kas, vst, kaw, vwt, gates, tq, tk):
    b, g, nq, ka, wide = qt.shape
    t = nq * tq
    assert tq & (tq - 1) == 0 and wide == Q_PER_KV * tq and ka == K_AUG
    assert tq == tk == WINDOW and 2 * wide // COL_GROUPS <= tq
    n_blk = kc.shape[2]
    nkt = t // tk
    d = HEAD_DIM
    bg = lambda *tail: (lambda i, j, q: (i, j) + tail)
    kern = functools.partial(_attn_prompt_kernel, tq=tq, tk=tk)
    return pl.pallas_call(
        kern,
        grid=(b, g, nq),
        in_specs=[
            pl.BlockSpec((1, 1, 1, ka, wide), lambda i, j, q: (i, j, q, 0, 0)),
            pl.BlockSpec((1, 1, n_blk, d), bg(0, 0)),
            pl.BlockSpec((1, 1, d, n_blk), bg(0, 0)),
            pl.BlockSpec((1, 1, t, ka), bg(0, 0)),
            pl.BlockSpec((1, 1, nkt, V_AUG, tk), bg(0, 0, 0)),
            pl.BlockSpec((1, 1, t, ka), bg(0, 0)),
            pl.BlockSpec((1, 1, nkt, V_AUG, tk), bg(0, 0, 0)),
            pl.BlockSpec((1, 1, gates.shape[2], tq), lambda i, j, q: (i, q, 0, 0)),
        ],
        out_specs=pl.BlockSpec((tq, Q_PER_KV * d), lambda i, j, q: (i * nq + q, j)),
        out_shape=jax.ShapeDtypeStruct((b * t, ATTN_WIDTH), BF16),
        scratch_shapes=[
            pltpu.VMEM((n_blk, wide), F32),
            pltpu.VMEM((1, wide), F32),
            pltpu.VMEM((V_AUG, wide), F32),
            pltpu.VMEM((d, wide), F32),
        ],
        compiler_params=_cparams(("parallel", "parallel", "parallel")),
        name="attn_prompt",
    )(qt, kc, vct, kas, vst, kaw, vwt, gates)


def _attn_sample_kernel(pt_ref, *refs, pages_per_step, past_len, n_blk_valid):
    page_refs = refs[:pages_per_step]
    (qbd_ref, kc_ref, vc_ref, new_s_ref, win_ref, new_w_ref, gt_ref, pos_ref,
     o_ref, selb_ref, m_ref, l_ref, acc_ref, out_ref) = refs[pages_per_step:]
    step = pl.program_id(1)
    n_steps = pl.num_programs(1)
    ncol = qbd_ref.shape[2]
    kvw = qbd_ref.shape[1]
    qbd = qbd_ref[0]
    pos = pos_ref[...]
    group_cols = ncol // Q_PER_KV

    def tile_rows(rows, bias):
        k, v = rows[:, :kvw].astype(BF16), rows[:, kvw:].astype(BF16)
        s = jnp.dot(k, qbd, preferred_element_type=F32)
        v_dot = lambda p: lax.dot_general(v, p, TN_DIMS, preferred_element_type=F32)
        _flash_update(s, bias, v_dot, m_ref, l_ref, acc_ref)

    def tile_chan(page, bias):
        kt, vt = page[:kvw].astype(BF16), page[kvw:].astype(BF16)
        s = lax.dot_general(kt, qbd, TN_DIMS, preferred_element_type=F32)
        v_dot = lambda p: jnp.dot(vt, p, preferred_element_type=F32)
        _flash_update(s, bias, v_dot, m_ref, l_ref, acc_ref)

    def window_bias(n_keys, p0):
        delta = pos - (p0 + lax.broadcasted_iota(jnp.int32, (n_keys, ncol), 0))
        return jnp.where((delta >= 0) & (delta < WINDOW), 0.0, NEG)

    @pl.when(step == 0)
    def _():
        n_blk = kc_ref.shape[1]
        blk = lax.broadcasted_iota(jnp.int32, (n_blk, ncol), 0)
        cmask = (blk * BLOCK + (BLOCK - 1) <= pos) & (blk < n_blk_valid)
        s = jnp.dot(kc_ref[0], qbd, preferred_element_type=F32)
        p = _masked_softmax0(s, cmask)
        o_c = lax.dot_general(vc_ref[0], p.astype(BF16), TN_DIMS, preferred_element_type=F32)
        out_ref[...] = gt_ref[0, 0:1, :] * o_c
        imp = p
        for r in range(1, Q_PER_KV):
            imp = imp + pltpu.roll(p, r * group_cols, 1)
        selb_ref[...] = _sel_to_bias(_select_blocks(imp, blk, pos >> BLOCK_SHIFT, n_blk_valid))

        _flash_init(m_ref, l_ref, acc_ref)
        n_win = win_ref.shape[2]
        tile_rows(new_w_ref[0], window_bias(new_w_ref.shape[1], past_len))
        tile_chan(win_ref[0], window_bias(n_win, past_len - n_win))
        o_w = acc_ref[...] / jnp.maximum(l_ref[...], 1e-30)
        out_ref[...] = out_ref[...] + gt_ref[0, 2:3, :] * o_w

        _flash_init(m_ref, l_ref, acc_ref)
        n_new = new_s_ref.shape[1]
        kpos = past_len + lax.broadcasted_iota(jnp.int32, (n_new, ncol), 0)
        chosen = jnp.broadcast_to(selb_ref[pl.ds(past_len // BLOCK, 1), :], (n_new, ncol))
        tile_rows(new_s_ref[0], jnp.where(kpos <= pos, chosen, NEG))

    page_rows = page_refs[0].shape[2]
    step_keys = pages_per_step * page_rows
    step_blocks = step_keys // BLOCK
    keys = jnp.concatenate([page_refs[j][0] for j in range(pages_per_step)], axis=1)
    chosen = jnp.concatenate(
        [jnp.broadcast_to(selb_ref[pl.ds(step * step_blocks + i, 1), :], (BLOCK, ncol)) for i in range(step_blocks)],
        axis=0)
    tile_chan(keys, chosen)

    @pl.when(step == n_steps - 1)
    def _():
        o_s = acc_ref[...] / jnp.maximum(l_ref[...], 1e-30)
        o_ref[0] = out_ref[...] + gt_ref[0, 1:2, :] * o_s


def _attn_sample(pool_t, page0, page_table, qbd, kc, vc, new_s, win_t, win0, new_w, gt, pos,
                 pages_per_step=PAGES_PER_STEP):
    _, c, page_rows = pool_t.shape
    b, n_pages = page_table.shape
    pages_per_step = min(pages_per_step, n_pages)
    _, kvw, ncol = qbd.shape
    n_blk = kc.shape[1]
    past_len = n_pages * page_rows
    per_b = lambda shape: pl.BlockSpec((1,) + shape, lambda i, s, pt: (i,) + (0,) * len(shape))
    grid_spec = pltpu.PrefetchScalarGridSpec(
        num_scalar_prefetch=1,
        grid=(b, n_pages // pages_per_step),
        in_specs=_page_specs(c, page_rows, pages_per_step, page0) + [
            per_b((kvw, ncol)),
            per_b((n_blk, kvw)),
            per_b((n_blk, kvw)),
            per_b(new_s.shape[1:]),
            pl.BlockSpec((1,) + win_t.shape[1:], lambda i, s, pt: (win0 + i, 0, 0)),
            per_b(new_w.shape[1:]),
            per_b((N_BRANCH, ncol)),
            pl.BlockSpec((1, ncol), lambda i, s, pt: (0, 0)),
        ],
        out_specs=per_b((kvw, ncol)),
        scratch_shapes=[
            pltpu.VMEM((n_blk, ncol), F32),
            pltpu.VMEM((1, ncol), F32),
            pltpu.VMEM((1, ncol), F32),
            pltpu.VMEM((kvw, ncol), F32),
            pltpu.VMEM((kvw, ncol), F32),
        ],
    )
    kern = functools.partial(_attn_sample_kernel, pages_per_step=pages_per_step, past_len=past_len,
                             n_blk_valid=past_len // BLOCK + 1)
    return pl.pallas_call(
        kern,
        grid_spec=grid_spec,
        out_shape=jax.ShapeDtypeStruct((b, kvw, ncol), F32),
        compiler_params=_cparams(("parallel", "arbitrary")),
        name="attn_sample",
    )(page_table, *([pool_t] * pages_per_step), qbd, kc, vc, new_s, win_t, new_w, gt, pos)


def _outproj_kernel(a_ref, c_ref, x_ref, w_ref, g_ref, b_ref, h_ref, *, alpha):
    aw = a_ref.shape[1]
    mix = jnp.dot(a_ref[...], w_ref[:aw, :], preferred_element_type=F32)
    mix = mix + jnp.dot(c_ref[...], w_ref[aw:, :], preferred_element_type=F32)
    h_ref[...] = _layer_norm(alpha * x_ref[...] + mix, g_ref[...], b_ref[...])


def _outproj_ln(attn, conv, x, w_out, layer, g, b, alpha, tm):
    n, d = x.shape
    row = lambda w: pl.BlockSpec((tm, w), lambda i: (i, 0))
    const = lambda shape: pl.BlockSpec(shape, lambda i: (0, 0))
    w_spec = pl.BlockSpec((None,) + w_out.shape[1:], lambda i: (layer, 0, 0))
    return pl.pallas_call(
        functools.partial(_outproj_kernel, alpha=alpha),
        grid=(n // tm,),
        in_specs=[row(attn.shape[1]), row(conv.shape[1]), row(d), w_spec, const((1, d)), const((1, d))],
        out_specs=row(d),
        out_shape=jax.ShapeDtypeStruct((n, d), F32),
        compiler_params=_cparams(("parallel",)),
        name="outproj_ln",
    )(attn, conv, x, w_out, g, b)


def _mlp_kernel(h_ref, w1_ref, w2_ref, g_ref, b_ref, y_ref, acc_ref, hb_ref, *, alpha):
    f = pl.program_id(1)

    @pl.when(f == 0)
    def _():
        hb_ref[...] = h_ref[...].astype(BF16)
        acc_ref[...] = jnp.zeros(acc_ref.shape, F32)

    a = jnp.dot(hb_ref[...], w1_ref[...], preferred_element_type=F32)
    a = jnp.square(jnp.maximum(a, 0.0)).astype(BF16)
    acc_ref[...] += jnp.dot(a, w2_ref[...], preferred_element_type=F32)

    @pl.when(f == pl.num_programs(1) - 1)
    def _():
        y_ref[...] = _layer_norm(alpha * h_ref[...] + acc_ref[...], g_ref[...], b_ref[...])


def _mlp_ln(h, w1, w2, layer, g, b, alpha, tm, tf):
    n, d = h.shape
    dff = w1.shape[2]
    return pl.pallas_call(
        functools.partial(_mlp_kernel, alpha=alpha),
        grid=(n // tm, dff // tf),
        in_specs=[
            pl.BlockSpec((tm, d), lambda i, f: (i, 0)),
            pl.BlockSpec((None, d, tf), lambda i, f: (layer, 0, f)),
            pl.BlockSpec((None, tf, d), lambda i, f: (layer, f, 0)),
            pl.BlockSpec((1, d), lambda i, f: (0, 0)),
            pl.BlockSpec((1, d), lambda i, f: (0, 0)),
        ],
        out_specs=pl.BlockSpec((tm, d), lambda i, f: (i, 0)),
        out_shape=jax.ShapeDtypeStruct((n, d), F32),
        scratch_shapes=[pltpu.VMEM((tm, d), F32), pltpu.VMEM((tm, d), BF16)],
        compiler_params=_cparams(("parallel", "arbitrary")),
        name="mlp_ln",
    )(h, w1, w2, g, b)


def _split_kv(kv, b, t):
    return kv.reshape(b, t, 2, KV_HEADS, HEAD_DIM)


def _prompt_layer(x, p, b, t, alpha, tm, tq, tk):
    w_c = jnp.tile(p["w_cmp"].T, (1, tm // BLOCK))
    kvt_c, kvt_s, kvt_w, vst, vwt, kas, kaw, qt, gates, cmp = _proj_prompt(
        x, p["w_in_t"], p["layer"], w_c, b, t, tm, tq, tk)
    conv, zlast = _conv_prompt(x, p["w_in_t"], p["layer"], p["conv_w"], t, tm)
    n_blk = t // BLOCK
    cmp = cmp.reshape(b, n_blk, 2, KV_HEADS, HEAD_DIM).astype(BF16)
    kc = cmp[:, :, 0].transpose(0, 2, 1, 3)
    vct = cmp[:, :, 1].transpose(0, 2, 3, 1)
    attn = _attn_prompt(qt, kc, vct, kas, vst, kaw, vwt, gates, tq, tk)
    h = _outproj_ln(attn, conv, x, p["w_out"], p["layer"], p["ln1_g"], p["ln1_b"], alpha, tm)
    y = _mlp_ln(h, p["w_mlp1"], p["w_mlp2"], p["layer"], p["ln2_g"], p["ln2_b"], alpha, min(MLP_ROWS, b * t), MLP_COLS)
    conv_state = zlast.reshape(b, t // tm, SUBLANES, -1)[:, -1, SUBLANES - (CONV_W - 1):]
    keep = min(WINDOW, t)
    rows_major = lambda kvt: kvt.reshape(b, 2, KV_HEADS, HEAD_DIM, t).transpose(0, 4, 1, 2, 3)
    return (y, rows_major(kvt_c), rows_major(kvt_s), rows_major(kvt_w)[:, t - keep:], conv_state)


def _sample_layer(x, p, pool_c, pool_s, page0, win_t, win0, cache_win, state_conv, page_table, db, dt, alpha):
    n = db * dt
    page_rows = pool_c.shape[2]
    n_pages = page_table.shape[1]
    past_len = n_pages * page_rows
    c = 2 * KV_WIDTH
    q, kv_c, kv_s, kv_w, gates = _proj(x, p["w_in_t"], p["layer"], n)

    zp = jnp.pad(state_conv, ((0, 0), (SUBLANES - (CONV_W - 1), 0), (0, 0))).reshape(n, -1)
    conv, z = _conv_sample(x, zp, p["w_in_t"], p["layer"], p["conv_w"])
    conv_state = jnp.concatenate([state_conv, z.reshape(db, dt, -1)], axis=1)[:, -(CONV_W - 1):]

    cmp_past = _cmp_pages(pool_c, page0, page_table, p["w_cmp_t"])
    tail = jnp.pad(kv_c.reshape(db, dt, c), ((0, 0), (0, SUBLANES * BLOCK - dt), (0, 0)))
    cmp_tail = _cmp_rows(tail, p["w_cmp"], SUBLANES)
    cmp_all = jnp.concatenate([cmp_past, cmp_tail], axis=1).astype(BF16)
    kc, vc = cmp_all[..., :KV_WIDTH], cmp_all[..., KV_WIDTH:]

    q5 = q.reshape(db, dt, KV_HEADS, Q_PER_KV, HEAD_DIM).transpose(0, 2, 4, 3, 1)
    eye = jnp.eye(KV_HEADS, dtype=q.dtype)
    qbd = (q5[:, :, :, :, None, :] * eye[None, :, None, None, :, None]).reshape(db, KV_WIDTH, N_HEADS * dt)
    gt = gates[:, :N_HEADS * N_BRANCH].reshape(db, dt, KV_HEADS, Q_PER_KV, N_BRANCH)
    gt = gt.transpose(0, 4, 3, 2, 1).reshape(db, N_BRANCH, N_HEADS * dt)
    pos = jnp.tile(past_len + jnp.arange(dt, dtype=jnp.int32), N_HEADS).reshape(1, N_HEADS * dt)
    new_rows = lambda kv: jnp.pad(kv.reshape(db, dt, c), ((0, 0), (0, 2 * SUBLANES - dt), (0, 0)))
    o = _attn_sample(pool_s, page0, page_table, qbd, kc, vc, new_rows(kv_s), win_t, win0, new_rows(kv_w), gt, pos)
    o = o.reshape(db, KV_HEADS, HEAD_DIM, Q_PER_KV, KV_HEADS, dt)
    o = jnp.stack([o[:, g, :, :, g, :] for g in range(KV_HEADS)], axis=1)
    attn = o.transpose(0, 4, 1, 3, 2).reshape(n, ATTN_WIDTH).astype(BF16)

    h = _outproj_ln(attn, conv, x, p["w_out"], p["layer"], p["ln1_g"], p["ln1_b"], alpha, n)
    y = _mlp_ln(h, p["w_mlp1"], p["w_mlp2"], p["layer"], p["ln2_g"], p["ln2_b"], alpha, n, 1024)
    win_keep = cache_win.shape[1]
    new_win = jnp.concatenate([cache_win, _split_kv(kv_w, db, dt)], axis=1)[:, -win_keep:]
    return y, _split_kv(kv_c, db, dt), _split_kv(kv_s, db, dt), new_win, conv_state


def kernel(x_prompt, x_sample, cache_cmp, cache_slc, cache_win, state_conv, page_table, w_in, w_cmp_k, w_cmp_v,
           conv_w, w_out, ln1_g, ln1_b, w_mlp1, w_mlp2, ln2_g, ln2_b):
    depth = w_in.shape[0]
    b, t, d = x_prompt.shape
    db, dt, _ = x_sample.shape
    alpha = (2.0 * depth) ** 0.25
    tm = min(512, t)
    tq, tk = min(512, t), min(512, t)

    n_pool, page_rows = cache_cmp.shape[1:3]
    chan_major = lambda c: c.transpose(0, 1, 3, 4, 5, 2).reshape(c.shape[0] * c.shape[1], -1, c.shape[2])
    pool_c, pool_s, win_t = chan_major(cache_cmp), chan_major(cache_slc), chan_major(cache_win)

    w_out_b, w_mlp1_b, w_mlp2_b = w_out.astype(BF16), w_mlp1.astype(BF16), w_mlp2.astype(BF16)
    yp = x_prompt.reshape(b * t, d)
    ys = x_sample.reshape(db * dt, d)
    outs = [[] for _ in range(8)]
    w_in_t = w_in.transpose(0, 2, 1).astype(BF16)
    for l in range(depth):
        w_cmp = jnp.concatenate([jnp.tile(w_cmp_k[l], (1, KV_HEADS)), jnp.tile(w_cmp_v[l], (1, KV_HEADS))], axis=1)
        p = {
            "w_in_t": w_in_t,
            "w_cmp": w_cmp,
            "w_cmp_t": jnp.tile(w_cmp.T, (1, page_rows // BLOCK)),
            "conv_w": conv_w[l],
            "layer": l,
            "w_out": w_out_b,
            "ln1_g": ln1_g[l].reshape(1, d), "ln1_b": ln1_b[l].reshape(1, d),
            "w_mlp1": w_mlp1_b, "w_mlp2": w_mlp2_b,
            "ln2_g": ln2_g[l].reshape(1, d), "ln2_b": ln2_b[l].reshape(1, d),
        }
        yp, c1, s1, w1, v1 = _prompt_layer(yp, p, b, t, alpha, tm, tq, tk)
        ys, c2, s2, w2, v2 = _sample_layer(ys, p, pool_c, pool_s, l * n_pool, win_t, l * db, cache_win[l],
                                           state_conv[l], page_table, db, dt, alpha)
        for lst, v in zip(outs, (c1, s1, w1, v1, c2, s2, w2, v2)):
            lst.append(v)
    stacked = [jnp.stack(o) for o in outs]
    return (yp.reshape(b, t, d), ys.reshape(db, dt, d), *stacked)
```

```python
import functools
import math

import jax
import jax.numpy as jnp
from jax import lax
from jax.experimental import pallas as pl
from jax.experimental.pallas import tpu as pltpu

HEAD_DIM = 64
KV_HEADS = 4
Q_PER_KV = 4
N_HEADS = KV_HEADS * Q_PER_KV
ATTN_WIDTH = N_HEADS * HEAD_DIM
KV_WIDTH = KV_HEADS * HEAD_DIM
N_BRANCH = 3
BLOCK = 64
BLOCK_SHIFT = BLOCK.bit_length() - 1
N_SEL = 16
WINDOW = 512
CONV_W = 3
LN_EPS = 1e-5
NEG = -1e30
FORCED_SCORE = 1e9
Q_SCALE = HEAD_DIM ** -0.5 * math.log2(math.e)
GATE_PAD = 128

SUBLANES = 8
VMEM_LIMIT = 56 * 1024 * 1024

BF16 = jnp.bfloat16
F32 = jnp.float32
NT_DIMS = (((1,), (1,)), ((), ()))
TN_DIMS = (((0,), (0,)), ((), ()))

ROW_Q = 0
ROW_KV = ATTN_WIDTH
ROW_GATE = ROW_KV + 6 * KV_WIDTH
ROW_CONV = ROW_GATE + N_HEADS * N_BRANCH


def _w_in_spec(w_all, layer):
    return pl.BlockSpec((None,) + w_all.shape[1:], lambda *_: (layer, 0, 0), pipeline_mode=pl.Buffered(1))


def _cparams(sem):
    return pltpu.CompilerParams(dimension_semantics=sem, vmem_limit_bytes=VMEM_LIMIT)


def _layer_norm(v, g, b):
    mu = jnp.mean(v, axis=-1, keepdims=True)
    d = v - mu
    var = jnp.mean(d * d, axis=-1, keepdims=True)
    return d * lax.rsqrt(var + LN_EPS) * g + b


def _proj_kernel(x_ref, w_ref, q_ref, kc_ref, ks_ref, kw_ref, g_ref):
    x = x_ref[...].astype(BF16)

    def mm(r0, r1):
        return lax.dot_general(x, w_ref[r0:r1, :], NT_DIMS, preferred_element_type=F32)

    kv2 = 2 * KV_WIDTH
    q_ref[...] = (mm(ROW_Q, ROW_KV) * Q_SCALE).astype(BF16)
    kc_ref[...] = mm(ROW_KV, ROW_KV + kv2)
    ks_ref[...] = mm(ROW_KV + kv2, ROW_KV + 2 * kv2)
    kw_ref[...] = mm(ROW_KV + 2 * kv2, ROW_KV + 3 * kv2)
    logits = mm(ROW_GATE, ROW_GATE + GATE_PAD)
    g_ref[...] = 1.0 / (1.0 + jnp.exp(-logits))


def _proj(x, w_all, layer, tm):
    n, d = x.shape
    row = lambda i: (i, 0)
    return pl.pallas_call(
        _proj_kernel,
        grid=(n // tm,),
        in_specs=[pl.BlockSpec((tm, d), row), _w_in_spec(w_all, layer)],
        out_specs=[
            pl.BlockSpec((tm, ATTN_WIDTH), row),
            pl.BlockSpec((tm, 2 * KV_WIDTH), row),
            pl.BlockSpec((tm, 2 * KV_WIDTH), row),
            pl.BlockSpec((tm, 2 * KV_WIDTH), row),
            pl.BlockSpec((tm, GATE_PAD), row),
        ],
        out_shape=[
            jax.ShapeDtypeStruct((n, ATTN_WIDTH), BF16),
            jax.ShapeDtypeStruct((n, 2 * KV_WIDTH), F32),
            jax.ShapeDtypeStruct((n, 2 * KV_WIDTH), F32),
            jax.ShapeDtypeStruct((n, 2 * KV_WIDTH), F32),
            jax.ShapeDtypeStruct((n, GATE_PAD), F32),
        ],
        compiler_params=_cparams(("parallel",)),
        name="proj",
    )(x, w_all)


K_AUG = 2 * HEAD_DIM
V_AUG = HEAD_DIM + 16
MLP_ROWS, MLP_COLS = 512, 1024
PAGES_PER_STEP = 32
COL_GROUPS = 8


def _block_sums_t(kvt, w_t, n_out=None, blk0=0):
    rows = kvt.shape[1]
    nb = rows // BLOCK if n_out is None else n_out
    pw = (kvt * w_t).astype(BF16)
    member = jnp.where(lax.broadcasted_iota(jnp.int32, (nb, rows), 0)
                       == blk0 + (lax.broadcasted_iota(jnp.int32, (nb, rows), 1) >> BLOCK_SHIFT),
                       1.0, 0.0).astype(BF16)
    return lax.dot_general(member, pw, NT_DIMS, preferred_element_type=F32)


def _proj_prompt_kernel(x_ref, wt_ref, wc_ref, kvc_ref, kvs_ref, kvw_ref, vts_ref, vtw_ref,
                        kas_ref, kaw_ref, qt_ref, g_ref, cmp_ref, *, tq, tk):
    tm = x_ref.shape[0]
    x = x_ref[...].astype(BF16)

    def nt(r0, r1):
        return lax.dot_general(wt_ref[r0:r1, :], x, NT_DIMS, preferred_element_type=F32)

    c = ROW_KV
    kv2 = 2 * KV_WIDTH
    qall = (nt(ROW_Q, ROW_KV) * Q_SCALE).astype(BF16)
    pad_rows = jnp.zeros((K_AUG - HEAD_DIM, Q_PER_KV * tq), BF16)
    for g in range(KV_HEADS):
        for j in range(tm // tq):
            heads = [qall[(g * Q_PER_KV + r) * HEAD_DIM:(g * Q_PER_KV + r + 1) * HEAD_DIM, j * tq:(j + 1) * tq]
                     for r in range(Q_PER_KV)]
            qt_ref[0, g, j, 0:HEAD_DIM, :] = jnp.concatenate(heads, axis=1)
            qt_ref[0, g, j, HEAD_DIM:K_AUG, :] = pad_rows

    kvt_c = nt(c, c + kv2)
    kvc_ref[0] = kvt_c
    cmp_ref[0] = _block_sums_t(kvt_c, wc_ref[...])

    hot_row = lax.broadcasted_iota(jnp.int32, (K_AUG - HEAD_DIM, tm), 0)
    row_blk = (lax.broadcasted_iota(jnp.int32, (K_AUG - HEAD_DIM, tm), 1) >> BLOCK_SHIFT) & (tk // BLOCK - 1)
    one_hot_t = jnp.where(hot_row == row_blk, 1.0, 0.0)
    sum_rows = jnp.where(lax.broadcasted_iota(jnp.int32, (V_AUG - HEAD_DIM, tk), 0) == 0, 1.0, 0.0).astype(BF16)
    for i, (kv_ref, vt_ref, ka_ref) in enumerate(((kvs_ref, vts_ref, kas_ref), (kvw_ref, vtw_ref, kaw_ref))):
        kvt = nt(c + (i + 1) * kv2, c + (i + 2) * kv2)
        kv_ref[0] = kvt
        for g in range(KV_HEADS):
            v_rows = kvt[KV_WIDTH + g * HEAD_DIM:KV_WIDTH + (g + 1) * HEAD_DIM]
            for jk in range(tm // tk):
                vt_ref[0, g, jk, 0:HEAD_DIM, :] = v_rows[:, jk * tk:(jk + 1) * tk].astype(BF16)
                vt_ref[0, g, jk, HEAD_DIM:V_AUG, :] = sum_rows
            k_t = kvt[g * HEAD_DIM:(g + 1) * HEAD_DIM]
            ka_ref[0, g] = jnp.concatenate([k_t, one_hot_t], axis=0).T.astype(BF16)

    logits = nt(ROW_GATE, ROW_GATE + g_ref.shape[2])
    gates = 1.0 / (1.0 + jnp.exp(-logits))
    for j in range(tm // tq):
        g_ref[0, j] = gates[:, j * tq:(j + 1) * tq]


def _proj_prompt(x, w_all, layer, w_c, b, t, tm, tq, tk):
    n, d = x.shape
    tiles = t // tm
    g_rows = 64
    const = lambda a: pl.BlockSpec(a.shape, lambda i, j: (0,) * a.ndim, pipeline_mode=pl.Buffered(1))
    kvt_spec = pl.BlockSpec((1, 2 * KV_WIDTH, tm), lambda i, j: (i, 0, j))
    vt_spec = pl.BlockSpec((1, KV_HEADS, tm // tk, V_AUG, tk), lambda i, j: (i, 0, j, 0, 0))
    ka_spec = pl.BlockSpec((1, KV_HEADS, tm, K_AUG), lambda i, j: (i, 0, j, 0))
    kvt_shape = jax.ShapeDtypeStruct((b, 2 * KV_WIDTH, t), F32)
    vt_shape = jax.ShapeDtypeStruct((b, KV_HEADS, t // tk, V_AUG, tk), BF16)
    ka_shape = jax.ShapeDtypeStruct((b, KV_HEADS, t, K_AUG), BF16)
    wide = Q_PER_KV * tq
    return pl.pallas_call(
        functools.partial(_proj_prompt_kernel, tq=tq, tk=tk),
        grid=(b, tiles),
        in_specs=[pl.BlockSpec((tm, d), lambda i, j: (i * tiles + j, 0)), _w_in_spec(w_all, layer), const(w_c)],
        out_specs=[
            kvt_spec, kvt_spec, kvt_spec, vt_spec, vt_spec, ka_spec, ka_spec,
            pl.BlockSpec((1, KV_HEADS, tm // tq, K_AUG, wide), lambda i, j: (i, 0, j, 0, 0)),
            pl.BlockSpec((1, tm // tq, g_rows, tq), lambda i, j: (i, j, 0, 0)),
            pl.BlockSpec((1, tm // BLOCK, 2 * KV_WIDTH), lambda i, j: (i, j, 0)),
        ],
        out_shape=[
            kvt_shape, kvt_shape, kvt_shape, vt_shape, vt_shape, ka_shape, ka_shape,
            jax.ShapeDtypeStruct((b, KV_HEADS, t // tq, K_AUG, wide), BF16),
            jax.ShapeDtypeStruct((b, t // tq, g_rows, tq), F32),
            jax.ShapeDtypeStruct((b, t // BLOCK, 2 * KV_WIDTH), F32),
        ],
        compiler_params=_cparams(("parallel", "parallel")),
        name="proj_prompt",
    )(x, w_all, w_c)


def _conv_from_z(z, zp, bgate, cw_ref, c0, c1):
    rows = z.shape[0]
    sub = lax.broadcasted_iota(jnp.int32, z.shape, 0) & (SUBLANES - 1)
    z1 = jnp.where(sub >= 1, pltpu.roll(z, 1, 0), pltpu.roll(zp, rows - (SUBLANES - 1), 0))
    z2 = jnp.where(sub >= 2, pltpu.roll(z, 2, 0), pltpu.roll(zp, rows - (SUBLANES - 2), 0))
    w0 = cw_ref[0:1, c0:c1]
    w1 = cw_ref[1:2, c0:c1]
    w2 = cw_ref[2:3, c0:c1]
    return bgate * (w2 * z + w0 * z2 + w1 * z1)


def _conv_prompt_kernel(x_ref, w_ref, cw_ref, y_ref, zl_ref, carry_ref, *, chunk):
    cdim = y_ref.shape[1]
    x = x_ref[...].astype(BF16)

    @pl.when(pl.program_id(1) == 0)
    def _():
        carry_ref[...] = jnp.zeros(carry_ref.shape, F32)

    for c0 in range(0, cdim, chunk):
        c1 = c0 + chunk

        def mm(off):
            rows = w_ref[ROW_CONV + off + c0:ROW_CONV + off + c1, :]
            return lax.dot_general(x, rows, NT_DIMS, preferred_element_type=F32)

        z = mm(2 * cdim) * mm(0)
        bgate = mm(cdim)
        zp = jnp.concatenate([carry_ref[:, c0:c1], z[:-SUBLANES]], axis=0)
        y_ref[:, c0:c1] = _conv_from_z(z, zp, bgate, cw_ref, c0, c1).astype(BF16)
        carry_ref[:, c0:c1] = z[-SUBLANES:]
        zl_ref[:, c0:c1] = z[-SUBLANES:]


def _conv_prompt(x, w_all, layer, cw, seq, tm, chunk=512):
    n, d = x.shape
    cdim = cw.shape[1]
    tiles = seq // tm
    return pl.pallas_call(
        functools.partial(_conv_prompt_kernel, chunk=chunk),
        grid=(n // seq, tiles),
        in_specs=[
            pl.BlockSpec((tm, d), lambda i, j: (i * tiles + j, 0)),
            _w_in_spec(w_all, layer),
            pl.BlockSpec((CONV_W, cdim), lambda i, j: (0, 0)),
        ],
        out_specs=[
            pl.BlockSpec((tm, cdim), lambda i, j: (i * tiles + j, 0)),
            pl.BlockSpec((SUBLANES, cdim), lambda i, j: (i * tiles + j, 0)),
        ],
        out_shape=[
            jax.ShapeDtypeStruct((n, cdim), BF16),
            jax.ShapeDtypeStruct((n // tm * SUBLANES, cdim), F32),
        ],
        scratch_shapes=[pltpu.VMEM((SUBLANES, cdim), F32)],
        compiler_params=_cparams(("parallel", "arbitrary")),
        name="conv_prompt",
    )(x, w_all, cw)


def _conv_sample_kernel(x_ref, zp_ref, w_ref, cw_ref, y_ref, z_ref, *, chunk):
    cdim = y_ref.shape[1]
    x = x_ref[...].astype(BF16)
    for c0 in range(0, cdim, chunk):
        c1 = c0 + chunk

        def mm(off):
            rows = w_ref[ROW_CONV + off + c0:ROW_CONV + off + c1, :]
            return lax.dot_general(x, rows, NT_DIMS, preferred_element_type=F32)

        z = mm(2 * cdim) * mm(0)
        bgate = mm(cdim)
        y_ref[:, c0:c1] = _conv_from_z(z, zp_ref[:, c0:c1], bgate, cw_ref, c0, c1).astype(BF16)
        z_ref[:, c0:c1] = z


def _conv_sample(x, zp, w_all, layer, cw, chunk=512):
    n, d = x.shape
    cdim = cw.shape[1]
    full = lambda shape: pl.BlockSpec(shape, lambda i: (0, 0))
    return pl.pallas_call(
        functools.partial(_conv_sample_kernel, chunk=chunk),
        grid=(1,),
        in_specs=[full((n, d)), full((n, cdim)), _w_in_spec(w_all, layer), full((CONV_W, cdim))],
        out_specs=[full((n, cdim)), full((n, cdim))],
        out_shape=[jax.ShapeDtypeStruct((n, cdim), BF16), jax.ShapeDtypeStruct((n, cdim), F32)],
        compiler_params=_cparams(("arbitrary",)),
        name="conv_sample",
    )(x, zp, w_all, cw)


def _cmp_rows_kernel(kv_ref, w_ref, o_ref):
    nb = o_ref.shape[1]
    x = kv_ref[0].reshape(nb, BLOCK, kv_ref.shape[2])
    o_ref[0] = jnp.sum(x * w_ref[...][None], axis=1)


def _cmp_rows(kv, w_cmp, nb):
    b, l, c = kv.shape
    return pl.pallas_call(
        _cmp_rows_kernel,
        grid=(b, l // (nb * BLOCK)),
        in_specs=[
            pl.BlockSpec((1, nb * BLOCK, c), lambda i, j: (i, j, 0)),
            pl.BlockSpec((BLOCK, c), lambda i, j: (0, 0)),
        ],
        out_specs=pl.BlockSpec((1, nb, c), lambda i, j: (i, j, 0)),
        out_shape=jax.ShapeDtypeStruct((b, l // BLOCK, c), F32),
        compiler_params=_cparams(("parallel", "parallel")),
        name="cmp_rows",
    )(kv, w_cmp)


def _cmp_pages_kernel(pt_ref, *refs, pages_per_step):
    page_refs = refs[:pages_per_step]
    w_ref = refs[pages_per_step]
    o_ref = refs[pages_per_step + 1]
    w = w_ref[...]
    nb = o_ref.shape[1]
    bpp = w.shape[1] // BLOCK
    acc = jnp.zeros(o_ref.shape[1:], F32)
    for j in range(pages_per_step):
        acc = acc + _block_sums_t(page_refs[j][0], w, nb, j * bpp)
    o_ref[0] = acc


def _page_specs(c, page_rows, pages_per_step, page0):
    def spec(j):
        return pl.BlockSpec((1, c, page_rows), lambda b, s, pt: (page0 + pt[b, s * pages_per_step + j], 0, 0))
    return [spec(j) for j in range(pages_per_step)]


def _cmp_pages(pool_t, page0, page_table, w_cmp_t, pages_per_step=PAGES_PER_STEP):
    _, c, page_rows = pool_t.shape
    b, n_pages = page_table.shape
    pages_per_step = min(pages_per_step, n_pages)
    nb = pages_per_step * page_rows // BLOCK
    grid_spec = pltpu.PrefetchScalarGridSpec(
        num_scalar_prefetch=1,
        grid=(b, n_pages // pages_per_step),
        in_specs=_page_specs(c, page_rows, pages_per_step, page0)
        + [pl.BlockSpec((c, page_rows), lambda i, s, pt: (0, 0))],
        out_specs=pl.BlockSpec((1, nb, c), lambda i, s, pt: (i, s, 0)),
    )
    return pl.pallas_call(
        functools.partial(_cmp_pages_kernel, pages_per_step=pages_per_step),
        grid_spec=grid_spec,
        out_shape=jax.ShapeDtypeStruct((b, n_pages * page_rows // BLOCK, c), F32),
        compiler_params=_cparams(("parallel", "parallel")),
        name="cmp_pages",
    )(page_table, *([pool_t] * pages_per_step), w_cmp_t)


def _masked_softmax0(s, mask):
    s = jnp.where(mask, s, NEG)
    m = jnp.max(s, axis=0, keepdims=True)
    e = jnp.where(mask, jnp.exp2(s - m), 0.0)
    return e / jnp.maximum(jnp.sum(e, axis=0, keepdims=True), 1e-30)


def _topk_mask0(score, blk, n_sel):
    big = jnp.int32(2 ** 30)
    sel = jnp.zeros(score.shape, F32)
    for _ in range(n_sel):
        m = jnp.max(score, axis=0, keepdims=True)
        idx = jnp.min(jnp.where(score == m, blk, big), axis=0, keepdims=True)
        pick = blk == idx
        sel = jnp.where(pick, 1.0, sel)
        score = jnp.where(pick, -jnp.inf, score)
    return sel


def _sel_to_bias(sel):
    return (sel - 1.0) * (-NEG)


def _write_selection_bias(write, imp, blk, cur, n_valid_blk):
    n_sel = min(N_SEL, n_valid_blk)
    below = -FORCED_SCORE * (1.0 + blk.astype(F32) * (2.0 ** -10))
    score = jnp.where(blk <= cur, imp, below)
    score = jnp.where(blk == cur - 1, FORCED_SCORE, score)
    score = jnp.where(blk == cur, 2 * FORCED_SCORE, score)
    score = jnp.where(blk == 0, 3 * FORCED_SCORE, score)
    score = jnp.where(blk < n_valid_blk, score, -jnp.inf)

    sel = jnp.zeros(score.shape, F32)
    left = score
    for _ in range(n_sel):
        pick = left == jnp.max(left, axis=0, keepdims=True)
        sel = jnp.where(pick, 1.0, sel)
        left = jnp.where(pick, -jnp.inf, left)
    write(_sel_to_bias(sel))
    miscount = jnp.max(jnp.abs(jnp.sum(sel, axis=0, keepdims=True) - n_sel))

    @pl.when(miscount > 0.0)
    def _():
        write(_sel_to_bias(_topk_mask0(score, blk, n_sel)))


def _flash_update(s, bias, v_dot, m_ref, l_ref, acc_ref):
    if bias is not None:
        s = bias + s
    m_old = m_ref[...]
    m_new = jnp.maximum(m_old, jnp.max(s, axis=0, keepdims=True))
    alpha = jnp.exp2(m_old - m_new)
    p = jnp.exp2(s - m_new)
    if l_ref is not None:
        l_ref[...] = alpha * l_ref[...] + jnp.sum(p, axis=0, keepdims=True)
    acc_ref[...] = alpha * acc_ref[...] + v_dot(p.astype(BF16))
    m_ref[...] = m_new


def _flash_init(m_ref, l_ref, acc_ref):
    m_ref[...] = jnp.full(m_ref.shape, NEG, F32)
    if l_ref is not None:
        l_ref[...] = jnp.zeros(l_ref.shape, F32)
    acc_ref[...] = jnp.zeros(acc_ref.shape, F32)


def _attn_prompt_kernel(qt_ref, kc_ref, vct_ref, ks_ref, vst_ref, kw_ref, vwt_ref, g_ref,
                        o_ref, selb_ref, m_ref, acc_ref, out_ref, *, tq, tk):
    gi = pl.program_id(1)
    qi = pl.program_id(2)
    q0 = qi * tq
    n_blk = kc_ref.shape[2]
    wide = Q_PER_KV * tq
    qt = qt_ref[0, 0, 0]
    q_rows = qt[0:HEAD_DIM]
    pos = q0 + lax.broadcasted_iota(jnp.int32, (1, tq), 1)
    pos_w = q0 + (lax.broadcasted_iota(jnp.int32, (1, wide), 1) & (tq - 1))

    def gate(branch):
        rows = [g_ref[0, 0, pl.ds((gi * Q_PER_KV + r) * N_BRANCH + branch, 1), :] for r in range(Q_PER_KV)]
        return jnp.concatenate(rows, axis=1)

    blk_w = lax.broadcasted_iota(jnp.int32, (n_blk, wide), 0)
    p = _masked_softmax0(jnp.dot(kc_ref[0, 0], q_rows, preferred_element_type=F32),
                         blk_w * BLOCK + (BLOCK - 1) <= pos_w)
    o_c = jnp.dot(vct_ref[0, 0], p.astype(BF16), preferred_element_type=F32)
    out_ref[...] = gate(0) * o_c
    imp = p[:, 0:tq]
    for r in range(1, Q_PER_KV):
        imp = imp + p[:, r * tq:(r + 1) * tq]
    blk = lax.broadcasted_iota(jnp.int32, (n_blk, tq), 0)
    def write_selb(bias):
        for r in range(Q_PER_KV):
            selb_ref[:, r * tq:(r + 1) * tq] = bias

    _write_selection_bias(write_selb, imp, blk, pos >> BLOCK_SHIFT, n_blk)

    kpos_tile = lax.broadcasted_iota(jnp.int32, (tk, tq), 0)
    bpt = tk // BLOCK
    bias_pad = jnp.zeros((K_AUG - HEAD_DIM - bpt, wide), F32)

    cw = wide // COL_GROUPS
    q_lo = [(h * cw) % tq for h in range(COL_GROUPS)]

    groups = [slice(h * cw, (h + 1) * cw) for h in range(COL_GROUPS)]

    def tiles_update(k_ref, vt_ref, tiles):
        staged = []
        for kt, q_op, bias, key_rows in tiles:
            k_tile = k_ref[0, 0, pl.ds(pl.multiple_of(kt * tk, tk), tk), :]
            rows = [(0, tk) if key_rows is None else key_rows(h) for h in range(COL_GROUPS)]
            scores = [jnp.dot(k_tile[r0:r1], q_op[:, cols], preferred_element_type=F32)
                      for (r0, r1), cols in zip(rows, groups)]
            staged.append((vt_ref[0, 0, kt], rows, scores, bias))
        for vt, rows, scores, bias in staged:
            for h, (s, (r0, r1), cols) in enumerate(zip(scores, rows, groups)):
                group_bias = None if bias is None else bias[r0:r1, q_lo[h]:q_lo[h] + cw]
                v_dot = lambda pr, vt=vt, r0=r0, r1=r1: jnp.dot(vt[:, r0:r1], pr, preferred_element_type=F32)
                _flash_update(s, group_bias, v_dot, m_ref.at[:, cols], None, acc_ref.at[:, cols])

    def tile_update(k_ref, vt_ref, kt, q_op, bias, key_rows=None):
        tiles_update(k_ref, vt_ref, [(kt, q_op, bias, key_rows)])

    def finish(branch):
        o = acc_ref[0:HEAD_DIM, :] / jnp.maximum(acc_ref[HEAD_DIM:HEAD_DIM + 1, :], 1e-30)
        out_ref[...] = out_ref[...] + gate(branch) * o

    def q_with_block_bias(kt):
        tile_bias = selb_ref[pl.ds(pl.multiple_of(kt * bpt, bpt), bpt), :]
        return jnp.concatenate([q_rows, jnp.concatenate([tile_bias, bias_pad], axis=0).astype(BF16)], axis=0)

    kt_diag = (q0 + tq - 1) // tk
    _flash_init(m_ref, None, acc_ref)

    def sel_body(kt, carry):
        tile_update(ks_ref, vst_ref, kt, q_with_block_bias(kt), None)
        return carry

    lax.fori_loop(0, kt_diag, sel_body, 0)
    causal = jnp.where(kt_diag * tk + kpos_tile <= pos, 0.0, NEG)
    below_diag = lambda h: (0, q_lo[h] + cw)
    tile_update(ks_ref, vst_ref, kt_diag, q_with_block_bias(kt_diag), causal, below_diag)
    finish(1)

    _flash_init(m_ref, None, acc_ref)

    @pl.when(qi > 0)
    def _():
        older = jnp.where(pos - ((kt_diag - 1) * tk + kpos_tile) < WINDOW, 0.0, NEG)
        tile_update(kw_ref, vwt_ref, kt_diag - 1, qt, older, lambda h: (q_lo[h], tk))

    tile_update(kw_ref, vwt_ref, kt_diag, qt, causal, below_diag)
    finish(2)

    for pair in range(Q_PER_KV // 2):
        two = jnp.concatenate([out_ref[:, (2 * pair + h) * tq:(2 * pair + h + 1) * tq] for h in range(2)], axis=0)
        o_ref[:, pair * 2 * HEAD_DIM:(pair + 1) * 2 * HEAD_DIM] = two.T.astype(o_ref.dtype)


def _attn_prompt(qt, kc, vct, kas, vst, kaw, vwt, gates, tq, tk):
    b, g, nq, ka, wide = qt.shape
    t = nq * tq
    assert tq & (tq - 1) == 0 and wide == Q_PER_KV * tq and ka == K_AUG
    assert tq == tk == WINDOW and 2 * wide // COL_GROUPS <= tq
    n_blk = kc.shape[2]
    nkt = t // tk
    d = HEAD_DIM
    bg = lambda *tail: (lambda i, j, q: (i, j) + tail)
    kern = functools.partial(_attn_prompt_kernel, tq=tq, tk=tk)
    return pl.pallas_call(
        kern,
        grid=(b, g, nq),
        in_specs=[
            pl.BlockSpec((1, 1, 1, ka, wide), lambda i, j, q: (i, j, q, 0, 0)),
            pl.BlockSpec((1, 1, n_blk, d), bg(0, 0)),
            pl.BlockSpec((1, 1, d, n_blk), bg(0, 0)),
            pl.BlockSpec((1, 1, t, ka), bg(0, 0)),
            pl.BlockSpec((1, 1, nkt, V_AUG, tk), bg(0, 0, 0)),
            pl.BlockSpec((1, 1, t, ka), bg(0, 0)),
            pl.BlockSpec((1, 1, nkt, V_AUG, tk), bg(0, 0, 0)),
            pl.BlockSpec((1, 1, gates.shape[2], tq), lambda i, j, q: (i, q, 0, 0)),
        ],
        out_specs=pl.BlockSpec((tq, Q_PER_KV * d), lambda i, j, q: (i * nq + q, j)),
        out_shape=jax.ShapeDtypeStruct((b * t, ATTN_WIDTH), BF16),
        scratch_shapes=[
            pltpu.VMEM((n_blk, wide), F32),
            pltpu.VMEM((1, wide), F32),
            pltpu.VMEM((V_AUG, wide), F32),
            pltpu.VMEM((d, wide), F32),
        ],
        compiler_params=_cparams(("parallel", "parallel", "parallel")),
        name="attn_prompt",
    )(qt, kc, vct, kas, vst, kaw, vwt, gates)


def _attn_sample_kernel(pt_ref, *refs, pages_per_step, past_len, n_blk_valid):
    page_refs = refs[:pages_per_step]
    (qbd_ref, kc_ref, vc_ref, new_s_ref, win_ref, new_w_ref, gt_ref, pos_ref,
     o_ref, selb_ref, m_ref, l_ref, acc_ref, out_ref) = refs[pages_per_step:]
    step = pl.program_id(1)
    n_steps = pl.num_programs(1)
    ncol = qbd_ref.shape[2]
    kvw = qbd_ref.shape[1]
    qbd = qbd_ref[0]
    pos = pos_ref[...]
    group_cols = ncol // Q_PER_KV

    def tile_rows(rows, bias):
        k, v = rows[:, :kvw].astype(BF16), rows[:, kvw:].astype(BF16)
        s = jnp.dot(k, qbd, preferred_element_type=F32)
        v_dot = lambda p: lax.dot_general(v, p, TN_DIMS, preferred_element_type=F32)
        _flash_update(s, bias, v_dot, m_ref, l_ref, acc_ref)

    def tile_chan(page, bias):
        kt, vt = page[:kvw].astype(BF16), page[kvw:].astype(BF16)
        s = lax.dot_general(kt, qbd, TN_DIMS, preferred_element_type=F32)
        v_dot = lambda p: jnp.dot(vt, p, preferred_element_type=F32)
        _flash_update(s, bias, v_dot, m_ref, l_ref, acc_ref)

    def window_bias(n_keys, p0):
        delta = pos - (p0 + lax.broadcasted_iota(jnp.int32, (n_keys, ncol), 0))
        return jnp.where((delta >= 0) & (delta < WINDOW), 0.0, NEG)

    @pl.when(step == 0)
    def _():
        n_blk = kc_ref.shape[1]
        blk = lax.broadcasted_iota(jnp.int32, (n_blk, ncol), 0)
        cmask = (blk * BLOCK + (BLOCK - 1) <= pos) & (blk < n_blk_valid)
        s = jnp.dot(kc_ref[0], qbd, preferred_element_type=F32)
        p = _masked_softmax0(s, cmask)
        o_c = lax.dot_general(vc_ref[0], p.astype(BF16), TN_DIMS, preferred_element_type=F32)
        out_ref[...] = gt_ref[0, 0:1, :] * o_c
        imp = p
        for r in range(1, Q_PER_KV):
            imp = imp + pltpu.roll(p, r * group_cols, 1)
        def write_selb(bias):
            selb_ref[...] = bias

        _write_selection_bias(write_selb, imp, blk, pos >> BLOCK_SHIFT, n_blk_valid)

        _flash_init(m_ref, l_ref, acc_ref)
        n_win = win_ref.shape[2]
        tile_rows(new_w_ref[0], window_bias(new_w_ref.shape[1], past_len))
        tile_chan(win_ref[0], window_bias(n_win, past_len - n_win))
        o_w = acc_ref[...] / jnp.maximum(l_ref[...], 1e-30)
        out_ref[...] = out_ref[...] + gt_ref[0, 2:3, :] * o_w

        _flash_init(m_ref, l_ref, acc_ref)
        n_new = new_s_ref.shape[1]
        kpos = past_len + lax.broadcasted_iota(jnp.int32, (n_new, ncol), 0)
        chosen = jnp.broadcast_to(selb_ref[pl.ds(past_len // BLOCK, 1), :], (n_new, ncol))
        tile_rows(new_s_ref[0], jnp.where(kpos <= pos, chosen, NEG))

    page_rows = page_refs[0].shape[2]
    step_keys = pages_per_step * page_rows
    step_blocks = step_keys // BLOCK
    keys = jnp.concatenate([page_refs[j][0] for j in range(pages_per_step)], axis=1)
    chosen = jnp.concatenate(
        [jnp.broadcast_to(selb_ref[pl.ds(step * step_blocks + i, 1), :], (BLOCK, ncol)) for i in range(step_blocks)],
        axis=0)
    tile_chan(keys, chosen)

    @pl.when(step == n_steps - 1)
    def _():
        o_s = acc_ref[...] / jnp.maximum(l_ref[...], 1e-30)
        o_ref[0] = out_ref[...] + gt_ref[0, 1:2, :] * o_s


def _attn_sample(pool_t, page0, page_table, qbd, kc, vc, new_s, win_t, win0, new_w, gt, pos,
                 pages_per_step=PAGES_PER_STEP):
    _, c, page_rows = pool_t.shape
    b, n_pages = page_table.shape
    pages_per_step = min(pages_per_step, n_pages)
    _, kvw, ncol = qbd.shape
    n_blk = kc.shape[1]
    past_len = n_pages * page_rows
    per_b = lambda shape: pl.BlockSpec((1,) + shape, lambda i, s, pt: (i,) + (0,) * len(shape))
    grid_spec = pltpu.PrefetchScalarGridSpec(
        num_scalar_prefetch=1,
        grid=(b, n_pages // pages_per_step),
        in_specs=_page_specs(c, page_rows, pages_per_step, page0) + [
            per_b((kvw, ncol)),
            per_b((n_blk, kvw)),
            per_b((n_blk, kvw)),
            per_b(new_s.shape[1:]),
            pl.BlockSpec((1,) + win_t.shape[1:], lambda i, s, pt: (win0 + i, 0, 0)),
            per_b(new_w.shape[1:]),
            per_b((N_BRANCH, ncol)),
            pl.BlockSpec((1, ncol), lambda i, s, pt: (0, 0)),
        ],
        out_specs=per_b((kvw, ncol)),
        scratch_shapes=[
            pltpu.VMEM((n_blk, ncol), F32),
            pltpu.VMEM((1, ncol), F32),
            pltpu.VMEM((1, ncol), F32),
            pltpu.VMEM((kvw, ncol), F32),
            pltpu.VMEM((kvw, ncol), F32),
        ],
    )
    kern = functools.partial(_attn_sample_kernel, pages_per_step=pages_per_step, past_len=past_len,
                             n_blk_valid=past_len // BLOCK + 1)
    return pl.pallas_call(
        kern,
        grid_spec=grid_spec,
        out_shape=jax.ShapeDtypeStruct((b, kvw, ncol), F32),
        compiler_params=_cparams(("parallel", "arbitrary")),
        name="attn_sample",
    )(page_table, *([pool_t] * pages_per_step), qbd, kc, vc, new_s, win_t, new_w, gt, pos)


def _outproj_kernel(a_ref, c_ref, x_ref, w_ref, g_ref, b_ref, h_ref, *, alpha):
    aw = a_ref.shape[1]
    mix = jnp.dot(a_ref[...], w_ref[:aw, :], preferred_element_type=F32)
    mix = mix + jnp.dot(c_ref[...], w_ref[aw:, :], preferred_element_type=F32)
    h_ref[...] = _layer_norm(alpha * x_ref[...] + mix, g_ref[...], b_ref[...])


def _outproj_ln(attn, conv, x, w_out, layer, g, b, alpha, tm):
    n, d = x.shape
    row = lambda w: pl.BlockSpec((tm, w), lambda i: (i, 0))
    const = lambda shape: pl.BlockSpec(shape, lambda i: (0, 0))
    w_spec = pl.BlockSpec((None,) + w_out.shape[1:], lambda i: (layer, 0, 0))
    return pl.pallas_call(
        functools.partial(_outproj_kernel, alpha=alpha),
        grid=(n // tm,),
        in_specs=[row(attn.shape[1]), row(conv.shape[1]), row(d), w_spec, const((1, d)), const((1, d))],
        out_specs=row(d),
        out_shape=jax.ShapeDtypeStruct((n, d), F32),
        compiler_params=_cparams(("parallel",)),
        name="outproj_ln",
    )(attn, conv, x, w_out, g, b)


def _mlp_kernel(h_ref, w1_ref, w2_ref, g_ref, b_ref, y_ref, acc_ref, hb_ref, *, alpha):
    f = pl.program_id(1)

    @pl.when(f == 0)
    def _():
        hb_ref[...] = h_ref[...].astype(BF16)
        acc_ref[...] = jnp.zeros(acc_ref.shape, F32)

    a = jnp.dot(hb_ref[...], w1_ref[...], preferred_element_type=F32)
    a = jnp.square(jnp.maximum(a, 0.0)).astype(BF16)
    acc_ref[...] += jnp.dot(a, w2_ref[...], preferred_element_type=F32)

    @pl.when(f == pl.num_programs(1) - 1)
    def _():
        y_ref[...] = _layer_norm(alpha * h_ref[...] + acc_ref[...], g_ref[...], b_ref[...])


def _mlp_ln(h, w1, w2, layer, g, b, alpha, tm, tf):
    n, d = h.shape
    dff = w1.shape[2]
    return pl.pallas_call(
        functools.partial(_mlp_kernel, alpha=alpha),
        grid=(n // tm, dff // tf),
        in_specs=[
            pl.BlockSpec((tm, d), lambda i, f: (i, 0)),
            pl.BlockSpec((None, d, tf), lambda i, f: (layer, 0, f)),
            pl.BlockSpec((None, tf, d), lambda i, f: (layer, f, 0)),
            pl.BlockSpec((1, d), lambda i, f: (0, 0)),
            pl.BlockSpec((1, d), lambda i, f: (0, 0)),
        ],
        out_specs=pl.BlockSpec((tm, d), lambda i, f: (i, 0)),
        out_shape=jax.ShapeDtypeStruct((n, d), F32),
        scratch_shapes=[pltpu.VMEM((tm, d), F32), pltpu.VMEM((tm, d), BF16)],
        compiler_params=_cparams(("parallel", "arbitrary")),
        name="mlp_ln",
    )(h, w1, w2, g, b)


def _split_kv(kv, b, t):
    return kv.reshape(b, t, 2, KV_HEADS, HEAD_DIM)


def _prompt_layer(x, p, b, t, alpha, tm, tq, tk):
    w_c = jnp.tile(p["w_cmp"].T, (1, tm // BLOCK))
    kvt_c, kvt_s, kvt_w, vst, vwt, kas, kaw, qt, gates, cmp = _proj_prompt(
        x, p["w_in_t"], p["layer"], w_c, b, t, tm, tq, tk)
    conv, zlast = _conv_prompt(x, p["w_in_t"], p["layer"], p["conv_w"], t, tm)
    n_blk = t // BLOCK
    cmp = cmp.reshape(b, n_blk, 2, KV_HEADS, HEAD_DIM).astype(BF16)
    kc = cmp[:, :, 0].transpose(0, 2, 1, 3)
    vct = cmp[:, :, 1].transpose(0, 2, 3, 1)
    attn = _attn_prompt(qt, kc, vct, kas, vst, kaw, vwt, gates, tq, tk)
    h = _outproj_ln(attn, conv, x, p["w_out"], p["layer"], p["ln1_g"], p["ln1_b"], alpha, tm)
    y = _mlp_ln(h, p["w_mlp1"], p["w_mlp2"], p["layer"], p["ln2_g"], p["ln2_b"], alpha, min(MLP_ROWS, b * t), MLP_COLS)
    conv_state = zlast.reshape(b, t // tm, SUBLANES, -1)[:, -1, SUBLANES - (CONV_W - 1):]
    keep = min(WINDOW, t)
    rows_major = lambda kvt: kvt.reshape(b, 2, KV_HEADS, HEAD_DIM, t).transpose(0, 4, 1, 2, 3)
    return (y, rows_major(kvt_c), rows_major(kvt_s), rows_major(kvt_w)[:, t - keep:], conv_state)


def _sample_layer(x, p, pool_c, pool_s, page0, win_t, win0, cache_win, state_conv, page_table, db, dt, alpha):
    n = db * dt
    page_rows = pool_c.shape[2]
    n_pages = page_table.shape[1]
    past_len = n_pages * page_rows
    c = 2 * KV_WIDTH
    q, kv_c, kv_s, kv_w, gates = _proj(x, p["w_in_t"], p["layer"], n)

    zp = jnp.pad(state_conv, ((0, 0), (SUBLANES - (CONV_W - 1), 0), (0, 0))).reshape(n, -1)
    conv, z = _conv_sample(x, zp, p["w_in_t"], p["layer"], p["conv_w"])
    conv_state = jnp.concatenate([state_conv, z.reshape(db, dt, -1)], axis=1)[:, -(CONV_W - 1):]

    cmp_past = _cmp_pages(pool_c, page0, page_table, p["w_cmp_t"])
    tail = jnp.pad(kv_c.reshape(db, dt, c), ((0, 0), (0, SUBLANES * BLOCK - dt), (0, 0)))
    cmp_tail = _cmp_rows(tail, p["w_cmp"], SUBLANES)
    cmp_all = jnp.concatenate([cmp_past, cmp_tail], axis=1).astype(BF16)
    kc, vc = cmp_all[..., :KV_WIDTH], cmp_all[..., KV_WIDTH:]

    q5 = q.reshape(db, dt, KV_HEADS, Q_PER_KV, HEAD_DIM).transpose(0, 2, 4, 3, 1)
    eye = jnp.eye(KV_HEADS, dtype=q.dtype)
    qbd = (q5[:, :, :, :, None, :] * eye[None, :, None, None, :, None]).reshape(db, KV_WIDTH, N_HEADS * dt)
    gt = gates[:, :N_HEADS * N_BRANCH].reshape(db, dt, KV_HEADS, Q_PER_KV, N_BRANCH)
    gt = gt.transpose(0, 4, 3, 2, 1).reshape(db, N_BRANCH, N_HEADS * dt)
    pos = jnp.tile(past_len + jnp.arange(dt, dtype=jnp.int32), N_HEADS).reshape(1, N_HEADS * dt)
    new_rows = lambda kv: jnp.pad(kv.reshape(db, dt, c), ((0, 0), (0, 2 * SUBLANES - dt), (0, 0)))
    o = _attn_sample(pool_s, page0, page_table, qbd, kc, vc, new_rows(kv_s), win_t, win0, new_rows(kv_w), gt, pos)
    o = o.reshape(db, KV_HEADS, HEAD_DIM, Q_PER_KV, KV_HEADS, dt)
    o = jnp.stack([o[:, g, :, :, g, :] for g in range(KV_HEADS)], axis=1)
    attn = o.transpose(0, 4, 1, 3, 2).reshape(n, ATTN_WIDTH).astype(BF16)

    h = _outproj_ln(attn, conv, x, p["w_out"], p["layer"], p["ln1_g"], p["ln1_b"], alpha, n)
    y = _mlp_ln(h, p["w_mlp1"], p["w_mlp2"], p["layer"], p["ln2_g"], p["ln2_b"], alpha, n, 1024)
    win_keep = cache_win.shape[1]
    new_win = jnp.concatenate([cache_win, _split_kv(kv_w, db, dt)], axis=1)[:, -win_keep:]
    return y, _split_kv(kv_c, db, dt), _split_kv(kv_s, db, dt), new_win, conv_state


def kernel(x_prompt, x_sample, cache_cmp, cache_slc, cache_win, state_conv, page_table, w_in, w_cmp_k, w_cmp_v,
           conv_w, w_out, ln1_g, ln1_b, w_mlp1, w_mlp2, ln2_g, ln2_b):
    depth = w_in.shape[0]
    b, t, d = x_prompt.shape
    db, dt, _ = x_sample.shape
    alpha = (2.0 * depth) ** 0.25
    tm = min(512, t)
    tq, tk = min(512, t), min(512, t)

    n_pool, page_rows = cache_cmp.shape[1:3]
    chan_major = lambda c: c.transpose(0, 1, 3, 4, 5, 2).reshape(c.shape[0] * c.shape[1], -1, c.shape[2])
    pool_c, pool_s, win_t = chan_major(cache_cmp), chan_major(cache_slc), chan_major(cache_win)

    w_out_b, w_mlp1_b, w_mlp2_b = w_out.astype(BF16), w_mlp1.astype(BF16), w_mlp2.astype(BF16)
    yp = x_prompt.reshape(b * t, d)
    ys = x_sample.reshape(db * dt, d)
    outs = [[] for _ in range(8)]
    w_in_t = w_in.transpose(0, 2, 1).astype(BF16)
    for l in range(depth):
        w_cmp = jnp.concatenate([jnp.tile(w_cmp_k[l], (1, KV_HEADS)), jnp.tile(w_cmp_v[l], (1, KV_HEADS))], axis=1)
        p = {
            "w_in_t": w_in_t,
            "w_cmp": w_cmp,
            "w_cmp_t": jnp.tile(w_cmp.T, (1, page_rows // BLOCK)),
            "conv_w": conv_w[l],
            "layer": l,
            "w_out": w_out_b,
            "ln1_g": ln1_g[l].reshape(1, d), "ln1_b": ln1_b[l].reshape(1, d),
            "w_mlp1": w_mlp1_b, "w_mlp2": w_mlp2_b,
            "ln2_g": ln2_g[l].reshape(1, d), "ln2_b": ln2_b[l].reshape(1, d),
        }
        yp, c1, s1, w1, v1 = _prompt_layer(yp, p, b, t, alpha, tm, tq, tk)
        ys, c2, s2, w2, v2 = _sample_layer(ys, p, pool_c, pool_s, l * n_pool, win_t, l * db, cache_win[l],
                                           state_conv[l], page_table, db, dt, alpha)
        for lst, v in zip(outs, (c1, s1, w1, v1, c2, s2, w2, v2)):
            lst.append(v)
    stacked = [jnp.stack(o) for o in outs]
    return (yp.reshape(b, t, d), ys.reshape(db, dt, d), *stacked)
```

```python
import functools
import math

import jax
import jax.numpy as jnp
from jax import lax
from jax.experimental import pallas as pl
from jax.experimental.pallas import tpu as pltpu

HEAD_DIM = 64
KV_HEADS = 4
Q_PER_KV = 4
N_HEADS = KV_HEADS * Q_PER_KV
ATTN_WIDTH = N_HEADS * HEAD_DIM
KV_WIDTH = KV_HEADS * HEAD_DIM
N_BRANCH = 3
BLOCK = 64
BLOCK_SHIFT = BLOCK.bit_length() - 1
N_SEL = 16
WINDOW = 512
CONV_W = 3
LN_EPS = 1e-5
NEG = -1e30
FORCED_SCORE = 1e9
Q_SCALE = HEAD_DIM ** -0.5 * math.log2(math.e)
GATE_PAD = 128

SUBLANES = 8
VMEM_LIMIT = 56 * 1024 * 1024

BF16 = jnp.bfloat16
F32 = jnp.float32
NT_DIMS = (((1,), (1,)), ((), ()))
TN_DIMS = (((0,), (0,)), ((), ()))

ROW_Q = 0
ROW_KV = ATTN_WIDTH
ROW_GATE = ROW_KV + 6 * KV_WIDTH
ROW_CONV = ROW_GATE + N_HEADS * N_BRANCH


def _w_in_spec(w_all, layer):
    return pl.BlockSpec((None,) + w_all.shape[1:], lambda *_: (layer, 0, 0), pipeline_mode=pl.Buffered(1))


def _cparams(sem):
    return pltpu.CompilerParams(dimension_semantics=sem, vmem_limit_bytes=VMEM_LIMIT)


def _layer_norm(v, g, b):
    mu = jnp.mean(v, axis=-1, keepdims=True)
    d = v - mu
    var = jnp.mean(d * d, axis=-1, keepdims=True)
    return d * lax.rsqrt(var + LN_EPS) * g + b


def _proj_kernel(x_ref, w_ref, q_ref, kc_ref, ks_ref, kw_ref, g_ref):
    x = x_ref[...].astype(BF16)

    def mm(r0, r1):
        return lax.dot_general(x, w_ref[r0:r1, :], NT_DIMS, preferred_element_type=F32)

    kv2 = 2 * KV_WIDTH
    q_ref[...] = (mm(ROW_Q, ROW_KV) * Q_SCALE).astype(BF16)
    kc_ref[...] = mm(ROW_KV, ROW_KV + kv2)
    ks_ref[...] = mm(ROW_KV + kv2, ROW_KV + 2 * kv2)
    kw_ref[...] = mm(ROW_KV + 2 * kv2, ROW_KV + 3 * kv2)
    logits = mm(ROW_GATE, ROW_GATE + GATE_PAD)
    g_ref[...] = 1.0 / (1.0 + jnp.exp(-logits))


def _proj(x, w_all, layer, tm):
    n, d = x.shape
    row = lambda i: (i, 0)
    return pl.pallas_call(
        _proj_kernel,
        grid=(n // tm,),
        in_specs=[pl.BlockSpec((tm, d), row), _w_in_spec(w_all, layer)],
        out_specs=[
            pl.BlockSpec((tm, ATTN_WIDTH), row),
            pl.BlockSpec((tm, 2 * KV_WIDTH), row),
            pl.BlockSpec((tm, 2 * KV_WIDTH), row),
            pl.BlockSpec((tm, 2 * KV_WIDTH), row),
            pl.BlockSpec((tm, GATE_PAD), row),
        ],
        out_shape=[
            jax.ShapeDtypeStruct((n, ATTN_WIDTH), BF16),
            jax.ShapeDtypeStruct((n, 2 * KV_WIDTH), F32),
            jax.ShapeDtypeStruct((n, 2 * KV_WIDTH), F32),
            jax.ShapeDtypeStruct((n, 2 * KV_WIDTH), F32),
            jax.ShapeDtypeStruct((n, GATE_PAD), F32),
        ],
        compiler_params=_cparams(("parallel",)),
        name="proj",
    )(x, w_all)


K_AUG = 2 * HEAD_DIM
V_AUG = HEAD_DIM + 16
MLP_ROWS, MLP_COLS = 512, 1024
PAGES_PER_STEP = 32
COL_GROUPS = 8


def _block_sums_t(kvt, w_t, n_out=None, blk0=0):
    rows = kvt.shape[1]
    nb = rows // BLOCK if n_out is None else n_out
    pw = (kvt * w_t).astype(BF16)
    member = jnp.where(lax.broadcasted_iota(jnp.int32, (nb, rows), 0)
                       == blk0 + (lax.broadcasted_iota(jnp.int32, (nb, rows), 1) >> BLOCK_SHIFT),
                       1.0, 0.0).astype(BF16)
    return lax.dot_general(member, pw, NT_DIMS, preferred_element_type=F32)


def _proj_prompt_kernel(x_ref, wt_ref, wc_ref, cw_ref, kvc_ref, kvs_ref, kvw_ref, vts_ref, vtw_ref,
                        kas_ref, kaw_ref, qt_ref, g_ref, cmp_ref, conv_ref, zl_ref, carry_ref, *, tq, tk):
    tm = x_ref.shape[0]
    x = x_ref[...].astype(BF16)
    _conv_prompt_tile(x, wt_ref, cw_ref, conv_ref, zl_ref, carry_ref)

    def nt(r0, r1):
        return lax.dot_general(wt_ref[r0:r1, :], x, NT_DIMS, preferred_element_type=F32)

    c = ROW_KV
    kv2 = 2 * KV_WIDTH
    qall = (nt(ROW_Q, ROW_KV) * Q_SCALE).astype(BF16)
    pad_rows = jnp.zeros((K_AUG - HEAD_DIM, Q_PER_KV * tq), BF16)
    for g in range(KV_HEADS):
        for j in range(tm // tq):
            heads = [qall[(g * Q_PER_KV + r) * HEAD_DIM:(g * Q_PER_KV + r + 1) * HEAD_DIM, j * tq:(j + 1) * tq]
                     for r in range(Q_PER_KV)]
            qt_ref[0, g, j, 0:HEAD_DIM, :] = jnp.concatenate(heads, axis=1)
            qt_ref[0, g, j, HEAD_DIM:K_AUG, :] = pad_rows

    kvt_c = nt(c, c + kv2)
    kvc_ref[0] = kvt_c
    cmp_ref[0] = _block_sums_t(kvt_c, wc_ref[...])

    hot_row = lax.broadcasted_iota(jnp.int32, (K_AUG - HEAD_DIM, tm), 0)
    row_blk = (lax.broadcasted_iota(jnp.int32, (K_AUG - HEAD_DIM, tm), 1) >> BLOCK_SHIFT) & (tk // BLOCK - 1)
    one_hot_t = jnp.where(hot_row == row_blk, 1.0, 0.0)
    sum_rows = jnp.where(lax.broadcasted_iota(jnp.int32, (V_AUG - HEAD_DIM, tk), 0) == 0, 1.0, 0.0).astype(BF16)
    for i, (kv_ref, vt_ref, ka_ref) in enumerate(((kvs_ref, vts_ref, kas_ref), (kvw_ref, vtw_ref, kaw_ref))):
        kvt = nt(c + (i + 1) * kv2, c + (i + 2) * kv2)
        kv_ref[0] = kvt
        for g in range(KV_HEADS):
            v_rows = kvt[KV_WIDTH + g * HEAD_DIM:KV_WIDTH + (g + 1) * HEAD_DIM]
            for jk in range(tm // tk):
                vt_ref[0, g, jk, 0:HEAD_DIM, :] = v_rows[:, jk * tk:(jk + 1) * tk].astype(BF16)
                vt_ref[0, g, jk, HEAD_DIM:V_AUG, :] = sum_rows
            k_t = kvt[g * HEAD_DIM:(g + 1) * HEAD_DIM]
            ka_ref[0, g] = jnp.concatenate([k_t, one_hot_t], axis=0).T.astype(BF16)

    logits = nt(ROW_GATE, ROW_GATE + g_ref.shape[2])
    gates = 1.0 / (1.0 + jnp.exp(-logits))
    for j in range(tm // tq):
        g_ref[0, j] = gates[:, j * tq:(j + 1) * tq]


def _proj_prompt(x, w_all, layer, w_c, cw, b, t, tm, tq, tk):
    n, d = x.shape
    cdim = cw.shape[1]
    tiles = t // tm
    g_rows = 64
    const = lambda a: pl.BlockSpec(a.shape, lambda i, j: (0,) * a.ndim, pipeline_mode=pl.Buffered(1))
    kvt_spec = pl.BlockSpec((1, 2 * KV_WIDTH, tm), lambda i, j: (i, 0, j))
    vt_spec = pl.BlockSpec((1, KV_HEADS, tm // tk, V_AUG, tk), lambda i, j: (i, 0, j, 0, 0))
    ka_spec = pl.BlockSpec((1, KV_HEADS, tm, K_AUG), lambda i, j: (i, 0, j, 0))
    kvt_shape = jax.ShapeDtypeStruct((b, 2 * KV_WIDTH, t), F32)
    vt_shape = jax.ShapeDtypeStruct((b, KV_HEADS, t // tk, V_AUG, tk), BF16)
    ka_shape = jax.ShapeDtypeStruct((b, KV_HEADS, t, K_AUG), BF16)
    wide = Q_PER_KV * tq
    return pl.pallas_call(
        functools.partial(_proj_prompt_kernel, tq=tq, tk=tk),
        grid=(b, tiles),
        in_specs=[pl.BlockSpec((tm, d), lambda i, j: (i * tiles + j, 0)), _w_in_spec(w_all, layer), const(w_c),
                  const(cw)],
        out_specs=[
            kvt_spec, kvt_spec, kvt_spec, vt_spec, vt_spec, ka_spec, ka_spec,
            pl.BlockSpec((1, KV_HEADS, tm // tq, K_AUG, wide), lambda i, j: (i, 0, j, 0, 0)),
            pl.BlockSpec((1, tm // tq, g_rows, tq), lambda i, j: (i, j, 0, 0)),
            pl.BlockSpec((1, tm // BLOCK, 2 * KV_WIDTH), lambda i, j: (i, j, 0)),
            pl.BlockSpec((tm, cdim), lambda i, j: (i * tiles + j, 0)),
            pl.BlockSpec((SUBLANES, cdim), lambda i, j: (i * tiles + j, 0)),
        ],
        out_shape=[
            kvt_shape, kvt_shape, kvt_shape, vt_shape, vt_shape, ka_shape, ka_shape,
            jax.ShapeDtypeStruct((b, KV_HEADS, t // tq, K_AUG, wide), BF16),
            jax.ShapeDtypeStruct((b, t // tq, g_rows, tq), F32),
            jax.ShapeDtypeStruct((b, t // BLOCK, 2 * KV_WIDTH), F32),
            jax.ShapeDtypeStruct((n, cdim), BF16),
            jax.ShapeDtypeStruct((n // tm * SUBLANES, cdim), F32),
        ],
        scratch_shapes=[pltpu.VMEM((SUBLANES, cdim), F32)],
        compiler_params=_cparams(("parallel", "arbitrary")),
        name="proj_prompt",
    )(x, w_all, w_c, cw)


def _conv_from_z(z, zp, bgate, cw_ref, c0, c1):
    rows = z.shape[0]
    sub = lax.broadcasted_iota(jnp.int32, z.shape, 0) & (SUBLANES - 1)
    z1 = jnp.where(sub >= 1, pltpu.roll(z, 1, 0), pltpu.roll(zp, rows - (SUBLANES - 1), 0))
    z2 = jnp.where(sub >= 2, pltpu.roll(z, 2, 0), pltpu.roll(zp, rows - (SUBLANES - 2), 0))
    w0 = cw_ref[0:1, c0:c1]
    w1 = cw_ref[1:2, c0:c1]
    w2 = cw_ref[2:3, c0:c1]
    return bgate * (w2 * z + w0 * z2 + w1 * z1)


def _conv_prompt_tile(x, w_ref, cw_ref, y_ref, zl_ref, carry_ref, chunk=512):
    cdim = y_ref.shape[1]

    @pl.when(pl.program_id(1) == 0)
    def _():
        carry_ref[...] = jnp.zeros(carry_ref.shape, F32)

    for c0 in range(0, cdim, chunk):
        c1 = c0 + chunk

        def mm(off):
            rows = w_ref[ROW_CONV + off + c0:ROW_CONV + off + c1, :]
            return lax.dot_general(x, rows, NT_DIMS, preferred_element_type=F32)

        z = mm(2 * cdim) * mm(0)
        bgate = mm(cdim)
        zp = jnp.concatenate([carry_ref[:, c0:c1], z[:-SUBLANES]], axis=0)
        y_ref[:, c0:c1] = _conv_from_z(z, zp, bgate, cw_ref, c0, c1).astype(BF16)
        carry_ref[:, c0:c1] = z[-SUBLANES:]
        zl_ref[:, c0:c1] = z[-SUBLANES:]


def _conv_sample_kernel(x_ref, zp_ref, w_ref, cw_ref, y_ref, z_ref, *, chunk):
    cdim = y_ref.shape[1]
    x = x_ref[...].astype(BF16)
    for c0 in range(0, cdim, chunk):
        c1 = c0 + chunk

        def mm(off):
            rows = w_ref[ROW_CONV + off + c0:ROW_CONV + off + c1, :]
            return lax.dot_general(x, rows, NT_DIMS, preferred_element_type=F32)

        z = mm(2 * cdim) * mm(0)
        bgate = mm(cdim)
        y_ref[:, c0:c1] = _conv_from_z(z, zp_ref[:, c0:c1], bgate, cw_ref, c0, c1).astype(BF16)
        z_ref[:, c0:c1] = z


def _conv_sample(x, zp, w_all, layer, cw, chunk=512):
    n, d = x.shape
    cdim = cw.shape[1]
    full = lambda shape: pl.BlockSpec(shape, lambda i: (0, 0))
    return pl.pallas_call(
        functools.partial(_conv_sample_kernel, chunk=chunk),
        grid=(1,),
        in_specs=[full((n, d)), full((n, cdim)), _w_in_spec(w_all, layer), full((CONV_W, cdim))],
        out_specs=[full((n, cdim)), full((n, cdim))],
        out_shape=[jax.ShapeDtypeStruct((n, cdim), BF16), jax.ShapeDtypeStruct((n, cdim), F32)],
        compiler_params=_cparams(("arbitrary",)),
        name="conv_sample",
    )(x, zp, w_all, cw)


def _cmp_rows_kernel(kv_ref, w_ref, o_ref):
    nb = o_ref.shape[1]
    x = kv_ref[0].reshape(nb, BLOCK, kv_ref.shape[2])
    o_ref[0] = jnp.sum(x * w_ref[...][None], axis=1)


def _cmp_rows(kv, w_cmp, nb):
    b, l, c = kv.shape
    return pl.pallas_call(
        _cmp_rows_kernel,
        grid=(b, l // (nb * BLOCK)),
        in_specs=[
            pl.BlockSpec((1, nb * BLOCK, c), lambda i, j: (i, j, 0)),
            pl.BlockSpec((BLOCK, c), lambda i, j: (0, 0)),
        ],
        out_specs=pl.BlockSpec((1, nb, c), lambda i, j: (i, j, 0)),
        out_shape=jax.ShapeDtypeStruct((b, l // BLOCK, c), F32),
        compiler_params=_cparams(("parallel", "parallel")),
        name="cmp_rows",
    )(kv, w_cmp)


def _cmp_pages_kernel(pt_ref, *refs, pages_per_step):
    page_refs = refs[:pages_per_step]
    w_ref = refs[pages_per_step]
    o_ref = refs[pages_per_step + 1]
    w = w_ref[...]
    nb = o_ref.shape[1]
    bpp = w.shape[1] // BLOCK
    acc = jnp.zeros(o_ref.shape[1:], F32)
    for j in range(pages_per_step):
        acc = acc + _block_sums_t(page_refs[j][0], w, nb, j * bpp)
    o_ref[0] = acc


def _page_specs(c, page_rows, pages_per_step, page0):
    def spec(j):
        return pl.BlockSpec((1, c, page_rows), lambda b, s, pt: (page0 + pt[b, s * pages_per_step + j], 0, 0))
    return [spec(j) for j in range(pages_per_step)]


def _cmp_pages(pool_t, page0, page_table, w_cmp_t, pages_per_step=PAGES_PER_STEP):
    _, c, page_rows = pool_t.shape
    b, n_pages = page_table.shape
    pages_per_step = min(pages_per_step, n_pages)
    nb = pages_per_step * page_rows // BLOCK
    grid_spec = pltpu.PrefetchScalarGridSpec(
        num_scalar_prefetch=1,
        grid=(b, n_pages // pages_per_step),
        in_specs=_page_specs(c, page_rows, pages_per_step, page0)
        + [pl.BlockSpec((c, page_rows), lambda i, s, pt: (0, 0))],
        out_specs=pl.BlockSpec((1, nb, c), lambda i, s, pt: (i, s, 0)),
    )
    return pl.pallas_call(
        functools.partial(_cmp_pages_kernel, pages_per_step=pages_per_step),
        grid_spec=grid_spec,
        out_shape=jax.ShapeDtypeStruct((b, n_pages * page_rows // BLOCK, c), F32),
        compiler_params=_cparams(("parallel", "parallel")),
        name="cmp_pages",
    )(page_table, *([pool_t] * pages_per_step), w_cmp_t)


def _masked_softmax0(s, mask):
    s = jnp.where(mask, s, NEG)
    m = jnp.max(s, axis=0, keepdims=True)
    e = jnp.where(mask, jnp.exp2(s - m), 0.0)
    return e / jnp.maximum(jnp.sum(e, axis=0, keepdims=True), 1e-30)


def _topk_mask0(score, blk, n_sel):
    big = jnp.int32(2 ** 30)
    sel = jnp.zeros(score.shape, F32)
    for _ in range(n_sel):
        m = jnp.max(score, axis=0, keepdims=True)
        idx = jnp.min(jnp.where(score == m, blk, big), axis=0, keepdims=True)
        pick = blk == idx
        sel = jnp.where(pick, 1.0, sel)
        score = jnp.where(pick, -jnp.inf, score)
    return sel


def _sel_to_bias(sel):
    return (sel - 1.0) * (-NEG)


def _write_selection_bias(write, imp, blk, cur, n_valid_blk):
    n_sel = min(N_SEL, n_valid_blk)
    below = -FORCED_SCORE * (1.0 + blk.astype(F32) * (2.0 ** -10))
    score = jnp.where(blk <= cur, imp, below)
    score = jnp.where(blk == cur - 1, FORCED_SCORE, score)
    score = jnp.where(blk == cur, 2 * FORCED_SCORE, score)
    score = jnp.where(blk == 0, 3 * FORCED_SCORE, score)
    score = jnp.where(blk < n_valid_blk, score, -jnp.inf)

    sel = jnp.zeros(score.shape, F32)
    left = score
    for _ in range(n_sel):
        pick = left == jnp.max(left, axis=0, keepdims=True)
        sel = jnp.where(pick, 1.0, sel)
        left = jnp.where(pick, -jnp.inf, left)
    write(_sel_to_bias(sel))
    miscount = jnp.max(jnp.abs(jnp.sum(sel, axis=0, keepdims=True) - n_sel))

    @pl.when(miscount > 0.0)
    def _():
        write(_sel_to_bias(_topk_mask0(score, blk, n_sel)))


def _flash_update(s, bias, v_dot, m_ref, l_ref, acc_ref):
    if bias is not None:
        s = bias + s
    m_old = m_ref[...]
    m_new = jnp.maximum(m_old, jnp.max(s, axis=0, keepdims=True))
    alpha = jnp.exp2(m_old - m_new)
    p = jnp.exp2(s - m_new)
    if l_ref is not None:
        l_ref[...] = alpha * l_ref[...] + jnp.sum(p, axis=0, keepdims=True)
    acc_ref[...] = alpha * acc_ref[...] + v_dot(p.astype(BF16))
    m_ref[...] = m_new


def _flash_init(m_ref, l_ref, acc_ref):
    m_ref[...] = jnp.full(m_ref.shape, NEG, F32)
    if l_ref is not None:
        l_ref[...] = jnp.zeros(l_ref.shape, F32)
    acc_ref[...] = jnp.zeros(acc_ref.shape, F32)


def _attn_prompt_kernel(qt_ref, kc_ref, vct_ref, ks_ref, vst_ref, kw_ref, vwt_ref, g_ref,
                        o_ref, selb_ref, m_ref, acc_ref, out_ref, *, tq, tk):
    gi = pl.program_id(1)
    qi = pl.program_id(2)
    q0 = qi * tq
    n_blk = kc_ref.shape[2]
    wide = Q_PER_KV * tq
    qt = qt_ref[0, 0, 0]
    q_rows = qt[0:HEAD_DIM]
    pos = q0 + lax.broadcasted_iota(jnp.int32, (1, tq), 1)
    pos_w = q0 + (lax.broadcasted_iota(jnp.int32, (1, wide), 1) & (tq - 1))

    def gate(branch):
        rows = [g_ref[0, 0, pl.ds((gi * Q_PER_KV + r) * N_BRANCH + branch, 1), :] for r in range(Q_PER_KV)]
        return jnp.concatenate(rows, axis=1)

    blk_w = lax.broadcasted_iota(jnp.int32, (n_blk, wide), 0)
    p = _masked_softmax0(jnp.dot(kc_ref[0, 0], q_rows, preferred_element_type=F32),
                         blk_w * BLOCK + (BLOCK - 1) <= pos_w)
    o_c = jnp.dot(vct_ref[0, 0], p.astype(BF16), preferred_element_type=F32)
    out_ref[...] = gate(0) * o_c
    imp = p[:, 0:tq]
    for r in range(1, Q_PER_KV):
        imp = imp + p[:, r * tq:(r + 1) * tq]
    blk = lax.broadcasted_iota(jnp.int32, (n_blk, tq), 0)
    def write_selb(bias):
        for r in range(Q_PER_KV):
            selb_ref[:, r * tq:(r + 1) * tq] = bias

    _write_selection_bias(write_selb, imp, blk, pos >> BLOCK_SHIFT, n_blk)

    kpos_tile = lax.broadcasted_iota(jnp.int32, (tk, tq), 0)
    bpt = tk // BLOCK
    bias_pad = jnp.zeros((K_AUG - HEAD_DIM - bpt, wide), F32)

    cw = wide // COL_GROUPS
    q_lo = [(h * cw) % tq for h in range(COL_GROUPS)]

    groups = [slice(h * cw, (h + 1) * cw) for h in range(COL_GROUPS)]

    def tiles_update(k_ref, vt_ref, tiles):
        staged = []
        for kt, q_op, bias, key_rows in tiles:
            k_tile = k_ref[0, 0, pl.ds(pl.multiple_of(kt * tk, tk), tk), :]
            rows = [(0, tk) if key_rows is None else key_rows(h) for h in range(COL_GROUPS)]
            scores = [jnp.dot(k_tile[r0:r1], q_op[:, cols], preferred_element_type=F32)
                      for (r0, r1), cols in zip(rows, groups)]
            staged.append((vt_ref[0, 0, kt], rows, scores, bias))
        for vt, rows, scores, bias in staged:
            for h, (s, (r0, r1), cols) in enumerate(zip(scores, rows, groups)):
                group_bias = None if bias is None else bias[r0:r1, q_lo[h]:q_lo[h] + cw]
                v_dot = lambda pr, vt=vt, r0=r0, r1=r1: jnp.dot(vt[:, r0:r1], pr, preferred_element_type=F32)
                _flash_update(s, group_bias, v_dot, m_ref.at[:, cols], None, acc_ref.at[:, cols])

    def tile_update(k_ref, vt_ref, kt, q_op, bias, key_rows=None):
        tiles_update(k_ref, vt_ref, [(kt, q_op, bias, key_rows)])

    def finish(branch):
        o = acc_ref[0:HEAD_DIM, :] / jnp.maximum(acc_ref[HEAD_DIM:HEAD_DIM + 1, :], 1e-30)
        out_ref[...] = out_ref[...] + gate(branch) * o

    def q_with_block_bias(kt):
        tile_bias = selb_ref[pl.ds(pl.multiple_of(kt * bpt, bpt), bpt), :]
        return jnp.concatenate([q_rows, jnp.concatenate([tile_bias, bias_pad], axis=0).astype(BF16)], axis=0)

    kt_diag = (q0 + tq - 1) // tk
    _flash_init(m_ref, None, acc_ref)

    def sel_body(kt, carry):
        tile_update(ks_ref, vst_ref, kt, q_with_block_bias(kt), None)
        return carry

    lax.fori_loop(0, kt_diag, sel_body, 0)
    causal = jnp.where(kt_diag * tk + kpos_tile <= pos, 0.0, NEG)
    below_diag = lambda h: (0, q_lo[h] + cw)
    tile_update(ks_ref, vst_ref, kt_diag, q_with_block_bias(kt_diag), causal, below_diag)
    finish(1)

    _flash_init(m_ref, None, acc_ref)

    @pl.when(qi > 0)
    def _():
        older = jnp.where(pos - ((kt_diag - 1) * tk + kpos_tile) < WINDOW, 0.0, NEG)
        tile_update(kw_ref, vwt_ref, kt_diag - 1, qt, older, lambda h: (q_lo[h], tk))

    tile_update(kw_ref, vwt_ref, kt_diag, qt, causal, below_diag)
    finish(2)

    for pair in range(Q_PER_KV // 2):
        two = jnp.concatenate([out_ref[:, (2 * pair + h) * tq:(2 * pair + h + 1) * tq] for h in range(2)], axis=0)
        o_ref[:, pair * 2 * HEAD_DIM:(pair + 1) * 2 * HEAD_DIM] = two.T.astype(o_ref.dtype)


def _attn_prompt(qt, kc, vct, kas, vst, kaw, vwt, gates, tq, tk):
    b, g, nq, ka, wide = qt.shape
    t = nq * tq
    assert tq & (tq - 1) == 0 and wide == Q_PER_KV * tq and ka == K_AUG
    assert tq == tk == WINDOW and 2 * wide // COL_GROUPS <= tq
    n_blk = kc.shape[2]
    nkt = t // tk
    d = HEAD_DIM
    bg = lambda *tail: (lambda i, j, q: (i, j) + tail)
    kern = functools.partial(_attn_prompt_kernel, tq=tq, tk=tk)
    return pl.pallas_call(
        kern,
        grid=(b, g, nq),
        in_specs=[
            pl.BlockSpec((1, 1, 1, ka, wide), lambda i, j, q: (i, j, q, 0, 0)),
            pl.BlockSpec((1, 1, n_blk, d), bg(0, 0)),
            pl.BlockSpec((1, 1, d, n_blk), bg(0, 0)),
            pl.BlockSpec((1, 1, t, ka), bg(0, 0)),
            pl.BlockSpec((1, 1, nkt, V_AUG, tk), bg(0, 0, 0)),
            pl.BlockSpec((1, 1, t, ka), bg(0, 0)),
            pl.BlockSpec((1, 1, nkt, V_AUG, tk), bg(0, 0, 0)),
            pl.BlockSpec((1, 1, gates.shape[2], tq), lambda i, j, q: (i, q, 0, 0)),
        ],
        out_specs=pl.BlockSpec((tq, Q_PER_KV * d), lambda i, j, q: (i * nq + q, j)),
        out_shape=jax.ShapeDtypeStruct((b * t, ATTN_WIDTH), BF16),
        scratch_shapes=[
            pltpu.VMEM((n_blk, wide), F32),
            pltpu.VMEM((1, wide), F32),
            pltpu.VMEM((V_AUG, wide), F32),
            pltpu.VMEM((d, wide), F32),
        ],
        compiler_params=_cparams(("parallel", "parallel", "parallel")),
        name="attn_prompt",
    )(qt, kc, vct, kas, vst, kaw, vwt, gates)


def _attn_sample_kernel(pt_ref, *refs, pages_per_step, past_len, n_blk_valid):
    page_refs = refs[:pages_per_step]
    (qbd_ref, kc_ref, vc_ref, new_s_ref, win_ref, new_w_ref, gt_ref, pos_ref,
     o_ref, selb_ref, m_ref, l_ref, acc_ref, out_ref) = refs[pages_per_step:]
    step = pl.program_id(1)
    n_steps = pl.num_programs(1)
    ncol = qbd_ref.shape[2]
    kvw = qbd_ref.shape[1]
    qbd = qbd_ref[0]
    pos = pos_ref[...]
    group_cols = ncol // Q_PER_KV

    def tile_rows(rows, bias):
        k, v = rows[:, :kvw].astype(BF16), rows[:, kvw:].astype(BF16)
        s = jnp.dot(k, qbd, preferred_element_type=F32)
        v_dot = lambda p: lax.dot_general(v, p, TN_DIMS, preferred_element_type=F32)
        _flash_update(s, bias, v_dot, m_ref, l_ref, acc_ref)

    def tile_chan(page, bias):
        kt, vt = page[:kvw].astype(BF16), page[kvw:].astype(BF16)
        half = page.shape[1] // 2
        s = jnp.concatenate([lax.dot_general(kt[:, :half], qbd, TN_DIMS, preferred_element_type=F32),
                             lax.dot_general(kt[:, half:], qbd, TN_DIMS, preferred_element_type=F32)], axis=0)
        v_dot = lambda p: (jnp.dot(vt[:, :half], p[:half], preferred_element_type=F32)
                           + jnp.dot(vt[:, half:], p[half:], preferred_element_type=F32))
        _flash_update(s, bias, v_dot, m_ref, l_ref, acc_ref)

    def window_bias(n_keys, p0):
        delta = pos - (p0 + lax.broadcasted_iota(jnp.int32, (n_keys, ncol), 0))
        return jnp.where((delta >= 0) & (delta < WINDOW), 0.0, NEG)

    @pl.when(step == 0)
    def _():
        n_blk = kc_ref.shape[1]
        blk = lax.broadcasted_iota(jnp.int32, (n_blk, ncol), 0)
        cmask = (blk * BLOCK + (BLOCK - 1) <= pos) & (blk < n_blk_valid)
        s = jnp.dot(kc_ref[0], qbd, preferred_element_type=F32)
        p = _masked_softmax0(s, cmask)
        o_c = lax.dot_general(vc_ref[0], p.astype(BF16), TN_DIMS, preferred_element_type=F32)
        out_ref[...] = gt_ref[0, 0:1, :] * o_c
        imp = p
        for r in range(1, Q_PER_KV):
            imp = imp + pltpu.roll(p, r * group_cols, 1)
        def write_selb(bias):
            selb_ref[...] = bias

        _write_selection_bias(write_selb, imp, blk, pos >> BLOCK_SHIFT, n_blk_valid)

        _flash_init(m_ref, l_ref, acc_ref)
        n_win = win_ref.shape[2]
        tile_rows(new_w_ref[0], window_bias(new_w_ref.shape[1], past_len))
        tile_chan(win_ref[0], window_bias(n_win, past_len - n_win))
        o_w = acc_ref[...] / jnp.maximum(l_ref[...], 1e-30)
        out_ref[...] = out_ref[...] + gt_ref[0, 2:3, :] * o_w

        _flash_init(m_ref, l_ref, acc_ref)
        n_new = new_s_ref.shape[1]
        kpos = past_len + lax.broadcasted_iota(jnp.int32, (n_new, ncol), 0)
        chosen = jnp.broadcast_to(selb_ref[pl.ds(past_len // BLOCK, 1), :], (n_new, ncol))
        tile_rows(new_s_ref[0], jnp.where(kpos <= pos, chosen, NEG))

    page_rows = page_refs[0].shape[2]
    step_keys = pages_per_step * page_rows
    step_blocks = step_keys // BLOCK
    keys = jnp.concatenate([page_refs[j][0] for j in range(pages_per_step)], axis=1)
    chosen = jnp.concatenate(
        [jnp.broadcast_to(selb_ref[pl.ds(step * step_blocks + i, 1), :], (BLOCK, ncol)) for i in range(step_blocks)],
        axis=0)
    tile_chan(keys, chosen)

    @pl.when(step == n_steps - 1)
    def _():
        o_s = acc_ref[...] / jnp.maximum(l_ref[...], 1e-30)
        o_ref[0] = out_ref[...] + gt_ref[0, 1:2, :] * o_s


def _attn_sample(pool_t, page0, page_table, qbd, kc, vc, new_s, win_t, win0, new_w, gt, pos,
                 pages_per_step=PAGES_PER_STEP):
    _, c, page_rows = pool_t.shape
    b, n_pages = page_table.shape
    pages_per_step = min(pages_per_step, n_pages)
    _, kvw, ncol = qbd.shape
    n_blk = kc.shape[1]
    past_len = n_pages * page_rows
    per_b = lambda shape: pl.BlockSpec((1,) + shape, lambda i, s, pt: (i,) + (0,) * len(shape))
    grid_spec = pltpu.PrefetchScalarGridSpec(
        num_scalar_prefetch=1,
        grid=(b, n_pages // pages_per_step),
        in_specs=_page_specs(c, page_rows, pages_per_step, page0) + [
            per_b((kvw, ncol)),
            per_b((n_blk, kvw)),
            per_b((n_blk, kvw)),
            per_b(new_s.shape[1:]),
            pl.BlockSpec((1,) + win_t.shape[1:], lambda i, s, pt: (win0 + i, 0, 0)),
            per_b(new_w.shape[1:]),
            per_b((N_BRANCH, ncol)),
            pl.BlockSpec((1, ncol), lambda i, s, pt: (0, 0)),
        ],
        out_specs=per_b((kvw, ncol)),
        scratch_shapes=[
            pltpu.VMEM((n_blk, ncol), F32),
            pltpu.VMEM((1, ncol), F32),
            pltpu.VMEM((1, ncol), F32),
            pltpu.VMEM((kvw, ncol), F32),
            pltpu.VMEM((kvw, ncol), F32),
        ],
    )
    kern = functools.partial(_attn_sample_kernel, pages_per_step=pages_per_step, past_len=past_len,
                             n_blk_valid=past_len // BLOCK + 1)
    return pl.pallas_call(
        kern,
        grid_spec=grid_spec,
        out_shape=jax.ShapeDtypeStruct((b, kvw, ncol), F32),
        compiler_params=_cparams(("parallel", "arbitrary")),
        name="attn_sample",
    )(page_table, *([pool_t] * pages_per_step), qbd, kc, vc, new_s, win_t, new_w, gt, pos)


def _outproj_kernel(a_ref, c_ref, x_ref, w_ref, g_ref, b_ref, h_ref, *, alpha):
    aw = a_ref.shape[1]
    mix = jnp.dot(a_ref[...], w_ref[:aw, :], preferred_element_type=F32)
    mix = mix + jnp.dot(c_ref[...], w_ref[aw:, :], preferred_element_type=F32)
    h_ref[...] = _layer_norm(alpha * x_ref[...] + mix, g_ref[...], b_ref[...])


def _outproj_ln(attn, conv, x, w_out, layer, g, b, alpha, tm):
    n, d = x.shape
    row = lambda w: pl.BlockSpec((tm, w), lambda i: (i, 0))
    const = lambda shape: pl.BlockSpec(shape, lambda i: (0, 0))
    w_spec = pl.BlockSpec((None,) + w_out.shape[1:], lambda i: (layer, 0, 0))
    return pl.pallas_call(
        functools.partial(_outproj_kernel, alpha=alpha),
        grid=(n // tm,),
        in_specs=[row(attn.shape[1]), row(conv.shape[1]), row(d), w_spec, const((1, d)), const((1, d))],
        out_specs=row(d),
        out_shape=jax.ShapeDtypeStruct((n, d), F32),
        compiler_params=_cparams(("parallel",)),
        name="outproj_ln",
    )(attn, conv, x, w_out, g, b)


def _mlp_kernel(h_ref, w1_ref, w2_ref, g_ref, b_ref, y_ref, acc_ref, hb_ref, *, alpha):
    f = pl.program_id(1)

    @pl.when(f == 0)
    def _():
        hb_ref[...] = h_ref[...].astype(BF16)
        acc_ref[...] = jnp.zeros(acc_ref.shape, F32)

    a = jnp.dot(hb_ref[...], w1_ref[...], preferred_element_type=F32)
    a = jnp.square(jnp.maximum(a, 0.0)).astype(BF16)
    acc_ref[...] += jnp.dot(a, w2_ref[...], preferred_element_type=F32)

    @pl.when(f == pl.num_programs(1) - 1)
    def _():
        y_ref[...] = _layer_norm(alpha * h_ref[...] + acc_ref[...], g_ref[...], b_ref[...])


def _mlp_ln(h, w1, w2, layer, g, b, alpha, tm, tf):
    n, d = h.shape
    dff = w1.shape[2]
    return pl.pallas_call(
        functools.partial(_mlp_kernel, alpha=alpha),
        grid=(n // tm, dff // tf),
        in_specs=[
            pl.BlockSpec((tm, d), lambda i, f: (i, 0)),
            pl.BlockSpec((None, d, tf), lambda i, f: (layer, 0, f)),
            pl.BlockSpec((None, tf, d), lambda i, f: (layer, f, 0)),
            pl.BlockSpec((1, d), lambda i, f: (0, 0)),
            pl.BlockSpec((1, d), lambda i, f: (0, 0)),
        ],
        out_specs=pl.BlockSpec((tm, d), lambda i, f: (i, 0)),
        out_shape=jax.ShapeDtypeStruct((n, d), F32),
        scratch_shapes=[pltpu.VMEM((tm, d), F32), pltpu.VMEM((tm, d), BF16)],
        compiler_params=_cparams(("parallel", "arbitrary")),
        name="mlp_ln",
    )(h, w1, w2, g, b)


def _split_kv(kv, b, t):
    return kv.reshape(b, t, 2, KV_HEADS, HEAD_DIM)


def _prompt_layer(x, p, b, t, alpha, tm, tq, tk):
    w_c = jnp.tile(p["w_cmp"].T, (1, tm // BLOCK))
    kvt_c, kvt_s, kvt_w, vst, vwt, kas, kaw, qt, gates, cmp, conv, zlast = _proj_prompt(
        x, p["w_in_t"], p["layer"], w_c, p["conv_w"], b, t, tm, tq, tk)
    n_blk = t // BLOCK
    cmp = cmp.reshape(b, n_blk, 2, KV_HEADS, HEAD_DIM).astype(BF16)
    kc = cmp[:, :, 0].transpose(0, 2, 1, 3)
    vct = cmp[:, :, 1].transpose(0, 2, 3, 1)
    attn = _attn_prompt(qt, kc, vct, kas, vst, kaw, vwt, gates, tq, tk)
    h = _outproj_ln(attn, conv, x, p["w_out"], p["layer"], p["ln1_g"], p["ln1_b"], alpha, tm)
    y = _mlp_ln(h, p["w_mlp1"], p["w_mlp2"], p["layer"], p["ln2_g"], p["ln2_b"], alpha, min(MLP_ROWS, b * t), MLP_COLS)
    conv_state = zlast.reshape(b, t // tm, SUBLANES, -1)[:, -1, SUBLANES - (CONV_W - 1):]
    keep = min(WINDOW, t)
    rows_major = lambda kvt: kvt.reshape(b, 2, KV_HEADS, HEAD_DIM, t).transpose(0, 4, 1, 2, 3)
    return (y, rows_major(kvt_c), rows_major(kvt_s), rows_major(kvt_w)[:, t - keep:], conv_state)


def _sample_layer(x, p, pool_c, pool_s, page0, win_t, win0, cache_win, state_conv, page_table, db, dt, alpha):
    n = db * dt
    page_rows = pool_c.shape[2]
    n_pages = page_table.shape[1]
    past_len = n_pages * page_rows
    c = 2 * KV_WIDTH
    q, kv_c, kv_s, kv_w, gates = _proj(x, p["w_in_t"], p["layer"], n)

    zp = jnp.pad(state_conv, ((0, 0), (SUBLANES - (CONV_W - 1), 0), (0, 0))).reshape(n, -1)
    conv, z = _conv_sample(x, zp, p["w_in_t"], p["layer"], p["conv_w"])
    conv_state = jnp.concatenate([state_conv, z.reshape(db, dt, -1)], axis=1)[:, -(CONV_W - 1):]

    cmp_past = _cmp_pages(pool_c, page0, page_table, p["w_cmp_t"])
    tail = jnp.pad(kv_c.reshape(db, dt, c), ((0, 0), (0, SUBLANES * BLOCK - dt), (0, 0)))
    cmp_tail = _cmp_rows(tail, p["w_cmp"], SUBLANES)
    cmp_all = jnp.concatenate([cmp_past, cmp_tail], axis=1).astype(BF16)
    kc, vc = cmp_all[..., :KV_WIDTH], cmp_all[..., KV_WIDTH:]

    q5 = q.reshape(db, dt, KV_HEADS, Q_PER_KV, HEAD_DIM).transpose(0, 2, 4, 3, 1)
    eye = jnp.eye(KV_HEADS, dtype=q.dtype)
    qbd = (q5[:, :, :, :, None, :] * eye[None, :, None, None, :, None]).reshape(db, KV_WIDTH, N_HEADS * dt)
    gt = gates[:, :N_HEADS * N_BRANCH].reshape(db, dt, KV_HEADS, Q_PER_KV, N_BRANCH)
    gt = gt.transpose(0, 4, 3, 2, 1).reshape(db, N_BRANCH, N_HEADS * dt)
    pos = jnp.tile(past_len + jnp.arange(dt, dtype=jnp.int32), N_HEADS).reshape(1, N_HEADS * dt)
    new_rows = lambda kv: jnp.pad(kv.reshape(db, dt, c), ((0, 0), (0, 2 * SUBLANES - dt), (0, 0)))
    o = _attn_sample(pool_s, page0, page_table, qbd, kc, vc, new_rows(kv_s), win_t, win0, new_rows(kv_w), gt, pos)
    o = o.reshape(db, KV_HEADS, HEAD_DIM, Q_PER_KV, KV_HEADS, dt)
    o = jnp.stack([o[:, g, :, :, g, :] for g in range(KV_HEADS)], axis=1)
    attn = o.transpose(0, 4, 1, 3, 2).reshape(n, ATTN_WIDTH).astype(BF16)

    h = _outproj_ln(attn, conv, x, p["w_out"], p["layer"], p["ln1_g"], p["ln1_b"], alpha, n)
    y = _mlp_ln(h, p["w_mlp1"], p["w_mlp2"], p["layer"], p["ln2_g"], p["ln2_b"], alpha, n, 1024)
    win_keep = cache_win.shape[1]
    new_win = jnp.concatenate([cache_win, _split_kv(kv_w, db, dt)], axis=1)[:, -win_keep:]
    return y, _split_kv(kv_c, db, dt), _split_kv(kv_s, db, dt), new_win, conv_state


def kernel(x_prompt, x_sample, cache_cmp, cache_slc, cache_win, state_conv, page_table, w_in, w_cmp_k, w_cmp_v,
           conv_w, w_out, ln1_g, ln1_b, w_mlp1, w_mlp2, ln2_g, ln2_b):
    depth = w_in.shape[0]
    b, t, d = x_prompt.shape
    db, dt, _ = x_sample.shape
    alpha = (2.0 * depth) ** 0.25
    tm = min(512, t)
    tq, tk = min(512, t), min(512, t)

    n_pool, page_rows = cache_cmp.shape[1:3]
    chan_major = lambda c: c.transpose(0, 1, 3, 4, 5, 2).reshape(c.shape[0] * c.shape[1], -1, c.shape[2])
    pool_c, pool_s, win_t = chan_major(cache_cmp), chan_major(cache_slc), chan_major(cache_win)

    w_out_b, w_mlp1_b, w_mlp2_b = w_out.astype(BF16), w_mlp1.astype(BF16), w_mlp2.astype(BF16)
    yp = x_prompt.reshape(b * t, d)
    ys = x_sample.reshape(db * dt, d)
    outs = [[] for _ in range(8)]
    w_in_t = w_in.transpose(0, 2, 1).astype(BF16)
    for l in range(depth):
        w_cmp = jnp.concatenate([jnp.tile(w_cmp_k[l], (1, KV_HEADS)), jnp.tile(w_cmp_v[l], (1, KV_HEADS))], axis=1)
        p = {
            "w_in_t": w_in_t,
            "w_cmp": w_cmp,
            "w_cmp_t": jnp.tile(w_cmp.T, (1, page_rows // BLOCK)),
            "conv_w": conv_w[l],
            "layer": l,
            "w_out": w_out_b,
            "ln1_g": ln1_g[l].reshape(1, d), "ln1_b": ln1_b[l].reshape(1, d),
            "w_mlp1": w_mlp1_b, "w_mlp2": w_mlp2_b,
            "ln2_g": ln2_g[l].reshape(1, d), "ln2_b": ln2_b[l].reshape(1, d),
        }
        yp, c1, s1, w1, v1 = _prompt_layer(yp, p, b, t, alpha, tm, tq, tk)
        ys, c2, s2, w2, v2 = _sample_layer(ys, p, pool_c, pool_s, l * n_pool, win_t, l * db, cache_win[l],
                                           state_conv[l], page_table, db, dt, alpha)
        for lst, v in zip(outs, (c1, s1, w1, v1, c2, s2, w2, v2)):
            lst.append(v)
    stacked = [jnp.stack(o) for o in outs]
    return (yp.reshape(b, t, d), ys.reshape(db, dt, d), *stacked)
```

```python
import functools
import math

import jax
import jax.numpy as jnp
from jax import lax
from jax.experimental import pallas as pl
from jax.experimental.pallas import tpu as pltpu

HEAD_DIM = 64
KV_HEADS = 4
Q_PER_KV = 4
N_HEADS = KV_HEADS * Q_PER_KV
ATTN_WIDTH = N_HEADS * HEAD_DIM
KV_WIDTH = KV_HEADS * HEAD_DIM
N_BRANCH = 3
BLOCK = 64
BLOCK_SHIFT = BLOCK.bit_length() - 1
N_SEL = 16
WINDOW = 512
CONV_W = 3
LN_EPS = 1e-5
NEG = -1e30
FORCED_SCORE = 1e9
Q_SCALE = HEAD_DIM ** -0.5 * math.log2(math.e)
GATE_PAD = 128

SUBLANES = 8
VMEM_LIMIT = 56 * 1024 * 1024

BF16 = jnp.bfloat16
F32 = jnp.float32
NT_DIMS = (((1,), (1,)), ((), ()))
TN_DIMS = (((0,), (0,)), ((), ()))

ROW_Q = 0
ROW_KV = ATTN_WIDTH
ROW_GATE = ROW_KV + 6 * KV_WIDTH
ROW_CONV = ROW_GATE + N_HEADS * N_BRANCH


def _w_in_spec(w_all, layer):
    return pl.BlockSpec((None,) + w_all.shape[1:], lambda *_: (layer, 0, 0), pipeline_mode=pl.Buffered(1))


def _cparams(sem):
    return pltpu.CompilerParams(dimension_semantics=sem, vmem_limit_bytes=VMEM_LIMIT)


def _layer_norm(v, g, b):
    mu = jnp.mean(v, axis=-1, keepdims=True)
    d = v - mu
    var = jnp.mean(d * d, axis=-1, keepdims=True)
    return d * lax.rsqrt(var + LN_EPS) * g + b


def _proj_kernel(x_ref, w_ref, q_ref, kc_ref, ks_ref, kw_ref, g_ref):
    x = x_ref[...].astype(BF16)

    def mm(r0, r1):
        return lax.dot_general(x, w_ref[r0:r1, :], NT_DIMS, preferred_element_type=F32)

    kv2 = 2 * KV_WIDTH
    q_ref[...] = (mm(ROW_Q, ROW_KV) * Q_SCALE).astype(BF16)
    kc_ref[...] = mm(ROW_KV, ROW_KV + kv2)
    ks_ref[...] = mm(ROW_KV + kv2, ROW_KV + 2 * kv2)
    kw_ref[...] = mm(ROW_KV + 2 * kv2, ROW_KV + 3 * kv2)
    logits = mm(ROW_GATE, ROW_GATE + GATE_PAD)
    g_ref[...] = 1.0 / (1.0 + jnp.exp(-logits))


K_AUG = 2 * HEAD_DIM
V_AUG = HEAD_DIM + 16
MLP_ROWS, MLP_COLS = 512, 1024
OUTPROJ_ROW_PARTS = 4
OUTPROJ_MIN_PART = 128
PAGES_PER_STEP = 32
COL_GROUPS = 8


def _block_sums_t(kvt, w_t, n_out=None, blk0=0):
    rows = kvt.shape[1]
    nb = rows // BLOCK if n_out is None else n_out
    pw = (kvt * w_t).astype(BF16)
    member = jnp.where(lax.broadcasted_iota(jnp.int32, (nb, rows), 0)
                       == blk0 + (lax.broadcasted_iota(jnp.int32, (nb, rows), 1) >> BLOCK_SHIFT),
                       1.0, 0.0).astype(BF16)
    return lax.dot_general(member, pw, NT_DIMS, preferred_element_type=F32)


def _proj_prompt_kernel(x_ref, wt_ref, wc_ref, cw_ref, kvc_ref, kvs_ref, kvw_ref, vts_ref, vtw_ref,
                        kas_ref, kaw_ref, qt_ref, g_ref, cmp_ref, conv_ref, zl_ref, carry_ref, *, tq, tk):
    tm = x_ref.shape[0]
    x = x_ref[...].astype(BF16)
    _conv_prompt_tile(x, wt_ref, cw_ref, conv_ref, zl_ref, carry_ref)

    def nt(r0, r1):
        return lax.dot_general(wt_ref[r0:r1, :], x, NT_DIMS, preferred_element_type=F32)

    c = ROW_KV
    kv2 = 2 * KV_WIDTH
    qall = (nt(ROW_Q, ROW_KV) * Q_SCALE).astype(BF16)
    pad_rows = jnp.zeros((K_AUG - HEAD_DIM, Q_PER_KV * tq), BF16)
    for g in range(KV_HEADS):
        for j in range(tm // tq):
            heads = [qall[(g * Q_PER_KV + r) * HEAD_DIM:(g * Q_PER_KV + r + 1) * HEAD_DIM, j * tq:(j + 1) * tq]
                     for r in range(Q_PER_KV)]
            qt_ref[0, g, j, 0:HEAD_DIM, :] = jnp.concatenate(heads, axis=1)
            qt_ref[0, g, j, HEAD_DIM:K_AUG, :] = pad_rows

    kvt_c = nt(c, c + kv2)
    kvc_ref[0] = kvt_c
    cmp_ref[0] = _block_sums_t(kvt_c, wc_ref[...])

    hot_row = lax.broadcasted_iota(jnp.int32, (K_AUG - HEAD_DIM, tm), 0)
    row_blk = (lax.broadcasted_iota(jnp.int32, (K_AUG - HEAD_DIM, tm), 1) >> BLOCK_SHIFT) & (tk // BLOCK - 1)
    one_hot_t = jnp.where(hot_row == row_blk, 1.0, 0.0)
    sum_rows = jnp.where(lax.broadcasted_iota(jnp.int32, (V_AUG - HEAD_DIM, tk), 0) == 0, 1.0, 0.0).astype(BF16)
    for i, (kv_ref, vt_ref, ka_ref) in enumerate(((kvs_ref, vts_ref, kas_ref), (kvw_ref, vtw_ref, kaw_ref))):
        kvt = nt(c + (i + 1) * kv2, c + (i + 2) * kv2)
        kv_ref[0] = kvt
        for g in range(KV_HEADS):
            v_rows = kvt[KV_WIDTH + g * HEAD_DIM:KV_WIDTH + (g + 1) * HEAD_DIM]
            for jk in range(tm // tk):
                vt_ref[0, g, jk, 0:HEAD_DIM, :] = v_rows[:, jk * tk:(jk + 1) * tk].astype(BF16)
                vt_ref[0, g, jk, HEAD_DIM:V_AUG, :] = sum_rows
            k_t = kvt[g * HEAD_DIM:(g + 1) * HEAD_DIM]
            ka_ref[0, g] = jnp.concatenate([k_t, one_hot_t], axis=0).T.astype(BF16)

    logits = nt(ROW_GATE, ROW_GATE + g_ref.shape[2])
    gates = 1.0 / (1.0 + jnp.exp(-logits))
    for j in range(tm // tq):
        g_ref[0, j] = gates[:, j * tq:(j + 1) * tq]


def _proj_prompt(x, w_all, layer, w_c, cw, b, t, tm, tq, tk):
    n, d = x.shape
    cdim = cw.shape[1]
    tiles = t // tm
    g_rows = 64
    const = lambda a: pl.BlockSpec(a.shape, lambda i, j: (0,) * a.ndim, pipeline_mode=pl.Buffered(1))
    kvt_spec = pl.BlockSpec((1, 2 * KV_WIDTH, tm), lambda i, j: (i, 0, j))
    vt_spec = pl.BlockSpec((1, KV_HEADS, tm // tk, V_AUG, tk), lambda i, j: (i, 0, j, 0, 0))
    ka_spec = pl.BlockSpec((1, KV_HEADS, tm, K_AUG), lambda i, j: (i, 0, j, 0))
    kvt_shape = jax.ShapeDtypeStruct((b, 2 * KV_WIDTH, t), F32)
    vt_shape = jax.ShapeDtypeStruct((b, KV_HEADS, t // tk, V_AUG, tk), BF16)
    ka_shape = jax.ShapeDtypeStruct((b, KV_HEADS, t, K_AUG), BF16)
    wide = Q_PER_KV * tq
    return pl.pallas_call(
        functools.partial(_proj_prompt_kernel, tq=tq, tk=tk),
        grid=(b, tiles),
        in_specs=[pl.BlockSpec((tm, d), lambda i, j: (i * tiles + j, 0)), _w_in_spec(w_all, layer), const(w_c),
                  const(cw)],
        out_specs=[
            kvt_spec, kvt_spec, kvt_spec, vt_spec, vt_spec, ka_spec, ka_spec,
            pl.BlockSpec((1, KV_HEADS, tm // tq, K_AUG, wide), lambda i, j: (i, 0, j, 0, 0)),
            pl.BlockSpec((1, tm // tq, g_rows, tq), lambda i, j: (i, j, 0, 0)),
            pl.BlockSpec((1, tm // BLOCK, 2 * KV_WIDTH), lambda i, j: (i, j, 0)),
            pl.BlockSpec((tm, cdim), lambda i, j: (i * tiles + j, 0)),
            pl.BlockSpec((SUBLANES, cdim), lambda i, j: (i * tiles + j, 0)),
        ],
        out_shape=[
            kvt_shape, kvt_shape, kvt_shape, vt_shape, vt_shape, ka_shape, ka_shape,
            jax.ShapeDtypeStruct((b, KV_HEADS, t // tq, K_AUG, wide), BF16),
            jax.ShapeDtypeStruct((b, t // tq, g_rows, tq), F32),
            jax.ShapeDtypeStruct((b, t // BLOCK, 2 * KV_WIDTH), F32),
            jax.ShapeDtypeStruct((n, cdim), BF16),
            jax.ShapeDtypeStruct((n // tm * SUBLANES, cdim), F32),
        ],
        scratch_shapes=[pltpu.VMEM((SUBLANES, cdim), F32)],
        compiler_params=_cparams(("parallel", "arbitrary")),
        name="proj_prompt",
    )(x, w_all, w_c, cw)


def _conv_from_z(z, zp, bgate, cw_ref, c0, c1):
    rows = z.shape[0]
    sub = lax.broadcasted_iota(jnp.int32, z.shape, 0) & (SUBLANES - 1)
    z1 = jnp.where(sub >= 1, pltpu.roll(z, 1, 0), pltpu.roll(zp, rows - (SUBLANES - 1), 0))
    z2 = jnp.where(sub >= 2, pltpu.roll(z, 2, 0), pltpu.roll(zp, rows - (SUBLANES - 2), 0))
    w0 = cw_ref[0:1, c0:c1]
    w1 = cw_ref[1:2, c0:c1]
    w2 = cw_ref[2:3, c0:c1]
    return bgate * (w2 * z + w0 * z2 + w1 * z1)


def _conv_prompt_tile(x, w_ref, cw_ref, y_ref, zl_ref, carry_ref, chunk=512):
    cdim = y_ref.shape[1]

    @pl.when(pl.program_id(1) == 0)
    def _():
        carry_ref[...] = jnp.zeros(carry_ref.shape, F32)

    for c0 in range(0, cdim, chunk):
        c1 = c0 + chunk

        def mm(off):
            rows = w_ref[ROW_CONV + off + c0:ROW_CONV + off + c1, :]
            return lax.dot_general(x, rows, NT_DIMS, preferred_element_type=F32)

        z = mm(2 * cdim) * mm(0)
        bgate = mm(cdim)
        zp = jnp.concatenate([carry_ref[:, c0:c1], z[:-SUBLANES]], axis=0)
        y_ref[:, c0:c1] = _conv_from_z(z, zp, bgate, cw_ref, c0, c1).astype(BF16)
        carry_ref[:, c0:c1] = z[-SUBLANES:]
        zl_ref[:, c0:c1] = z[-SUBLANES:]


def _conv_sample_kernel(x_ref, zp_ref, w_ref, cw_ref, y_ref, z_ref, *, chunk):
    cdim = y_ref.shape[1]
    x = x_ref[...].astype(BF16)
    for c0 in range(0, cdim, chunk):
        c1 = c0 + chunk

        def mm(off):
            rows = w_ref[ROW_CONV + off + c0:ROW_CONV + off + c1, :]
            return lax.dot_general(x, rows, NT_DIMS, preferred_element_type=F32)

        z = mm(2 * cdim) * mm(0)
        bgate = mm(cdim)
        y_ref[:, c0:c1] = _conv_from_z(z, zp_ref[:, c0:c1], bgate, cw_ref, c0, c1).astype(BF16)
        z_ref[:, c0:c1] = z


def _proj_sample_kernel(x_ref, zp_ref, w_ref, cw_ref, q_ref, kc_ref, ks_ref, kw_ref, g_ref, y_ref, z_ref, *, chunk):
    _proj_kernel(x_ref, w_ref, q_ref, kc_ref, ks_ref, kw_ref, g_ref)
    _conv_sample_kernel(x_ref, zp_ref, w_ref, cw_ref, y_ref, z_ref, chunk=chunk)


def _proj_sample(x, zp, w_all, layer, cw, chunk=512):
    n, d = x.shape
    cdim = cw.shape[1]
    full = lambda shape: pl.BlockSpec(shape, lambda i: (0, 0))
    widths = (ATTN_WIDTH, 2 * KV_WIDTH, 2 * KV_WIDTH, 2 * KV_WIDTH, GATE_PAD, cdim, cdim)
    dtypes = (BF16, F32, F32, F32, F32, BF16, F32)
    return pl.pallas_call(
        functools.partial(_proj_sample_kernel, chunk=chunk),
        grid=(1,),
        in_specs=[full((n, d)), full((n, cdim)), _w_in_spec(w_all, layer), full((CONV_W, cdim))],
        out_specs=[full((n, w)) for w in widths],
        out_shape=[jax.ShapeDtypeStruct((n, w), dt) for w, dt in zip(widths, dtypes)],
        compiler_params=_cparams(("arbitrary",)),
        name="proj_sample",
    )(x, zp, w_all, cw)


def _cmp_rows_kernel(kv_ref, w_ref, o_ref):
    nb = o_ref.shape[1]
    x = kv_ref[0].reshape(nb, BLOCK, kv_ref.shape[2])
    o_ref[0] = jnp.sum(x * w_ref[...][None], axis=1)


def _cmp_rows(kv, w_cmp, nb):
    b, l, c = kv.shape
    return pl.pallas_call(
        _cmp_rows_kernel,
        grid=(b, l // (nb * BLOCK)),
        in_specs=[
            pl.BlockSpec((1, nb * BLOCK, c), lambda i, j: (i, j, 0)),
            pl.BlockSpec((BLOCK, c), lambda i, j: (0, 0)),
        ],
        out_specs=pl.BlockSpec((1, nb, c), lambda i, j: (i, j, 0)),
        out_shape=jax.ShapeDtypeStruct((b, l // BLOCK, c), F32),
        compiler_params=_cparams(("parallel", "parallel")),
        name="cmp_rows",
    )(kv, w_cmp)


def _cmp_pages_kernel(pt_ref, *refs, pages_per_step):
    page_refs = refs[:pages_per_step]
    w_ref = refs[pages_per_step]
    o_ref = refs[pages_per_step + 1]
    w = w_ref[...]
    nb = o_ref.shape[1]
    bpp = w.shape[1] // BLOCK
    acc = jnp.zeros(o_ref.shape[1:], F32)
    for j in range(pages_per_step):
        acc = acc + _block_sums_t(page_refs[j][0], w, nb, j * bpp)
    o_ref[0] = acc


def _page_specs(c, page_rows, pages_per_step, page0):
    def spec(j):
        return pl.BlockSpec((1, c, page_rows), lambda b, s, pt: (page0 + pt[b, s * pages_per_step + j], 0, 0))
    return [spec(j) for j in range(pages_per_step)]


def _cmp_pages(pool_t, page0, page_table, w_cmp_t, pages_per_step=PAGES_PER_STEP):
    _, c, page_rows = pool_t.shape
    b, n_pages = page_table.shape
    pages_per_step = min(pages_per_step, n_pages)
    nb = pages_per_step * page_rows // BLOCK
    grid_spec = pltpu.PrefetchScalarGridSpec(
        num_scalar_prefetch=1,
        grid=(b, n_pages // pages_per_step),
        in_specs=_page_specs(c, page_rows, pages_per_step, page0)
        + [pl.BlockSpec((c, page_rows), lambda i, s, pt: (0, 0))],
        out_specs=pl.BlockSpec((1, nb, c), lambda i, s, pt: (i, s, 0)),
    )
    return pl.pallas_call(
        functools.partial(_cmp_pages_kernel, pages_per_step=pages_per_step),
        grid_spec=grid_spec,
        out_shape=jax.ShapeDtypeStruct((b, n_pages * page_rows // BLOCK, c), F32),
        compiler_params=_cparams(("parallel", "parallel")),
        name="cmp_pages",
    )(page_table, *([pool_t] * pages_per_step), w_cmp_t)


def _masked_softmax0(s, mask):
    s = jnp.where(mask, s, NEG)
    m = jnp.max(s, axis=0, keepdims=True)
    e = jnp.where(mask, jnp.exp2(s - m), 0.0)
    return e / jnp.maximum(jnp.sum(e, axis=0, keepdims=True), 1e-30)


def _topk_mask0(score, blk, n_sel):
    big = jnp.int32(2 ** 30)
    sel = jnp.zeros(score.shape, F32)
    for _ in range(n_sel):
        m = jnp.max(score, axis=0, keepdims=True)
        idx = jnp.min(jnp.where(score == m, blk, big), axis=0, keepdims=True)
        pick = blk == idx
        sel = jnp.where(pick, 1.0, sel)
        score = jnp.where(pick, -jnp.inf, score)
    return sel


def _sel_to_bias(sel):
    return (sel - 1.0) * (-NEG)


def _write_selection_bias(write, imp, blk, cur, n_valid_blk):
    n_sel = min(N_SEL, n_valid_blk)
    below = -FORCED_SCORE * (1.0 + blk.astype(F32) * (2.0 ** -10))
    score = jnp.where(blk <= cur, imp, below)
    score = jnp.where(blk == cur - 1, FORCED_SCORE, score)
    score = jnp.where(blk == cur, 2 * FORCED_SCORE, score)
    score = jnp.where(blk == 0, 3 * FORCED_SCORE, score)
    score = jnp.where(blk < n_valid_blk, score, -jnp.inf)

    sel = jnp.zeros(score.shape, F32)
    left = score
    for _ in range(n_sel):
        pick = left == jnp.max(left, axis=0, keepdims=True)
        sel = jnp.where(pick, 1.0, sel)
        left = jnp.where(pick, -jnp.inf, left)
    write(_sel_to_bias(sel))
    miscount = jnp.max(jnp.abs(jnp.sum(sel, axis=0, keepdims=True) - n_sel))

    @pl.when(miscount > 0.0)
    def _():
        write(_sel_to_bias(_topk_mask0(score, blk, n_sel)))


def _flash_update(s, bias, v_dot, m_ref, l_ref, acc_ref):
    if bias is not None:
        s = bias + s
    m_old = m_ref[...]
    m_new = jnp.maximum(m_old, jnp.max(s, axis=0, keepdims=True))
    alpha = jnp.exp2(m_old - m_new)
    p = jnp.exp2(s - m_new)
    if l_ref is not None:
        l_ref[...] = alpha * l_ref[...] + jnp.sum(p, axis=0, keepdims=True)
    acc_ref[...] = alpha * acc_ref[...] + v_dot(p.astype(BF16))
    m_ref[...] = m_new


def _flash_init(m_ref, l_ref, acc_ref):
    m_ref[...] = jnp.full(m_ref.shape, NEG, F32)
    if l_ref is not None:
        l_ref[...] = jnp.zeros(l_ref.shape, F32)
    acc_ref[...] = jnp.zeros(acc_ref.shape, F32)


def _attn_prompt_kernel(qt_ref, kc_ref, vct_ref, ks_ref, vst_ref, kw_ref, vwt_ref, g_ref,
                        o_ref, selb_ref, m_ref, acc_ref, out_ref, *, tq, tk):
    gi = pl.program_id(1)
    qi = pl.program_id(2)
    q0 = qi * tq
    n_blk = kc_ref.shape[2]
    wide = Q_PER_KV * tq
    qt = qt_ref[0, 0, 0]
    q_rows = qt[0:HEAD_DIM]
    pos = q0 + lax.broadcasted_iota(jnp.int32, (1, tq), 1)
    pos_w = q0 + (lax.broadcasted_iota(jnp.int32, (1, wide), 1) & (tq - 1))

    def gate(branch):
        rows = [g_ref[0, 0, pl.ds((gi * Q_PER_KV + r) * N_BRANCH + branch, 1), :] for r in range(Q_PER_KV)]
        return jnp.concatenate(rows, axis=1)

    blk_w = lax.broadcasted_iota(jnp.int32, (n_blk, wide), 0)
    p = _masked_softmax0(jnp.dot(kc_ref[0, 0], q_rows, preferred_element_type=F32),
                         blk_w * BLOCK + (BLOCK - 1) <= pos_w)
    o_c = jnp.dot(vct_ref[0, 0], p.astype(BF16), preferred_element_type=F32)
    out_ref[...] = gate(0) * o_c
    imp = p[:, 0:tq]
    for r in range(1, Q_PER_KV):
        imp = imp + p[:, r * tq:(r + 1) * tq]
    blk = lax.broadcasted_iota(jnp.int32, (n_blk, tq), 0)
    def write_selb(bias):
        for r in range(Q_PER_KV):
            selb_ref[:, r * tq:(r + 1) * tq] = bias

    _write_selection_bias(write_selb, imp, blk, pos >> BLOCK_SHIFT, n_blk)

    kpos_tile = lax.broadcasted_iota(jnp.int32, (tk, tq), 0)
    bpt = tk // BLOCK
    bias_pad = jnp.zeros((K_AUG - HEAD_DIM - bpt, wide), F32)

    cw = wide // COL_GROUPS
    q_lo = [(h * cw) % tq for h in range(COL_GROUPS)]

    groups = [slice(h * cw, (h + 1) * cw) for h in range(COL_GROUPS)]

    def tiles_update(k_ref, vt_ref, tiles):
        staged = []
        for kt, q_op, bias, key_rows in tiles:
            k_tile = k_ref[0, 0, pl.ds(pl.multiple_of(kt * tk, tk), tk), :]
            rows = [(0, tk) if key_rows is None else key_rows(h) for h in range(COL_GROUPS)]
            scores = [jnp.dot(k_tile[r0:r1], q_op[:, cols], preferred_element_type=F32)
                      for (r0, r1), cols in zip(rows, groups)]
            staged.append((vt_ref[0, 0, kt], rows, scores, bias))
        for vt, rows, scores, bias in staged:
            for h, (s, (r0, r1), cols) in enumerate(zip(scores, rows, groups)):
                group_bias = None if bias is None else bias[r0:r1, q_lo[h]:q_lo[h] + cw]
                v_dot = lambda pr, vt=vt, r0=r0, r1=r1: jnp.dot(vt[:, r0:r1], pr, preferred_element_type=F32)
                _flash_update(s, group_bias, v_dot, m_ref.at[:, cols], None, acc_ref.at[:, cols])

    def tile_update(k_ref, vt_ref, kt, q_op, bias, key_rows=None):
        tiles_update(k_ref, vt_ref, [(kt, q_op, bias, key_rows)])

    def finish(branch):
        o = acc_ref[0:HEAD_DIM, :] / jnp.maximum(acc_ref[HEAD_DIM:HEAD_DIM + 1, :], 1e-30)
        out_ref[...] = out_ref[...] + gate(branch) * o

    def q_with_block_bias(kt):
        tile_bias = selb_ref[pl.ds(pl.multiple_of(kt * bpt, bpt), bpt), :]
        return jnp.concatenate([q_rows, jnp.concatenate([tile_bias, bias_pad], axis=0).astype(BF16)], axis=0)

    kt_diag = (q0 + tq - 1) // tk
    _flash_init(m_ref, None, acc_ref)

    def sel_body(kt, carry):
        tile_update(ks_ref, vst_ref, kt, q_with_block_bias(kt), None)
        return carry

    lax.fori_loop(0, kt_diag, sel_body, 0)
    causal = jnp.where(kt_diag * tk + kpos_tile <= pos, 0.0, NEG)
    below_diag = lambda h: (0, q_lo[h] + cw)
    tile_update(ks_ref, vst_ref, kt_diag, q_with_block_bias(kt_diag), causal, below_diag)
    finish(1)

    _flash_init(m_ref, None, acc_ref)

    @pl.when(qi > 0)
    def _():
        older = jnp.where(pos - ((kt_diag - 1) * tk + kpos_tile) < WINDOW, 0.0, NEG)
        tile_update(kw_ref, vwt_ref, kt_diag - 1, qt, older, lambda h: (q_lo[h], tk))

    tile_update(kw_ref, vwt_ref, kt_diag, qt, causal, below_diag)
    finish(2)

    for pair in range(Q_PER_KV // 2):
        two = jnp.concatenate([out_ref[:, (2 * pair + h) * tq:(2 * pair + h + 1) * tq] for h in range(2)], axis=0)
        o_ref[:, pair * 2 * HEAD_DIM:(pair + 1) * 2 * HEAD_DIM] = two.T.astype(o_ref.dtype)


def _attn_prompt(qt, kc, vct, kas, vst, kaw, vwt, gates, tq, tk):
    b, g, nq, ka, wide = qt.shape
    t = nq * tq
    assert tq & (tq - 1) == 0 and wide == Q_PER_KV * tq and ka == K_AUG
    assert tq == tk == WINDOW and 2 * wide // COL_GROUPS <= tq
    n_blk = kc.shape[2]
    nkt = t // tk
    d = HEAD_DIM
    bg = lambda *tail: (lambda i, j, q: (i, j) + tail)
    kern = functools.partial(_attn_prompt_kernel, tq=tq, tk=tk)
    return pl.pallas_call(
        kern,
        grid=(b, g, nq),
        in_specs=[
            pl.BlockSpec((1, 1, 1, ka, wide), lambda i, j, q: (i, j, q, 0, 0)),
            pl.BlockSpec((1, 1, n_blk, d), bg(0, 0)),
            pl.BlockSpec((1, 1, d, n_blk), bg(0, 0)),
            pl.BlockSpec((1, 1, t, ka), bg(0, 0)),
            pl.BlockSpec((1, 1, nkt, V_AUG, tk), bg(0, 0, 0)),
            pl.BlockSpec((1, 1, t, ka), bg(0, 0)),
            pl.BlockSpec((1, 1, nkt, V_AUG, tk), bg(0, 0, 0)),
            pl.BlockSpec((1, 1, gates.shape[2], tq), lambda i, j, q: (i, q, 0, 0)),
        ],
        out_specs=pl.BlockSpec((tq, Q_PER_KV * d), lambda i, j, q: (i * nq + q, j)),
        out_shape=jax.ShapeDtypeStruct((b * t, ATTN_WIDTH), BF16),
        scratch_shapes=[
            pltpu.VMEM((n_blk, wide), F32),
            pltpu.VMEM((1, wide), F32),
            pltpu.VMEM((V_AUG, wide), F32),
            pltpu.VMEM((d, wide), F32),
        ],
        compiler_params=_cparams(("parallel", "parallel", "parallel")),
        name="attn_prompt",
    )(qt, kc, vct, kas, vst, kaw, vwt, gates)


def _attn_sample_kernel(pt_ref, *refs, pages_per_step, past_len, n_blk_valid):
    page_refs = refs[:pages_per_step]
    (qbd_ref, kc_ref, vc_ref, new_s_ref, win_ref, new_w_ref, gt_ref, pos_ref,
     o_ref, selb_ref, m_ref, l_ref, acc_ref, out_ref) = refs[pages_per_step:]
    step = pl.program_id(1)
    n_steps = pl.num_programs(1)
    ncol = qbd_ref.shape[2]
    kvw = qbd_ref.shape[1]
    qbd = qbd_ref[0]
    pos = pos_ref[...]
    group_cols = ncol // Q_PER_KV

    def tile_rows(rows, bias):
        k, v = rows[:, :kvw].astype(BF16), rows[:, kvw:].astype(BF16)
        s = jnp.dot(k, qbd, preferred_element_type=F32)
        v_dot = lambda p: lax.dot_general(v, p, TN_DIMS, preferred_element_type=F32)
        _flash_update(s, bias, v_dot, m_ref, l_ref, acc_ref)

    def tile_chan(page, bias):
        kt, vt = page[:kvw].astype(BF16), page[kvw:].astype(BF16)
        half = page.shape[1] // 2
        s = jnp.concatenate([lax.dot_general(kt[:, :half], qbd, TN_DIMS, preferred_element_type=F32),
                             lax.dot_general(kt[:, half:], qbd, TN_DIMS, preferred_element_type=F32)], axis=0)
        v_dot = lambda p: (jnp.dot(vt[:, :half], p[:half], preferred_element_type=F32)
                           + jnp.dot(vt[:, half:], p[half:], preferred_element_type=F32))
        _flash_update(s, bias, v_dot, m_ref, l_ref, acc_ref)

    def window_bias(n_keys, p0):
        delta = pos - (p0 + lax.broadcasted_iota(jnp.int32, (n_keys, ncol), 0))
        return jnp.where((delta >= 0) & (delta < WINDOW), 0.0, NEG)

    @pl.when(step == 0)
    def _():
        n_blk = kc_ref.shape[1]
        blk = lax.broadcasted_iota(jnp.int32, (n_blk, ncol), 0)
        cmask = (blk * BLOCK + (BLOCK - 1) <= pos) & (blk < n_blk_valid)
        s = jnp.dot(kc_ref[0], qbd, preferred_element_type=F32)
        p = _masked_softmax0(s, cmask)
        o_c = lax.dot_general(vc_ref[0], p.astype(BF16), TN_DIMS, preferred_element_type=F32)
        out_ref[...] = gt_ref[0, 0:1, :] * o_c
        imp = p
        for r in range(1, Q_PER_KV):
            imp = imp + pltpu.roll(p, r * group_cols, 1)
        def write_selb(bias):
            selb_ref[...] = bias

        _write_selection_bias(write_selb, imp, blk, pos >> BLOCK_SHIFT, n_blk_valid)

        _flash_init(m_ref, l_ref, acc_ref)
        n_win = win_ref.shape[2]
        tile_rows(new_w_ref[0], window_bias(new_w_ref.shape[1], past_len))
        tile_chan(win_ref[0], window_bias(n_win, past_len - n_win))
        o_w = acc_ref[...] / jnp.maximum(l_ref[...], 1e-30)
        out_ref[...] = out_ref[...] + gt_ref[0, 2:3, :] * o_w

        _flash_init(m_ref, l_ref, acc_ref)
        n_new = new_s_ref.shape[1]
        kpos = past_len + lax.broadcasted_iota(jnp.int32, (n_new, ncol), 0)
        chosen = jnp.broadcast_to(selb_ref[pl.ds(past_len // BLOCK, 1), :], (n_new, ncol))
        tile_rows(new_s_ref[0], jnp.where(kpos <= pos, chosen, NEG))

    page_rows = page_refs[0].shape[2]
    step_keys = pages_per_step * page_rows
    step_blocks = step_keys // BLOCK
    keys = jnp.concatenate([page_refs[j][0] for j in range(pages_per_step)], axis=1)
    chosen = jnp.concatenate(
        [jnp.broadcast_to(selb_ref[pl.ds(step * step_blocks + i, 1), :], (BLOCK, ncol)) for i in range(step_blocks)],
        axis=0)
    tile_chan(keys, chosen)

    @pl.when(step == n_steps - 1)
    def _():
        o_s = acc_ref[...] / jnp.maximum(l_ref[...], 1e-30)
        o_ref[0] = out_ref[...] + gt_ref[0, 1:2, :] * o_s


def _attn_sample(pool_t, page0, page_table, qbd, kc, vc, new_s, win_t, win0, new_w, gt, pos,
                 pages_per_step=PAGES_PER_STEP):
    _, c, page_rows = pool_t.shape
    b, n_pages = page_table.shape
    pages_per_step = min(pages_per_step, n_pages)
    _, kvw, ncol = qbd.shape
    n_blk = kc.shape[1]
    past_len = n_pages * page_rows
    per_b = lambda shape: pl.BlockSpec((1,) + shape, lambda i, s, pt: (i,) + (0,) * len(shape))
    grid_spec = pltpu.PrefetchScalarGridSpec(
        num_scalar_prefetch=1,
        grid=(b, n_pages // pages_per_step),
        in_specs=_page_specs(c, page_rows, pages_per_step, page0) + [
            per_b((kvw, ncol)),
            per_b((n_blk, kvw)),
            per_b((n_blk, kvw)),
            per_b(new_s.shape[1:]),
            pl.BlockSpec((1,) + win_t.shape[1:], lambda i, s, pt: (win0 + i, 0, 0)),
            per_b(new_w.shape[1:]),
            per_b((N_BRANCH, ncol)),
            pl.BlockSpec((1, ncol), lambda i, s, pt: (0, 0)),
        ],
        out_specs=per_b((kvw, ncol)),
        scratch_shapes=[
            pltpu.VMEM((n_blk, ncol), F32),
            pltpu.VMEM((1, ncol), F32),
            pltpu.VMEM((1, ncol), F32),
            pltpu.VMEM((kvw, ncol), F32),
            pltpu.VMEM((kvw, ncol), F32),
        ],
    )
    kern = functools.partial(_attn_sample_kernel, pages_per_step=pages_per_step, past_len=past_len,
                             n_blk_valid=past_len // BLOCK + 1)
    return pl.pallas_call(
        kern,
        grid_spec=grid_spec,
        out_shape=jax.ShapeDtypeStruct((b, kvw, ncol), F32),
        compiler_params=_cparams(("parallel", "arbitrary")),
        name="attn_sample",
    )(page_table, *([pool_t] * pages_per_step), qbd, kc, vc, new_s, win_t, new_w, gt, pos)


def _outproj_kernel(a_ref, c_ref, x_ref, w_ref, g_ref, b_ref, h_ref, *, alpha):
    aw = a_ref.shape[1]
    rows = a_ref.shape[0]
    part = rows // OUTPROJ_ROW_PARTS if rows % (OUTPROJ_ROW_PARTS * OUTPROJ_MIN_PART) == 0 else rows
    for r0 in range(0, rows, part):
        rs = slice(r0, r0 + part)
        mix = jnp.dot(a_ref[rs, :], w_ref[:aw, :], preferred_element_type=F32)
        mix = mix + jnp.dot(c_ref[rs, :], w_ref[aw:, :], preferred_element_type=F32)
        h_ref[rs, :] = _layer_norm(alpha * x_ref[rs, :] + mix, g_ref[...], b_ref[...])


def _outproj_ln(attn, conv, x, w_out, layer, g, b, alpha, tm):
    n, d = x.shape
    row = lambda w: pl.BlockSpec((tm, w), lambda i: (i, 0))
    const = lambda shape: pl.BlockSpec(shape, lambda i: (0, 0))
    w_spec = pl.BlockSpec((None,) + w_out.shape[1:], lambda i: (layer, 0, 0))
    return pl.pallas_call(
        functools.partial(_outproj_kernel, alpha=alpha),
        grid=(n // tm,),
        in_specs=[row(attn.shape[1]), row(conv.shape[1]), row(d), w_spec, const((1, d)), const((1, d))],
        out_specs=row(d),
        out_shape=jax.ShapeDtypeStruct((n, d), F32),
        compiler_params=_cparams(("parallel",)),
        name="outproj_ln",
    )(attn, conv, x, w_out, g, b)


def _mlp_kernel(h_ref, w1_ref, w2_ref, g_ref, b_ref, y_ref, acc_ref, hb_ref, *, alpha):
    f = pl.program_id(1)

    @pl.when(f == 0)
    def _():
        hb_ref[...] = h_ref[...].astype(BF16)
        acc_ref[...] = jnp.zeros(acc_ref.shape, F32)

    a = jnp.dot(hb_ref[...], w1_ref[...], preferred_element_type=F32)
    a = jnp.square(jnp.maximum(a, 0.0)).astype(BF16)
    acc_ref[...] += jnp.dot(a, w2_ref[...], preferred_element_type=F32)

    @pl.when(f == pl.num_programs(1) - 1)
    def _():
        y_ref[...] = _layer_norm(alpha * h_ref[...] + acc_ref[...], g_ref[...], b_ref[...])


def _mlp_ln(h, w1, w2, layer, g, b, alpha, tm, tf):
    n, d = h.shape
    dff = w1.shape[2]
    return pl.pallas_call(
        functools.partial(_mlp_kernel, alpha=alpha),
        grid=(n // tm, dff // tf),
        in_specs=[
            pl.BlockSpec((tm, d), lambda i, f: (i, 0)),
            pl.BlockSpec((None, d, tf), lambda i, f: (layer, 0, f)),
            pl.BlockSpec((None, tf, d), lambda i, f: (layer, f, 0)),
            pl.BlockSpec((1, d), lambda i, f: (0, 0)),
            pl.BlockSpec((1, d), lambda i, f: (0, 0)),
        ],
        out_specs=pl.BlockSpec((tm, d), lambda i, f: (i, 0)),
        out_shape=jax.ShapeDtypeStruct((n, d), F32),
        scratch_shapes=[pltpu.VMEM((tm, d), F32), pltpu.VMEM((tm, d), BF16)],
        compiler_params=_cparams(("parallel", "arbitrary")),
        name="mlp_ln",
    )(h, w1, w2, g, b)


def _split_kv(kv, b, t):
    return kv.reshape(b, t, 2, KV_HEADS, HEAD_DIM)


def _prompt_layer(x, p, b, t, alpha, tm, tq, tk):
    w_c = jnp.tile(p["w_cmp"].T, (1, tm // BLOCK))
    kvt_c, kvt_s, kvt_w, vst, vwt, kas, kaw, qt, gates, cmp, conv, zlast = _proj_prompt(
        x, p["w_in_t"], p["layer"], w_c, p["conv_w"], b, t, tm, tq, tk)
    n_blk = t // BLOCK
    cmp = cmp.reshape(b, n_blk, 2, KV_HEADS, HEAD_DIM).astype(BF16)
    kc = cmp[:, :, 0].transpose(0, 2, 1, 3)
    vct = cmp[:, :, 1].transpose(0, 2, 3, 1)
    attn = _attn_prompt(qt, kc, vct, kas, vst, kaw, vwt, gates, tq, tk)
    h = _outproj_ln(attn, conv, x, p["w_out"], p["layer"], p["ln1_g"], p["ln1_b"], alpha, tm)
    y = _mlp_ln(h, p["w_mlp1"], p["w_mlp2"], p["layer"], p["ln2_g"], p["ln2_b"], alpha, min(MLP_ROWS, b * t), MLP_COLS)
    conv_state = zlast.reshape(b, t // tm, SUBLANES, -1)[:, -1, SUBLANES - (CONV_W - 1):]
    keep = min(WINDOW, t)
    rows_major = lambda kvt: kvt.reshape(b, 2, KV_HEADS, HEAD_DIM, t).transpose(0, 4, 1, 2, 3)
    return (y, rows_major(kvt_c), rows_major(kvt_s), rows_major(kvt_w)[:, t - keep:], conv_state)


def _sample_layer(x, p, pool_c, pool_s, page0, win_t, win0, cache_win, state_conv, page_table, db, dt, alpha):
    n = db * dt
    page_rows = pool_c.shape[2]
    n_pages = page_table.shape[1]
    past_len = n_pages * page_rows
    c = 2 * KV_WIDTH
    zp = jnp.pad(state_conv, ((0, 0), (SUBLANES - (CONV_W - 1), 0), (0, 0))).reshape(n, -1)
    q, kv_c, kv_s, kv_w, gates, conv, z = _proj_sample(x, zp, p["w_in_t"], p["layer"], p["conv_w"])
    conv_state = jnp.concatenate([state_conv, z.reshape(db, dt, -1)], axis=1)[:, -(CONV_W - 1):]

    cmp_past = _cmp_pages(pool_c, page0, page_table, p["w_cmp_t"])
    tail = jnp.pad(kv_c.reshape(db, dt, c), ((0, 0), (0, SUBLANES * BLOCK - dt), (0, 0)))
    cmp_tail = _cmp_rows(tail, p["w_cmp"], SUBLANES)
    cmp_all = jnp.concatenate([cmp_past, cmp_tail], axis=1).astype(BF16)
    kc, vc = cmp_all[..., :KV_WIDTH], cmp_all[..., KV_WIDTH:]

    q5 = q.reshape(db, dt, KV_HEADS, Q_PER_KV, HEAD_DIM).transpose(0, 2, 4, 3, 1)
    eye = jnp.eye(KV_HEADS, dtype=q.dtype)
    qbd = (q5[:, :, :, :, None, :] * eye[None, :, None, None, :, None]).reshape(db, KV_WIDTH, N_HEADS * dt)
    gt = gates[:, :N_HEADS * N_BRANCH].reshape(db, dt, KV_HEADS, Q_PER_KV, N_BRANCH)
    gt = gt.transpose(0, 4, 3, 2, 1).reshape(db, N_BRANCH, N_HEADS * dt)
    pos = jnp.tile(past_len + jnp.arange(dt, dtype=jnp.int32), N_HEADS).reshape(1, N_HEADS * dt)
    new_rows = lambda kv: jnp.pad(kv.reshape(db, dt, c), ((0, 0), (0, 2 * SUBLANES - dt), (0, 0)))
    o = _attn_sample(pool_s, page0, page_table, qbd, kc, vc, new_rows(kv_s), win_t, win0, new_rows(kv_w), gt, pos)
    o = o.reshape(db, KV_HEADS, HEAD_DIM, Q_PER_KV, KV_HEADS, dt)
    o = jnp.stack([o[:, g, :, :, g, :] for g in range(KV_HEADS)], axis=1)
    attn = o.transpose(0, 4, 1, 3, 2).reshape(n, ATTN_WIDTH).astype(BF16)

    h = _outproj_ln(attn, conv, x, p["w_out"], p["layer"], p["ln1_g"], p["ln1_b"], alpha, n)
    y = _mlp_ln(h, p["w_mlp1"], p["w_mlp2"], p["layer"], p["ln2_g"], p["ln2_b"], alpha, n, 1024)
    win_keep = cache_win.shape[1]
    new_win = jnp.concatenate([cache_win, _split_kv(kv_w, db, dt)], axis=1)[:, -win_keep:]
    return y, _split_kv(kv_c, db, dt), _split_kv(kv_s, db, dt), new_win, conv_state


def kernel(x_prompt, x_sample, cache_cmp, cache_slc, cache_win, state_conv, page_table, w_in, w_cmp_k, w_cmp_v,
           conv_w, w_out, ln1_g, ln1_b, w_mlp1, w_mlp2, ln2_g, ln2_b):
    depth = w_in.shape[0]
    b, t, d = x_prompt.shape
    db, dt, _ = x_sample.shape
    alpha = (2.0 * depth) ** 0.25
    tm = min(512, t)
    tq, tk = min(512, t), min(512, t)

    n_pool, page_rows = cache_cmp.shape[1:3]
    chan_major = lambda c: c.transpose(0, 1, 3, 4, 5, 2).reshape(c.shape[0] * c.shape[1], -1, c.shape[2])
    pool_c, pool_s, win_t = chan_major(cache_cmp), chan_major(cache_slc), chan_major(cache_win)

    w_out_b, w_mlp1_b, w_mlp2_b = w_out.astype(BF16), w_mlp1.astype(BF16), w_mlp2.astype(BF16)
    yp = x_prompt.reshape(b * t, d)
    ys = x_sample.reshape(db * dt, d)
    outs = [[] for _ in range(8)]
    w_in_t = w_in.transpose(0, 2, 1).astype(BF16)
    for l in range(depth):
        w_cmp = jnp.concatenate([jnp.tile(w_cmp_k[l], (1, KV_HEADS)), jnp.tile(w_cmp_v[l], (1, KV_HEADS))], axis=1)
        p = {
            "w_in_t": w_in_t,
            "w_cmp": w_cmp,
            "w_cmp_t": jnp.tile(w_cmp.T, (1, page_rows // BLOCK)),
            "conv_w": conv_w[l],
            "layer": l,
            "w_out": w_out_b,
            "ln1_g": ln1_g[l].reshape(1, d), "ln1_b": ln1_b[l].reshape(1, d),
            "w_mlp1": w_mlp1_b, "w_mlp2": w_mlp2_b,
            "ln2_g": ln2_g[l].reshape(1, d), "ln2_b": ln2_b[l].reshape(1, d),
        }
        yp, c1, s1, w1, v1 = _prompt_layer(yp, p, b, t, alpha, tm, tq, tk)
        ys, c2, s2, w2, v2 = _sample_layer(ys, p, pool_c, pool_s, l * n_pool, win_t, l * db, cache_win[l],
                                           state_conv[l], page_table, db, dt, alpha)
        for lst, v in zip(outs, (c1, s1, w1, v1, c2, s2, w2, v2)):
            lst.append(v)
    stacked = [jnp.stack(o) for o in outs]
    return (yp.reshape(b, t, d), ys.reshape(db, dt, d), *stacked)
```

```python
import functools
import math

import jax
import jax.numpy as jnp
from jax import lax
from jax.experimental import pallas as pl
from jax.experimental.pallas import tpu as pltpu

HEAD_DIM = 64
KV_HEADS = 4
Q_PER_KV = 4
N_HEADS = KV_HEADS * Q_PER_KV
ATTN_WIDTH = N_HEADS * HEAD_DIM
KV_WIDTH = KV_HEADS * HEAD_DIM
N_BRANCH = 3
BLOCK = 64
BLOCK_SHIFT = BLOCK.bit_length() - 1
N_SEL = 16
WINDOW = 512
CONV_W = 3
LN_EPS = 1e-5
NEG = -1e30
FORCED_SCORE = 1e9
Q_SCALE = HEAD_DIM ** -0.5 * math.log2(math.e)
GATE_PAD = 128

SUBLANES = 8
VMEM_LIMIT = 56 * 1024 * 1024

BF16 = jnp.bfloat16
F32 = jnp.float32
NT_DIMS = (((1,), (1,)), ((), ()))
TN_DIMS = (((0,), (0,)), ((), ()))

ROW_Q = 0
ROW_KV = ATTN_WIDTH
ROW_GATE = ROW_KV + 6 * KV_WIDTH
ROW_CONV = ROW_GATE + N_HEADS * N_BRANCH


def _w_in_spec(w_all, layer):
    return pl.BlockSpec((None,) + w_all.shape[1:], lambda *_: (layer, 0, 0), pipeline_mode=pl.Buffered(1))


def _cparams(sem):
    return pltpu.CompilerParams(dimension_semantics=sem, vmem_limit_bytes=VMEM_LIMIT)


def _layer_norm(v, g, b):
    mu = jnp.mean(v, axis=-1, keepdims=True)
    d = v - mu
    var = jnp.mean(d * d, axis=-1, keepdims=True)
    return d * lax.rsqrt(var + LN_EPS) * g + b


def _proj_kernel(x_ref, w_ref, q_ref, kc_ref, ks_ref, kw_ref, g_ref):
    x = x_ref[...].astype(BF16)

    def mm(r0, r1):
        return lax.dot_general(x, w_ref[r0:r1, :], NT_DIMS, preferred_element_type=F32)

    kv2 = 2 * KV_WIDTH
    q_ref[...] = (mm(ROW_Q, ROW_KV) * Q_SCALE).astype(BF16)
    kc_ref[...] = mm(ROW_KV, ROW_KV + kv2)
    ks_ref[...] = mm(ROW_KV + kv2, ROW_KV + 2 * kv2)
    kw_ref[...] = mm(ROW_KV + 2 * kv2, ROW_KV + 3 * kv2)
    logits = mm(ROW_GATE, ROW_GATE + GATE_PAD)
    g_ref[...] = 1.0 / (1.0 + jnp.exp(-logits))


K_AUG = 2 * HEAD_DIM
V_AUG = HEAD_DIM + 16
MLP_ROWS, MLP_COLS = 512, 1024
OUTPROJ_ROW_PARTS = 4
OUTPROJ_MIN_PART = 128
PAGES_PER_STEP = 32
COL_GROUPS = 8


def _block_sums_t(kvt, w_t, n_out=None, blk0=0):
    rows = kvt.shape[1]
    nb = rows // BLOCK if n_out is None else n_out
    pw = (kvt * w_t).astype(BF16)
    member = jnp.where(lax.broadcasted_iota(jnp.int32, (nb, rows), 0)
                       == blk0 + (lax.broadcasted_iota(jnp.int32, (nb, rows), 1) >> BLOCK_SHIFT),
                       1.0, 0.0).astype(BF16)
    return lax.dot_general(member, pw, NT_DIMS, preferred_element_type=F32)


def _proj_prompt_kernel(x_ref, wt_ref, wc_ref, cw_ref, kvc_ref, kvs_ref, kvw_ref, vts_ref, vtw_ref,
                        kas_ref, kaw_ref, qt_ref, g_ref, cmp_ref, conv_ref, zl_ref, carry_ref, *, tq, tk):
    tm = x_ref.shape[0]
    x = x_ref[...].astype(BF16)
    _conv_prompt_tile(x, wt_ref, cw_ref, conv_ref, zl_ref, carry_ref)

    def nt(r0, r1):
        return lax.dot_general(wt_ref[r0:r1, :], x, NT_DIMS, preferred_element_type=F32)

    c = ROW_KV
    kv2 = 2 * KV_WIDTH
    qall = (nt(ROW_Q, ROW_KV) * Q_SCALE).astype(BF16)
    pad_rows = jnp.zeros((K_AUG - HEAD_DIM, Q_PER_KV * tq), BF16)
    for g in range(KV_HEADS):
        for j in range(tm // tq):
            heads = [qall[(g * Q_PER_KV + r) * HEAD_DIM:(g * Q_PER_KV + r + 1) * HEAD_DIM, j * tq:(j + 1) * tq]
                     for r in range(Q_PER_KV)]
            qt_ref[0, g, j, 0:HEAD_DIM, :] = jnp.concatenate(heads, axis=1)
            qt_ref[0, g, j, HEAD_DIM:K_AUG, :] = pad_rows

    kvt_c = nt(c, c + kv2)
    kvc_ref[0] = kvt_c
    cmp_ref[0] = _block_sums_t(kvt_c, wc_ref[...])

    hot_row = lax.broadcasted_iota(jnp.int32, (K_AUG - HEAD_DIM, tm), 0)
    row_blk = (lax.broadcasted_iota(jnp.int32, (K_AUG - HEAD_DIM, tm), 1) >> BLOCK_SHIFT) & (tk // BLOCK - 1)
    one_hot_t = jnp.where(hot_row == row_blk, 1.0, 0.0)
    sum_rows = jnp.where(lax.broadcasted_iota(jnp.int32, (V_AUG - HEAD_DIM, tk), 0) == 0, 1.0, 0.0).astype(BF16)
    for i, (kv_ref, vt_ref, ka_ref) in enumerate(((kvs_ref, vts_ref, kas_ref), (kvw_ref, vtw_ref, kaw_ref))):
        kvt = nt(c + (i + 1) * kv2, c + (i + 2) * kv2)
        kv_ref[0] = kvt
        for g in range(KV_HEADS):
            v_rows = kvt[KV_WIDTH + g * HEAD_DIM:KV_WIDTH + (g + 1) * HEAD_DIM]
            for jk in range(tm // tk):
                vt_ref[0, g, jk, 0:HEAD_DIM, :] = v_rows[:, jk * tk:(jk + 1) * tk].astype(BF16)
                vt_ref[0, g, jk, HEAD_DIM:V_AUG, :] = sum_rows
            k_t = kvt[g * HEAD_DIM:(g + 1) * HEAD_DIM]
            ka_ref[0, g] = jnp.concatenate([k_t, one_hot_t], axis=0).T.astype(BF16)

    logits = nt(ROW_GATE, ROW_GATE + g_ref.shape[2])
    gates = 1.0 / (1.0 + jnp.exp(-logits))
    for j in range(tm // tq):
        g_ref[0, j] = gates[:, j * tq:(j + 1) * tq]


def _proj_prompt(x, w_all, layer, w_c, cw, b, t, tm, tq, tk):
    n, d = x.shape
    cdim = cw.shape[1]
    tiles = t // tm
    g_rows = 64
    const = lambda a: pl.BlockSpec(a.shape, lambda i, j: (0,) * a.ndim, pipeline_mode=pl.Buffered(1))
    kvt_spec = pl.BlockSpec((1, 2 * KV_WIDTH, tm), lambda i, j: (i, 0, j))
    vt_spec = pl.BlockSpec((1, KV_HEADS, tm // tk, V_AUG, tk), lambda i, j: (i, 0, j, 0, 0))
    ka_spec = pl.BlockSpec((1, KV_HEADS, tm, K_AUG), lambda i, j: (i, 0, j, 0))
    kvt_shape = jax.ShapeDtypeStruct((b, 2 * KV_WIDTH, t), F32)
    vt_shape = jax.ShapeDtypeStruct((b, KV_HEADS, t // tk, V_AUG, tk), BF16)
    ka_shape = jax.ShapeDtypeStruct((b, KV_HEADS, t, K_AUG), BF16)
    wide = Q_PER_KV * tq
    return pl.pallas_call(
        functools.partial(_proj_prompt_kernel, tq=tq, tk=tk),
        grid=(b, tiles),
        in_specs=[pl.BlockSpec((tm, d), lambda i, j: (i * tiles + j, 0)), _w_in_spec(w_all, layer), const(w_c),
                  const(cw)],
        out_specs=[
            kvt_spec, kvt_spec, kvt_spec, vt_spec, vt_spec, ka_spec, ka_spec,
            pl.BlockSpec((1, KV_HEADS, tm // tq, K_AUG, wide), lambda i, j: (i, 0, j, 0, 0)),
            pl.BlockSpec((1, tm // tq, g_rows, tq), lambda i, j: (i, j, 0, 0)),
            pl.BlockSpec((1, tm // BLOCK, 2 * KV_WIDTH), lambda i, j: (i, j, 0)),
            pl.BlockSpec((tm, cdim), lambda i, j: (i * tiles + j, 0)),
            pl.BlockSpec((SUBLANES, cdim), lambda i, j: (i * tiles + j, 0)),
        ],
        out_shape=[
            kvt_shape, kvt_shape, kvt_shape, vt_shape, vt_shape, ka_shape, ka_shape,
            jax.ShapeDtypeStruct((b, KV_HEADS, t // tq, K_AUG, wide), BF16),
            jax.ShapeDtypeStruct((b, t // tq, g_rows, tq), F32),
            jax.ShapeDtypeStruct((b, t // BLOCK, 2 * KV_WIDTH), F32),
            jax.ShapeDtypeStruct((n, cdim), BF16),
            jax.ShapeDtypeStruct((n // tm * SUBLANES, cdim), F32),
        ],
        scratch_shapes=[pltpu.VMEM((SUBLANES, cdim), F32)],
        compiler_params=_cparams(("parallel", "arbitrary")),
        name="proj_prompt",
    )(x, w_all, w_c, cw)


def _conv_from_z(z, zp, bgate, cw_ref, c0, c1):
    rows = z.shape[0]
    sub = lax.broadcasted_iota(jnp.int32, z.shape, 0) & (SUBLANES - 1)
    z1 = jnp.where(sub >= 1, pltpu.roll(z, 1, 0), pltpu.roll(zp, rows - (SUBLANES - 1), 0))
    z2 = jnp.where(sub >= 2, pltpu.roll(z, 2, 0), pltpu.roll(zp, rows - (SUBLANES - 2), 0))
    w0 = cw_ref[0:1, c0:c1]
    w1 = cw_ref[1:2, c0:c1]
    w2 = cw_ref[2:3, c0:c1]
    return bgate * (w2 * z + w0 * z2 + w1 * z1)


def _conv_prompt_tile(x, w_ref, cw_ref, y_ref, zl_ref, carry_ref, chunk=512):
    cdim = y_ref.shape[1]

    @pl.when(pl.program_id(1) == 0)
    def _():
        carry_ref[...] = jnp.zeros(carry_ref.shape, F32)

    for c0 in range(0, cdim, chunk):
        c1 = c0 + chunk

        def mm(off):
            rows = w_ref[ROW_CONV + off + c0:ROW_CONV + off + c1, :]
            return lax.dot_general(x, rows, NT_DIMS, preferred_element_type=F32)

        z = mm(2 * cdim) * mm(0)
        bgate = mm(cdim)
        zp = jnp.concatenate([carry_ref[:, c0:c1], z[:-SUBLANES]], axis=0)
        y_ref[:, c0:c1] = _conv_from_z(z, zp, bgate, cw_ref, c0, c1).astype(BF16)
        carry_ref[:, c0:c1] = z[-SUBLANES:]
        zl_ref[:, c0:c1] = z[-SUBLANES:]


def _conv_sample_kernel(x_ref, zp_ref, w_ref, cw_ref, y_ref, z_ref, *, chunk):
    cdim = y_ref.shape[1]
    x = x_ref[...].astype(BF16)
    for c0 in range(0, cdim, chunk):
        c1 = c0 + chunk

        def mm(off):
            rows = w_ref[ROW_CONV + off + c0:ROW_CONV + off + c1, :]
            return lax.dot_general(x, rows, NT_DIMS, preferred_element_type=F32)

        z = mm(2 * cdim) * mm(0)
        bgate = mm(cdim)
        y_ref[:, c0:c1] = _conv_from_z(z, zp_ref[:, c0:c1], bgate, cw_ref, c0, c1).astype(BF16)
        z_ref[:, c0:c1] = z


def _proj_sample_kernel(x_ref, zp_ref, w_ref, cw_ref, q_ref, kc_ref, ks_ref, kw_ref, g_ref, y_ref, z_ref, *, chunk):
    _proj_kernel(x_ref, w_ref, q_ref, kc_ref, ks_ref, kw_ref, g_ref)
    _conv_sample_kernel(x_ref, zp_ref, w_ref, cw_ref, y_ref, z_ref, chunk=chunk)


def _proj_sample(x, zp, w_all, layer, cw, chunk=512):
    n, d = x.shape
    cdim = cw.shape[1]
    full = lambda shape: pl.BlockSpec(shape, lambda i: (0, 0))
    widths = (ATTN_WIDTH, 2 * KV_WIDTH, 2 * KV_WIDTH, 2 * KV_WIDTH, GATE_PAD, cdim, cdim)
    dtypes = (BF16, F32, F32, F32, F32, BF16, F32)
    return pl.pallas_call(
        functools.partial(_proj_sample_kernel, chunk=chunk),
        grid=(1,),
        in_specs=[full((n, d)), full((n, cdim)), _w_in_spec(w_all, layer), full((CONV_W, cdim))],
        out_specs=[full((n, w)) for w in widths],
        out_shape=[jax.ShapeDtypeStruct((n, w), dt) for w, dt in zip(widths, dtypes)],
        compiler_params=_cparams(("arbitrary",)),
        name="proj_sample",
    )(x, zp, w_all, cw)


def _cmp_rows_kernel(kv_ref, w_ref, o_ref):
    nb = o_ref.shape[1]
    x = kv_ref[0].reshape(nb, BLOCK, kv_ref.shape[2])
    o_ref[0] = jnp.sum(x * w_ref[...][None], axis=1)


def _cmp_rows(kv, w_cmp, nb):
    b, l, c = kv.shape
    return pl.pallas_call(
        _cmp_rows_kernel,
        grid=(b, l // (nb * BLOCK)),
        in_specs=[
            pl.BlockSpec((1, nb * BLOCK, c), lambda i, j: (i, j, 0)),
            pl.BlockSpec((BLOCK, c), lambda i, j: (0, 0)),
        ],
        out_specs=pl.BlockSpec((1, nb, c), lambda i, j: (i, j, 0)),
        out_shape=jax.ShapeDtypeStruct((b, l // BLOCK, c), F32),
        compiler_params=_cparams(("parallel", "parallel")),
        name="cmp_rows",
    )(kv, w_cmp)


def _cmp_pages_kernel(pt_ref, *refs, pages_per_step):
    page_refs = refs[:pages_per_step]
    w_ref = refs[pages_per_step]
    o_ref = refs[pages_per_step + 1]
    w = w_ref[...]
    nb = o_ref.shape[1]
    bpp = w.shape[1] // BLOCK
    acc = jnp.zeros(o_ref.shape[1:], F32)
    for j in range(pages_per_step):
        acc = acc + _block_sums_t(page_refs[j][0], w, nb, j * bpp)
    o_ref[0] = acc


def _page_specs(c, page_rows, pages_per_step, page0):
    def spec(j):
        return pl.BlockSpec((1, c, page_rows), lambda b, s, pt: (page0 + pt[b, s * pages_per_step + j], 0, 0))
    return [spec(j) for j in range(pages_per_step)]


def _cmp_pages(pool_t, page0, page_table, w_cmp_t, pages_per_step=PAGES_PER_STEP):
    _, c, page_rows = pool_t.shape
    b, n_pages = page_table.shape
    pages_per_step = min(pages_per_step, n_pages)
    nb = pages_per_step * page_rows // BLOCK
    grid_spec = pltpu.PrefetchScalarGridSpec(
        num_scalar_prefetch=1,
        grid=(b, n_pages // pages_per_step),
        in_specs=_page_specs(c, page_rows, pages_per_step, page0)
        + [pl.BlockSpec((c, page_rows), lambda i, s, pt: (0, 0))],
        out_specs=pl.BlockSpec((1, nb, c), lambda i, s, pt: (i, s, 0)),
    )
    return pl.pallas_call(
        functools.partial(_cmp_pages_kernel, pages_per_step=pages_per_step),
        grid_spec=grid_spec,
        out_shape=jax.ShapeDtypeStruct((b, n_pages * page_rows // BLOCK, c), F32),
        compiler_params=_cparams(("parallel", "parallel")),
        name="cmp_pages",
    )(page_table, *([pool_t] * pages_per_step), w_cmp_t)


def _masked_softmax0(s, mask):
    s = jnp.where(mask, s, NEG)
    m = jnp.max(s, axis=0, keepdims=True)
    e = jnp.where(mask, jnp.exp2(s - m), 0.0)
    return e / jnp.maximum(jnp.sum(e, axis=0, keepdims=True), 1e-30)


def _topk_mask0(score, blk, n_sel):
    big = jnp.int32(2 ** 30)
    sel = jnp.zeros(score.shape, F32)
    for _ in range(n_sel):
        m = jnp.max(score, axis=0, keepdims=True)
        idx = jnp.min(jnp.where(score == m, blk, big), axis=0, keepdims=True)
        pick = blk == idx
        sel = jnp.where(pick, 1.0, sel)
        score = jnp.where(pick, -jnp.inf, score)
    return sel


def _sel_to_bias(sel):
    return (sel - 1.0) * (-NEG)


def _write_selection_bias(write, imp, blk, cur, n_valid_blk):
    n_sel = min(N_SEL, n_valid_blk)
    below = -FORCED_SCORE * (1.0 + blk.astype(F32) * (2.0 ** -10))
    score = jnp.where(blk <= cur, imp, below)
    score = jnp.where(blk == cur - 1, FORCED_SCORE, score)
    score = jnp.where(blk == cur, 2 * FORCED_SCORE, score)
    score = jnp.where(blk == 0, 3 * FORCED_SCORE, score)
    score = jnp.where(blk < n_valid_blk, score, -jnp.inf)

    sel = jnp.zeros(score.shape, F32)
    left = score
    for _ in range(n_sel):
        pick = left == jnp.max(left, axis=0, keepdims=True)
        sel = jnp.where(pick, 1.0, sel)
        left = jnp.where(pick, -jnp.inf, left)
    write(_sel_to_bias(sel))
    miscount = jnp.max(jnp.abs(jnp.sum(sel, axis=0, keepdims=True) - n_sel))

    @pl.when(miscount > 0.0)
    def _():
        write(_sel_to_bias(_topk_mask0(score, blk, n_sel)))


def _flash_update(s, bias, v_dot, m_ref, l_ref, acc_ref):
    if bias is not None:
        s = bias + s
    m_old = m_ref[...]
    m_new = jnp.maximum(m_old, jnp.max(s, axis=0, keepdims=True))
    alpha = jnp.exp2(m_old - m_new)
    p = jnp.exp2(s - m_new)
    if l_ref is not None:
        l_ref[...] = alpha * l_ref[...] + jnp.sum(p, axis=0, keepdims=True)
    acc_ref[...] = alpha * acc_ref[...] + v_dot(p.astype(BF16))
    m_ref[...] = m_new


def _flash_init(m_ref, l_ref, acc_ref):
    m_ref[...] = jnp.full(m_ref.shape, NEG, F32)
    if l_ref is not None:
        l_ref[...] = jnp.zeros(l_ref.shape, F32)
    acc_ref[...] = jnp.zeros(acc_ref.shape, F32)


def _attn_prompt_kernel(qt_ref, kc_ref, vct_ref, ks_ref, vst_ref, kw_ref, vwt_ref, g_ref,
                        o_ref, selb_ref, m_ref, acc_ref, m2_ref, acc2_ref, out_ref, *, tq, tk):
    gi = pl.program_id(1)
    qi = pl.program_id(2)
    q0 = qi * tq
    n_blk = kc_ref.shape[2]
    wide = Q_PER_KV * tq
    qt = qt_ref[0, 0, 0]
    q_rows = qt[0:HEAD_DIM]
    pos = q0 + lax.broadcasted_iota(jnp.int32, (1, tq), 1)
    pos_w = q0 + (lax.broadcasted_iota(jnp.int32, (1, wide), 1) & (tq - 1))

    def gate(branch):
        rows = [g_ref[0, 0, pl.ds((gi * Q_PER_KV + r) * N_BRANCH + branch, 1), :] for r in range(Q_PER_KV)]
        return jnp.concatenate(rows, axis=1)

    blk_w = lax.broadcasted_iota(jnp.int32, (n_blk, wide), 0)
    p = _masked_softmax0(jnp.dot(kc_ref[0, 0], q_rows, preferred_element_type=F32),
                         blk_w * BLOCK + (BLOCK - 1) <= pos_w)
    o_c = jnp.dot(vct_ref[0, 0], p.astype(BF16), preferred_element_type=F32)
    out_ref[...] = gate(0) * o_c
    imp = p[:, 0:tq]
    for r in range(1, Q_PER_KV):
        imp = imp + p[:, r * tq:(r + 1) * tq]
    blk = lax.broadcasted_iota(jnp.int32, (n_blk, tq), 0)
    def write_selb(bias):
        for r in range(Q_PER_KV):
            selb_ref[:, r * tq:(r + 1) * tq] = bias

    _write_selection_bias(write_selb, imp, blk, pos >> BLOCK_SHIFT, n_blk)

    kpos_tile = lax.broadcasted_iota(jnp.int32, (tk, tq), 0)
    bpt = tk // BLOCK
    bias_pad = jnp.zeros((K_AUG - HEAD_DIM - bpt, wide), F32)

    cw = wide // COL_GROUPS
    q_lo = [(h * cw) % tq for h in range(COL_GROUPS)]

    groups = [slice(h * cw, (h + 1) * cw) for h in range(COL_GROUPS)]

    sel_state, win_state = (m_ref, acc_ref), (m2_ref, acc2_ref)

    def tiles_update(tiles):
        staged = []
        for k_ref, vt_ref, state, kt, q_op, bias, key_rows in tiles:
            k_tile = k_ref[0, 0, pl.ds(pl.multiple_of(kt * tk, tk), tk), :]
            rows = [(0, tk) if key_rows is None else key_rows(h) for h in range(COL_GROUPS)]
            scores = [jnp.dot(k_tile[r0:r1], q_op[:, cols], preferred_element_type=F32)
                      for (r0, r1), cols in zip(rows, groups)]
            staged.append((vt_ref[0, 0, kt], state, rows, scores, bias))
        for vt, (m_r, acc_r), rows, scores, bias in staged:
            for h, (s, (r0, r1), cols) in enumerate(zip(scores, rows, groups)):
                group_bias = None if bias is None else bias[r0:r1, q_lo[h]:q_lo[h] + cw]
                v_dot = lambda pr, vt=vt, r0=r0, r1=r1: jnp.dot(vt[:, r0:r1], pr, preferred_element_type=F32)
                _flash_update(s, group_bias, v_dot, m_r.at[:, cols], None, acc_r.at[:, cols])

    def finish(branch, state):
        acc_r = state[1]
        o = acc_r[0:HEAD_DIM, :] / jnp.maximum(acc_r[HEAD_DIM:HEAD_DIM + 1, :], 1e-30)
        out_ref[...] = out_ref[...] + gate(branch) * o

    def q_with_block_bias(kt):
        tile_bias = selb_ref[pl.ds(pl.multiple_of(kt * bpt, bpt), bpt), :]
        return jnp.concatenate([q_rows, jnp.concatenate([tile_bias, bias_pad], axis=0).astype(BF16)], axis=0)

    kt_diag = (q0 + tq - 1) // tk
    _flash_init(m_ref, None, acc_ref)
    _flash_init(m2_ref, None, acc2_ref)

    def sel_body(kt, carry):
        tiles_update([(ks_ref, vst_ref, sel_state, kt, q_with_block_bias(kt), None, None)])
        return carry

    lax.fori_loop(0, kt_diag, sel_body, 0)
    causal = jnp.where(kt_diag * tk + kpos_tile <= pos, 0.0, NEG)
    below_diag = lambda h: (0, q_lo[h] + cw)
    tiles_update([(ks_ref, vst_ref, sel_state, kt_diag, q_with_block_bias(kt_diag), causal, below_diag),
                  (kw_ref, vwt_ref, win_state, kt_diag, qt, causal, below_diag)])

    @pl.when(qi > 0)
    def _():
        older = jnp.where(pos - ((kt_diag - 1) * tk + kpos_tile) < WINDOW, 0.0, NEG)
        tiles_update([(kw_ref, vwt_ref, win_state, kt_diag - 1, qt, older, lambda h: (q_lo[h], tk))])

    finish(1, sel_state)
    finish(2, win_state)

    for pair in range(Q_PER_KV // 2):
        two = jnp.concatenate([out_ref[:, (2 * pair + h) * tq:(2 * pair + h + 1) * tq] for h in range(2)], axis=0)
        o_ref[:, pair * 2 * HEAD_DIM:(pair + 1) * 2 * HEAD_DIM] = two.T.astype(o_ref.dtype)


def _attn_prompt(qt, kc, vct, kas, vst, kaw, vwt, gates, tq, tk):
    b, g, nq, ka, wide = qt.shape
    t = nq * tq
    assert tq & (tq - 1) == 0 and wide == Q_PER_KV * tq and ka == K_AUG
    assert tq == tk == WINDOW and 2 * wide // COL_GROUPS <= tq
    n_blk = kc.shape[2]
    nkt = t // tk
    d = HEAD_DIM
    bg = lambda *tail: (lambda i, j, q: (i, j) + tail)
    kern = functools.partial(_attn_prompt_kernel, tq=tq, tk=tk)
    return pl.pallas_call(
        kern,
        grid=(b, g, nq),
        in_specs=[
            pl.BlockSpec((1, 1, 1, ka, wide), lambda i, j, q: (i, j, q, 0, 0)),
            pl.BlockSpec((1, 1, n_blk, d), bg(0, 0)),
            pl.BlockSpec((1, 1, d, n_blk), bg(0, 0)),
            pl.BlockSpec((1, 1, t, ka), bg(0, 0)),
            pl.BlockSpec((1, 1, nkt, V_AUG, tk), bg(0, 0, 0)),
            pl.BlockSpec((1, 1, t, ka), bg(0, 0)),
            pl.BlockSpec((1, 1, nkt, V_AUG, tk), bg(0, 0, 0)),
            pl.BlockSpec((1, 1, gates.shape[2], tq), lambda i, j, q: (i, q, 0, 0)),
        ],
        out_specs=pl.BlockSpec((tq, Q_PER_KV * d), lambda i, j, q: (i * nq + q, j)),
        out_shape=jax.ShapeDtypeStruct((b * t, ATTN_WIDTH), BF16),
        scratch_shapes=[
            pltpu.VMEM((n_blk, wide), F32),
            pltpu.VMEM((1, wide), F32),
            pltpu.VMEM((V_AUG, wide), F32),
            pltpu.VMEM((1, wide), F32),
            pltpu.VMEM((V_AUG, wide), F32),
            pltpu.VMEM((d, wide), F32),
        ],
        compiler_params=_cparams(("parallel", "parallel", "parallel")),
        name="attn_prompt",
    )(qt, kc, vct, kas, vst, kaw, vwt, gates)


def _attn_sample_kernel(pt_ref, *refs, pages_per_step, past_len, n_blk_valid):
    page_refs = refs[:pages_per_step]
    (qbd_ref, kc_ref, vc_ref, new_s_ref, win_ref, new_w_ref, gt_ref, pos_ref,
     o_ref, selb_ref, m_ref, l_ref, acc_ref, out_ref) = refs[pages_per_step:]
    step = pl.program_id(1)
    n_steps = pl.num_programs(1)
    ncol = qbd_ref.shape[2]
    kvw = qbd_ref.shape[1]
    qbd = qbd_ref[0]
    pos = pos_ref[...]
    group_cols = ncol // Q_PER_KV

    def tile_rows(rows, bias):
        k, v = rows[:, :kvw].astype(BF16), rows[:, kvw:].astype(BF16)
        s = jnp.dot(k, qbd, preferred_element_type=F32)
        v_dot = lambda p: lax.dot_general(v, p, TN_DIMS, preferred_element_type=F32)
        _flash_update(s, bias, v_dot, m_ref, l_ref, acc_ref)

    def tile_chan(page, bias):
        kt, vt = page[:kvw].astype(BF16), page[kvw:].astype(BF16)
        half = page.shape[1] // 2
        s = jnp.concatenate([lax.dot_general(kt[:, :half], qbd, TN_DIMS, preferred_element_type=F32),
                             lax.dot_general(kt[:, half:], qbd, TN_DIMS, preferred_element_type=F32)], axis=0)
        v_dot = lambda p: (jnp.dot(vt[:, :half], p[:half], preferred_element_type=F32)
                           + jnp.dot(vt[:, half:], p[half:], preferred_element_type=F32))
        _flash_update(s, bias, v_dot, m_ref, l_ref, acc_ref)

    def window_bias(n_keys, p0):
        delta = pos - (p0 + lax.broadcasted_iota(jnp.int32, (n_keys, ncol), 0))
        return jnp.where((delta >= 0) & (delta < WINDOW), 0.0, NEG)

    @pl.when(step == 0)
    def _():
        n_blk = kc_ref.shape[1]
        blk = lax.broadcasted_iota(jnp.int32, (n_blk, ncol), 0)
        cmask = (blk * BLOCK + (BLOCK - 1) <= pos) & (blk < n_blk_valid)
        s = jnp.dot(kc_ref[0], qbd, preferred_element_type=F32)
        p = _masked_softmax0(s, cmask)
        o_c = lax.dot_general(vc_ref[0], p.astype(BF16), TN_DIMS, preferred_element_type=F32)
        out_ref[...] = gt_ref[0, 0:1, :] * o_c
        imp = p
        for r in range(1, Q_PER_KV):
            imp = imp + pltpu.roll(p, r * group_cols, 1)
        def write_selb(bias):
            selb_ref[...] = bias

        _write_selection_bias(write_selb, imp, blk, pos >> BLOCK_SHIFT, n_blk_valid)

        _flash_init(m_ref, l_ref, acc_ref)
        n_win = win_ref.shape[2]
        tile_rows(new_w_ref[0], window_bias(new_w_ref.shape[1], past_len))
        tile_chan(win_ref[0], window_bias(n_win, past_len - n_win))
        o_w = acc_ref[...] / jnp.maximum(l_ref[...], 1e-30)
        out_ref[...] = out_ref[...] + gt_ref[0, 2:3, :] * o_w

        _flash_init(m_ref, l_ref, acc_ref)
        n_new = new_s_ref.shape[1]
        kpos = past_len + lax.broadcasted_iota(jnp.int32, (n_new, ncol), 0)
        chosen = jnp.broadcast_to(selb_ref[pl.ds(past_len // BLOCK, 1), :], (n_new, ncol))
        tile_rows(new_s_ref[0], jnp.where(kpos <= pos, chosen, NEG))

    page_rows = page_refs[0].shape[2]
    step_keys = pages_per_step * page_rows
    step_blocks = step_keys // BLOCK
    keys = jnp.concatenate([page_refs[j][0] for j in range(pages_per_step)], axis=1)
    chosen = jnp.concatenate(
        [jnp.broadcast_to(selb_ref[pl.ds(step * step_blocks + i, 1), :], (BLOCK, ncol)) for i in range(step_blocks)],
        axis=0)
    tile_chan(keys, chosen)

    @pl.when(step == n_steps - 1)
    def _():
        o_s = acc_ref[...] / jnp.maximum(l_ref[...], 1e-30)
        o_ref[0] = out_ref[...] + gt_ref[0, 1:2, :] * o_s


def _attn_sample(pool_t, page0, page_table, qbd, kc, vc, new_s, win_t, win0, new_w, gt, pos,
                 pages_per_step=PAGES_PER_STEP):
    _, c, page_rows = pool_t.shape
    b, n_pages = page_table.shape
    pages_per_step = min(pages_per_step, n_pages)
    _, kvw, ncol = qbd.shape
    n_blk = kc.shape[1]
    past_len = n_pages * page_rows
    per_b = lambda shape: pl.BlockSpec((1,) + shape, lambda i, s, pt: (i,) + (0,) * len(shape))
    grid_spec = pltpu.PrefetchScalarGridSpec(
        num_scalar_prefetch=1,
        grid=(b, n_pages // pages_per_step),
        in_specs=_page_specs(c, page_rows, pages_per_step, page0) + [
            per_b((kvw, ncol)),
            per_b((n_blk, kvw)),
            per_b((n_blk, kvw)),
            per_b(new_s.shape[1:]),
            pl.BlockSpec((1,) + win_t.shape[1:], lambda i, s, pt: (win0 + i, 0, 0)),
            per_b(new_w.shape[1:]),
            per_b((N_BRANCH, ncol)),
            pl.BlockSpec((1, ncol), lambda i, s, pt: (0, 0)),
        ],
        out_specs=per_b((kvw, ncol)),
        scratch_shapes=[
            pltpu.VMEM((n_blk, ncol), F32),
            pltpu.VMEM((1, ncol), F32),
            pltpu.VMEM((1, ncol), F32),
            pltpu.VMEM((kvw, ncol), F32),
            pltpu.VMEM((kvw, ncol), F32),
        ],
    )
    kern = functools.partial(_attn_sample_kernel, pages_per_step=pages_per_step, past_len=past_len,
                             n_blk_valid=past_len // BLOCK + 1)
    return pl.pallas_call(
        kern,
        grid_spec=grid_spec,
        out_shape=jax.ShapeDtypeStruct((b, kvw, ncol), F32),
        compiler_params=_cparams(("parallel", "arbitrary")),
        name="attn_sample",
    )(page_table, *([pool_t] * pages_per_step), qbd, kc, vc, new_s, win_t, new_w, gt, pos)


def _outproj_kernel(a_ref, c_ref, x_ref, w_ref, g_ref, b_ref, h_ref, *, alpha):
    aw = a_ref.shape[1]
    rows = a_ref.shape[0]
    part = rows // OUTPROJ_ROW_PARTS if rows % (OUTPROJ_ROW_PARTS * OUTPROJ_MIN_PART) == 0 else rows
    for r0 in range(0, rows, part):
        rs = slice(r0, r0 + part)
        mix = jnp.dot(a_ref[rs, :], w_ref[:aw, :], preferred_element_type=F32)
        mix = mix + jnp.dot(c_ref[rs, :], w_ref[aw:, :], preferred_element_type=F32)
        h_ref[rs, :] = _layer_norm(alpha * x_ref[rs, :] + mix, g_ref[...], b_ref[...])


def _outproj_ln(attn, conv, x, w_out, layer, g, b, alpha, tm):
    n, d = x.shape
    row = lambda w: pl.BlockSpec((tm, w), lambda i: (i, 0))
    const = lambda shape: pl.BlockSpec(shape, lambda i: (0, 0))
    w_spec = pl.BlockSpec((None,) + w_out.shape[1:], lambda i: (layer, 0, 0))
    return pl.pallas_call(
        functools.partial(_outproj_kernel, alpha=alpha),
        grid=(n // tm,),
        in_specs=[row(attn.shape[1]), row(conv.shape[1]), row(d), w_spec, const((1, d)), const((1, d))],
        out_specs=row(d),
        out_shape=jax.ShapeDtypeStruct((n, d), F32),
        compiler_params=_cparams(("parallel",)),
        name="outproj_ln",
    )(attn, conv, x, w_out, g, b)


def _mlp_kernel(h_ref, w1_ref, w2_ref, g_ref, b_ref, y_ref, acc_ref, hb_ref, *, alpha):
    f = pl.program_id(1)

    @pl.when(f == 0)
    def _():
        hb_ref[...] = h_ref[...].astype(BF16)
        acc_ref[...] = jnp.zeros(acc_ref.shape, F32)

    a = jnp.dot(hb_ref[...], w1_ref[...], preferred_element_type=F32)
    a = jnp.square(jnp.maximum(a, 0.0)).astype(BF16)
    acc_ref[...] += jnp.dot(a, w2_ref[...], preferred_element_type=F32)

    @pl.when(f == pl.num_programs(1) - 1)
    def _():
        y_ref[...] = _layer_norm(alpha * h_ref[...] + acc_ref[...], g_ref[...], b_ref[...])


def _mlp_ln(h, w1, w2, layer, g, b, alpha, tm, tf):
    n, d = h.shape
    dff = w1.shape[2]
    return pl.pallas_call(
        functools.partial(_mlp_kernel, alpha=alpha),
        grid=(n // tm, dff // tf),
        in_specs=[
            pl.BlockSpec((tm, d), lambda i, f: (i, 0)),
            pl.BlockSpec((None, d, tf), lambda i, f: (layer, 0, f)),
            pl.BlockSpec((None, tf, d), lambda i, f: (layer, f, 0)),
            pl.BlockSpec((1, d), lambda i, f: (0, 0)),
            pl.BlockSpec((1, d), lambda i, f: (0, 0)),
        ],
        out_specs=pl.BlockSpec((tm, d), lambda i, f: (i, 0)),
        out_shape=jax.ShapeDtypeStruct((n, d), F32),
        scratch_shapes=[pltpu.VMEM((tm, d), F32), pltpu.VMEM((tm, d), BF16)],
        compiler_params=_cparams(("parallel", "arbitrary")),
        name="mlp_ln",
    )(h, w1, w2, g, b)


def _split_kv(kv, b, t):
    return kv.reshape(b, t, 2, KV_HEADS, HEAD_DIM)


def _prompt_layer(x, p, b, t, alpha, tm, tq, tk):
    w_c = jnp.tile(p["w_cmp"].T, (1, tm // BLOCK))
    kvt_c, kvt_s, kvt_w, vst, vwt, kas, kaw, qt, gates, cmp, conv, zlast = _proj_prompt(
        x, p["w_in_t"], p["layer"], w_c, p["conv_w"], b, t, tm, tq, tk)
    n_blk = t // BLOCK
    cmp = cmp.reshape(b, n_blk, 2, KV_HEADS, HEAD_DIM).astype(BF16)
    kc = cmp[:, :, 0].transpose(0, 2, 1, 3)
    vct = cmp[:, :, 1].transpose(0, 2, 3, 1)
    attn = _attn_prompt(qt, kc, vct, kas, vst, kaw, vwt, gates, tq, tk)
    h = _outproj_ln(attn, conv, x, p["w_out"], p["layer"], p["ln1_g"], p["ln1_b"], alpha, tm)
    y = _mlp_ln(h, p["w_mlp1"], p["w_mlp2"], p["layer"], p["ln2_g"], p["ln2_b"], alpha, min(MLP_ROWS, b * t), MLP_COLS)
    conv_state = zlast.reshape(b, t // tm, SUBLANES, -1)[:, -1, SUBLANES - (CONV_W - 1):]
    keep = min(WINDOW, t)
    rows_major = lambda kvt: kvt.reshape(b, 2, KV_HEADS, HEAD_DIM, t).transpose(0, 4, 1, 2, 3)
    return (y, rows_major(kvt_c), rows_major(kvt_s), rows_major(kvt_w)[:, t - keep:], conv_state)


def _sample_layer(x, p, pool_c, pool_s, page0, win_t, win0, cache_win, state_conv, page_table, db, dt, alpha):
    n = db * dt
    page_rows = pool_c.shape[2]
    n_pages = page_table.shape[1]
    past_len = n_pages * page_rows
    c = 2 * KV_WIDTH
    zp = jnp.pad(state_conv, ((0, 0), (SUBLANES - (CONV_W - 1), 0), (0, 0))).reshape(n, -1)
    q, kv_c, kv_s, kv_w, gates, conv, z = _proj_sample(x, zp, p["w_in_t"], p["layer"], p["conv_w"])
    conv_state = jnp.concatenate([state_conv, z.reshape(db, dt, -1)], axis=1)[:, -(CONV_W - 1):]

    cmp_past = _cmp_pages(pool_c, page0, page_table, p["w_cmp_t"])
    tail = jnp.pad(kv_c.reshape(db, dt, c), ((0, 0), (0, SUBLANES * BLOCK - dt), (0, 0)))
    cmp_tail = _cmp_rows(tail, p["w_cmp"], SUBLANES)
    cmp_all = jnp.concatenate([cmp_past, cmp_tail], axis=1).astype(BF16)
    kc, vc = cmp_all[..., :KV_WIDTH], cmp_all[..., KV_WIDTH:]

    q5 = q.reshape(db, dt, KV_HEADS, Q_PER_KV, HEAD_DIM).transpose(0, 2, 4, 3, 1)
    eye = jnp.eye(KV_HEADS, dtype=q.dtype)
    qbd = (q5[:, :, :, :, None, :] * eye[None, :, None, None, :, None]).reshape(db, KV_WIDTH, N_HEADS * dt)
    gt = gates[:, :N_HEADS * N_BRANCH].reshape(db, dt, KV_HEADS, Q_PER_KV, N_BRANCH)
    gt = gt.transpose(0, 4, 3, 2, 1).reshape(db, N_BRANCH, N_HEADS * dt)
    pos = jnp.tile(past_len + jnp.arange(dt, dtype=jnp.int32), N_HEADS).reshape(1, N_HEADS * dt)
    new_rows = lambda kv: jnp.pad(kv.reshape(db, dt, c), ((0, 0), (0, 2 * SUBLANES - dt), (0, 0)))
    o = _attn_sample(pool_s, page0, page_table, qbd, kc, vc, new_rows(kv_s), win_t, win0, new_rows(kv_w), gt, pos)
    o = o.reshape(db, KV_HEADS, HEAD_DIM, Q_PER_KV, KV_HEADS, dt)
    o = jnp.stack([o[:, g, :, :, g, :] for g in range(KV_HEADS)], axis=1)
    attn = o.transpose(0, 4, 1, 3, 2).reshape(n, ATTN_WIDTH).astype(BF16)

    h = _outproj_ln(attn, conv, x, p["w_out"], p["layer"], p["ln1_g"], p["ln1_b"], alpha, n)
    y = _mlp_ln(h, p["w_mlp1"], p["w_mlp2"], p["layer"], p["ln2_g"], p["ln2_b"], alpha, n, 1024)
    win_keep = cache_win.shape[1]
    new_win = jnp.concatenate([cache_win, _split_kv(kv_w, db, dt)], axis=1)[:, -win_keep:]
    return y, _split_kv(kv_c, db, dt), _split_kv(kv_s, db, dt), new_win, conv_state


def kernel(x_prompt, x_sample, cache_cmp, cache_slc, cache_win, state_conv, page_table, w_in, w_cmp_k, w_cmp_v,
           conv_w, w_out, ln1_g, ln1_b, w_mlp1, w_mlp2, ln2_g, ln2_b):
    depth = w_in.shape[0]
    b, t, d = x_prompt.shape
    db, dt, _ = x_sample.shape
    alpha = (2.0 * depth) ** 0.25
    tm = min(512, t)
    tq, tk = min(512, t), min(512, t)

    n_pool, page_rows = cache_cmp.shape[1:3]
    chan_major = lambda c: c.transpose(0, 1, 3, 4, 5, 2).reshape(c.shape[0] * c.shape[1], -1, c.shape[2])
    pool_c, pool_s, win_t = chan_major(cache_cmp), chan_major(cache_slc), chan_major(cache_win)

    w_out_b, w_mlp1_b, w_mlp2_b = w_out.astype(BF16), w_mlp1.astype(BF16), w_mlp2.astype(BF16)
    yp = x_prompt.reshape(b * t, d)
    ys = x_sample.reshape(db * dt, d)
    outs = [[] for _ in range(8)]
    w_in_t = w_in.transpose(0, 2, 1).astype(BF16)
    for l in range(depth):
        w_cmp = jnp.concatenate([jnp.tile(w_cmp_k[l], (1, KV_HEADS)), jnp.tile(w_cmp_v[l], (1, KV_HEADS))], axis=1)
        p = {
            "w_in_t": w_in_t,
            "w_cmp": w_cmp,
            "w_cmp_t": jnp.tile(w_cmp.T, (1, page_rows // BLOCK)),
            "conv_w": conv_w[l],
            "layer": l,
            "w_out": w_out_b,
            "ln1_g": ln1_g[l].reshape(1, d), "ln1_b": ln1_b[l].reshape(1, d),
            "w_mlp1": w_mlp1_b, "w_mlp2": w_mlp2_b,
            "ln2_g": ln2_g[l].reshape(1, d), "ln2_b": ln2_b[l].reshape(1, d),
        }
        yp, c1, s1, w1, v1 = _prompt_layer(yp, p, b, t, alpha, tm, tq, tk)
        ys, c2, s2, w2, v2 = _sample_layer(ys, p, pool_c, pool_s, l * n_pool, win_t, l * db, cache_win[l],
                                           state_conv[l], page_table, db, dt, alpha)
        for lst, v in zip(outs, (c1, s1, w1, v1, c2, s2, w2, v2)):
            lst.append(v)
    stacked = [jnp.stack(o) for o in outs]
    return (yp.reshape(b, t, d), ys.reshape(db, dt, d), *stacked)
```
